```python
import math
import jax, jax.numpy as jnp
from jax import lax
import numpy as np

D_MODEL = 2048
BATCH = 4
SEQ = 2048
DEPTH = 2
DEC_BATCH = 8
DEC_SEQ = 1
PAST_LEN = 16384
PAGE_SIZE = 128

HEAD_DIM = 128
MOBA_HEADS = 6
FOX_HEADS = 6
MEM_HEADS = 4
MOBA_W = MOBA_HEADS * HEAD_DIM
FOX_W = FOX_HEADS * HEAD_DIM
MEM_W = MEM_HEADS * HEAD_DIM
N_BRANCH = 3
MOBA_BLOCK = 256
MOBA_TOPK = 3
MOBA_QCHUNK = 64
FOX_QBLOCK = 128
N_MEM = 256
D_FF = ((8 * D_MODEL // 3 + 127) // 128) * 128
CONV_W = 3
ROPE_THETA = 10000.0
NORM_EPS = 1e-6
FORGET_BIAS_INIT = 3.0
NEG = -1e30
IN_SPLITS = (MOBA_W, MOBA_W, MOBA_W, FOX_W, FOX_W, FOX_W, FOX_HEADS, MEM_W, N_BRANCH * D_MODEL)
IN_COLS = sum(IN_SPLITS)

kernel_name = 'hybrid_moba_fox_memory_decoder_step'


def rmsnorm(x, g):
    xf = x.astype(jnp.float32)
    y = xf * lax.rsqrt(jnp.mean(xf * xf, axis=-1, keepdims=True) + NORM_EPS)
    return (y * g.astype(jnp.float32)).astype(x.dtype)


def rope(x, pos):
    half = HEAD_DIM // 2
    inv_freq = ROPE_THETA ** (-jnp.arange(half, dtype=jnp.float32) / half)
    ang = pos.astype(jnp.float32)[:, None] * inv_freq[None, :]
    cos = jnp.cos(ang)[None, :, None, :]
    sin = jnp.sin(ang)[None, :, None, :]
    xf = x.astype(jnp.float32)
    x1, x2 = xf[..., :half], xf[..., half:]
    return jnp.concatenate([x1 * cos - x2 * sin, x2 * cos + x1 * sin], axis=-1).astype(x.dtype)


def split_in_proj(z):
    B, S, _ = z.shape
    offs = np.cumsum(IN_SPLITS)[:-1].tolist()
    qm, km, vm, qf, kf, vf, fl, qc, gl = jnp.split(z, offs, axis=-1)
    heads = lambda t, h: t.reshape(B, S, h, HEAD_DIM)
    return (heads(qm, MOBA_HEADS), heads(km, MOBA_HEADS), heads(vm, MOBA_HEADS),
            heads(qf, FOX_HEADS), heads(kf, FOX_HEADS), heads(vf, FOX_HEADS),
            fl, heads(qc, MEM_HEADS), gl.reshape(B, S, N_BRANCH, D_MODEL))


def moba_attend(q, k, v, q_pos0):
    B, Sq, H, _ = q.shape
    L = k.shape[1]
    nb = -(-L // MOBA_BLOCK)
    pad = ((0, 0), (0, nb * MOBA_BLOCK - L), (0, 0), (0, 0))
    kb = jnp.pad(k, pad).reshape(B, nb, MOBA_BLOCK, H, HEAD_DIM)
    vb = jnp.pad(v, pad).reshape(B, nb, MOBA_BLOCK, H, HEAD_DIM)
    k_mean = jnp.mean(kb.astype(jnp.float32), axis=2)
    topk = min(MOBA_TOPK, nb)
    qc = MOBA_QCHUNK if Sq % MOBA_QCHUNK == 0 else Sq
    n_chunks = Sq // qc
    q_chunks = q.reshape(B, n_chunks, qc, H, HEAD_DIM).swapaxes(0, 1)
    bi = jnp.arange(B)[:, None, None]
    hi = jnp.arange(H)[None, :, None]
    blk_ids = jnp.arange(nb)
    in_blk = jnp.arange(MOBA_BLOCK)
    scale = HEAD_DIM ** -0.5

    def one_chunk(args):
        qb, ci = args
        t = q_pos0 + ci * qc + jnp.arange(qc)
        own = t // MOBA_BLOCK
        bscore = jnp.einsum('bqhd,bnhd->bhqn', qb.astype(jnp.float32), k_mean)
        bscore = jnp.where(blk_ids[None, :] < own[:, None], bscore, NEG)
        _, sel = lax.top_k(bscore, topk)
        blocks = [sel[..., j] for j in range(topk)] + [jnp.broadcast_to(own[None, None, :], (B, H, qc))]
        masks = [(j < own)[:, None] for j in range(topk)]
        masks.append(own[:, None] * MOBA_BLOCK + in_blk[None, :] <= t[:, None])
        scores = []
        for blk, m in zip(blocks, masks):
            kg = kb[bi, blk, :, hi]
            s = jnp.einsum('bqhd,bhqkd->bhqk', qb, kg).astype(jnp.float32) * scale
            scores.append(jnp.where(m, s, NEG))
        p = jax.nn.softmax(jnp.concatenate(scores, axis=-1), axis=-1).astype(v.dtype)
        p = p.reshape(B, H, qc, topk + 1, MOBA_BLOCK)
        return sum(jnp.einsum('bhqk,bhqkd->bqhd', p[:, :, :, j], vb[bi, blk, :, hi])
                   for j, blk in enumerate(blocks))

    out = lax.map(one_chunk, (q_chunks, jnp.arange(n_chunks)))
    return out.swapaxes(0, 1).reshape(B, Sq, H, HEAD_DIM)


def fox_attend(q, k, v, c_q, c_k, q_pos0):
    B, Sq, H, _ = q.shape
    L = k.shape[1]
    qb_len = FOX_QBLOCK if Sq % FOX_QBLOCK == 0 else Sq
    n_blocks = Sq // qb_len
    q_blocks = q.reshape(B, n_blocks, qb_len, H, HEAD_DIM).swapaxes(0, 1)
    cq_blocks = c_q.reshape(B, n_blocks, qb_len, H).swapaxes(0, 1)
    ck = c_k.swapaxes(1, 2)
    k_pos = jnp.arange(L)
    scale = HEAD_DIM ** -0.5

    def one_block(args):
        qb, cqb, bidx = args
        t = q_pos0 + bidx * qb_len + jnp.arange(qb_len)
        s = jnp.einsum('bqhd,bkhd->bhqk', qb, k).astype(jnp.float32) * scale
        s = s + cqb.swapaxes(1, 2)[..., None] - ck[:, :, None, :]
        s = jnp.where(k_pos[None, :] <= t[:, None], s, NEG)
        p = jax.nn.softmax(s, axis=-1).astype(v.dtype)
        return jnp.einsum('bhqk,bkhd->bqhd', p, v)

    out = lax.map(one_block, (q_blocks, cq_blocks, jnp.arange(n_blocks)))
    return out.swapaxes(0, 1).reshape(B, Sq, H, HEAD_DIM)


def mem_attend(q, mk, mv):
    s = jnp.einsum('bqhd,bmhd->bhqm', q, mk.astype(q.dtype)).astype(jnp.float32) * HEAD_DIM ** -0.5
    p = jax.nn.softmax(s, axis=-1).astype(q.dtype)
    return jnp.einsum('bhqm,bmhd->bqhd', p, mv.astype(q.dtype))


def mem_kv_proj(mem, g_mem, w_mem_kv):
    B = mem.shape[0]
    return (rmsnorm(mem, g_mem) @ w_mem_kv).reshape(B, N_MEM, 2, MEM_HEADS, HEAD_DIM)


def conv_ffn(h, conv_prev, w_up, w_conv, b_conv, w_down):
    S = h.shape[1]
    a, b = jnp.split(h @ w_up, 2, axis=-1)
    a_ext = jnp.concatenate([conv_prev.astype(a.dtype), a], axis=1)
    a_conv = b_conv + sum(a_ext[:, i:i + S] * w_conv[i] for i in range(CONV_W))
    out = (jax.nn.silu(a_conv) * b) @ w_down
    return out, a_ext[:, S:]


def decoder_layer(x, pos0, mkv, past_moba, past_fox, past_logf, conv_prev,
                  g_attn, w_in, b_f, w_br_moba, w_br_fox, w_br_mem, w_out,
                  g_ffn, w_up, w_conv, b_conv, w_down):
    B, S, _ = x.shape
    h = rmsnorm(x, g_attn)
    qm, km, vm, qf, kf, vf, fl, qc, gl = split_in_proj(h @ w_in)
    pos = pos0 + jnp.arange(S)
    qm = rope(qm, pos)
    km = rope(km, pos)
    logf = jax.nn.log_sigmoid(fl.astype(jnp.float32) + b_f.astype(jnp.float32))
    c_local = jnp.cumsum(logf, axis=1)
    if past_moba is None:
        km_all, vm_all, kf_all, vf_all = km, vm, kf, vf
        c_q = c_local
        c_k = c_local
    else:
        km_all = jnp.concatenate([past_moba[:, :, 0].astype(km.dtype), km], axis=1)
        vm_all = jnp.concatenate([past_moba[:, :, 1].astype(vm.dtype), vm], axis=1)
        kf_all = jnp.concatenate([past_fox[:, :, 0].astype(kf.dtype), kf], axis=1)
        vf_all = jnp.concatenate([past_fox[:, :, 1].astype(vf.dtype), vf], axis=1)
        c_past = jnp.cumsum(past_logf.astype(jnp.float32), axis=1)
        c_q = c_past[:, -1:] + c_local
        c_k = jnp.concatenate([c_past, c_q], axis=1)
    o_m = moba_attend(qm, km_all, vm_all, pos0)
    o_f = fox_attend(qf, kf_all, vf_all, c_q, c_k, pos0)
    o_c = mem_attend(qc, mkv[:, :, 0], mkv[:, :, 1])
    gates = jax.nn.sigmoid(gl.astype(jnp.float32)).astype(x.dtype)
    merged = (gates[:, :, 0] * (o_m.reshape(B, S, MOBA_W) @ w_br_moba)
              + gates[:, :, 1] * (o_f.reshape(B, S, FOX_W) @ w_br_fox)
              + gates[:, :, 2] * (o_c.reshape(B, S, MEM_W) @ w_br_mem))
    x = x + merged @ w_out
    f, conv_state = conv_ffn(rmsnorm(x, g_ffn), conv_prev, w_up, w_conv, b_conv, w_down)
    x = x + f
    return (x, jnp.stack([km, vm], axis=2), jnp.stack([kf, vf], axis=2), logf.astype(x.dtype), conv_state)


def setup_inputs(seed: int = 0) -> dict:
    key = jax.random.key(seed)
    ks = jax.random.split(key, 24)
    f32 = jnp.float32
    n_pages = PAST_LEN // PAGE_SIZE
    n_phys = (5 * DEC_BATCH * n_pages + 3) // 4
    nrm = lambda k, shape, s: jax.random.normal(k, shape, f32) * s
    page_table = jax.random.permutation(ks[0], n_phys)[:DEC_BATCH * n_pages]
    page_table = page_table.reshape(DEC_BATCH, n_pages).astype(jnp.int32)
    return {
        'x_prompt': nrm(ks[1], (BATCH, SEQ, D_MODEL), 1.0),
        'x_sample': nrm(ks[2], (DEC_BATCH, DEC_SEQ, D_MODEL), 1.0),
        'cache_moba_kv': nrm(ks[3], (DEPTH, n_phys, PAGE_SIZE, 2, MOBA_HEADS, HEAD_DIM), 1.0),
        'cache_fox_kv': nrm(ks[4], (DEPTH, n_phys, PAGE_SIZE, 2, FOX_HEADS, HEAD_DIM), 1.0),
        'cache_fox_logf': jax.nn.log_sigmoid(FORGET_BIAS_INIT + nrm(ks[5], (DEPTH, n_phys, PAGE_SIZE, FOX_HEADS), 1.0)),
        'cache_mem_kv': nrm(ks[6], (DEPTH, DEC_BATCH, N_MEM, 2, MEM_HEADS, HEAD_DIM), 1.0),
        'state_ffn_conv': nrm(ks[7], (DEPTH, DEC_BATCH, CONV_W - 1, D_FF), 1.0),
        'page_table': page_table,
        'mem_prompt': nrm(ks[8], (BATCH, N_MEM, D_MODEL), 1.0),
        'g_attn': 1.0 + nrm(ks[9], (DEPTH, D_MODEL), 0.05),
        'w_in': nrm(ks[10], (DEPTH, D_MODEL, IN_COLS), D_MODEL ** -0.5),
        'b_f': FORGET_BIAS_INIT + nrm(ks[11], (DEPTH, FOX_HEADS), 0.5),
        'w_br_moba': nrm(ks[12], (DEPTH, MOBA_W, D_MODEL), MOBA_W ** -0.5),
        'w_br_fox': nrm(ks[13], (DEPTH, FOX_W, D_MODEL), FOX_W ** -0.5),
        'w_br_mem': nrm(ks[14], (DEPTH, MEM_W, D_MODEL), MEM_W ** -0.5),
        'w_out': nrm(ks[15], (DEPTH, D_MODEL, D_MODEL), D_MODEL ** -0.5),
        'g_mem': 1.0 + nrm(ks[16], (DEPTH, D_MODEL), 0.05),
        'w_mem_kv': nrm(ks[17], (DEPTH, D_MODEL, 2 * MEM_W), D_MODEL ** -0.5),
        'g_ffn': 1.0 + nrm(ks[18], (DEPTH, D_MODEL), 0.05),
        'w_up': nrm(ks[19], (DEPTH, D_MODEL, 2 * D_FF), D_MODEL ** -0.5),
        'w_conv': nrm(ks[20], (DEPTH, CONV_W, D_FF), CONV_W ** -0.5),
        'b_conv': nrm(ks[21], (DEPTH, D_FF), 0.02),
        'w_down': nrm(ks[22], (DEPTH, D_FF, D_MODEL), D_FF ** -0.5),
        'g_final': 1.0 + nrm(ks[23], (D_MODEL,), 0.05),
    }


def reference(x_prompt, x_sample, cache_moba_kv, cache_fox_kv, cache_fox_logf, cache_mem_kv,
              state_ffn_conv, page_table, mem_prompt, g_attn, w_in, b_f, w_br_moba, w_br_fox,
              w_br_mem, w_out, g_mem, w_mem_kv, g_ffn, w_up, w_conv, b_conv, w_down, g_final):
    B = x_prompt.shape[0]
    DB = x_sample.shape[0]
    past_len = page_table.shape[1] * cache_moba_kv.shape[2]
    xp, xs = x_prompt, x_sample
    p_moba, p_fox, p_logf, p_mem, p_conv = [], [], [], [], []
    s_moba, s_fox, s_logf, s_conv = [], [], [], []
    for l in range(DEPTH):
        lw = (g_attn[l], w_in[l], b_f[l], w_br_moba[l], w_br_fox[l], w_br_mem[l], w_out[l],
              g_ffn[l], w_up[l], w_conv[l], b_conv[l], w_down[l])
        mkv = mem_kv_proj(mem_prompt, g_mem[l], w_mem_kv[l])
        conv0 = jnp.zeros((B, CONV_W - 1, D_FF), xp.dtype)
        xp, mk, fk, lf, cs = decoder_layer(xp, 0, mkv, None, None, None, conv0, *lw)
        p_moba.append(mk)
        p_fox.append(fk)
        p_logf.append(lf)
        p_mem.append(mkv)
        p_conv.append(cs)
        pm = cache_moba_kv[l, page_table].reshape(DB, past_len, 2, MOBA_HEADS, HEAD_DIM)
        pf = cache_fox_kv[l, page_table].reshape(DB, past_len, 2, FOX_HEADS, HEAD_DIM)
        plf = cache_fox_logf[l, page_table].reshape(DB, past_len, FOX_HEADS)
        xs, mk, fk, lf, cs = decoder_layer(xs, past_len, cache_mem_kv[l], pm, pf, plf, state_ffn_conv[l], *lw)
        s_moba.append(mk)
        s_fox.append(fk)
        s_logf.append(lf)
        s_conv.append(cs)
    y_prompt = rmsnorm(xp, g_final)
    y_sample = rmsnorm(xs, g_final)
    return (y_prompt, y_sample,
            jnp.stack(p_moba), jnp.stack(p_fox), jnp.stack(p_logf), jnp.stack(p_mem), jnp.stack(p_conv),
            jnp.stack(s_moba), jnp.stack(s_fox), jnp.stack(s_logf), jnp.stack(s_conv))
```

```python
import functools

import jax
import jax.numpy as jnp
from jax import lax
from jax.experimental import pallas as pl
from jax.experimental.pallas import tpu as pltpu

F32 = jnp.float32
BF16 = jnp.bfloat16

HEAD_DIM = 128
MOBA_HEADS = 6
FOX_HEADS = 6
MEM_HEADS = 4
MOBA_BLOCK = 256
MOBA_TOPK = 3
CONV_W = 3
ROPE_THETA = 10000.0
NORM_EPS = 1e-6
NEG = -1e30

LANES = 128
SUBLANES = 8
HEAD_TILE = MOBA_HEADS * HEAD_DIM
VMEM_LIMIT = 56 * 1024 * 1024
SCALE = HEAD_DIM ** -0.5

_NT = (((1,), (1,)), ((), ()))


def _params(sem):
    return pltpu.CompilerParams(dimension_semantics=sem, vmem_limit_bytes=VMEM_LIMIT)


def _rms(x, g):
    return x * lax.rsqrt(jnp.mean(x * x, axis=-1, keepdims=True) + NORM_EPS) * g


def _log_sigmoid(x):
    return jnp.minimum(x, 0.0) - jnp.log1p(jnp.exp(-jnp.abs(x)))


def _split3(v):
    hi = v.astype(BF16)
    r1 = v - hi.astype(F32)
    mid = r1.astype(BF16)
    lo = (r1 - mid.astype(F32)).astype(BF16)
    return hi, mid, lo


def _dot(a, b):
    return jnp.dot(a, b, preferred_element_type=F32)


def _dot_nt(a, b):
    return lax.dot_general(a, b, _NT, preferred_element_type=F32)


def _cumsum_rows(v, carry, blk=256):
    tm = v.shape[0]
    blk = min(blk, tm)
    r = lax.broadcasted_iota(jnp.int32, (blk, blk), 0)
    c = lax.broadcasted_iota(jnp.int32, (blk, blk), 1)
    tri = (c <= r).astype(BF16)
    outs = []
    for s in range(0, tm, blk):
        hi, mid, lo = _split3(v[s:s + blk])
        cs = _dot(tri, hi) + _dot(tri, mid) + _dot(tri, lo) + carry
        outs.append(cs)
        carry = cs[blk - 1:blk, :]
    return jnp.concatenate(outs, axis=0), carry


def _rope(acc, cos, sin_signed):
    outs = []
    for hh in range(acc.shape[1] // HEAD_DIM):
        xh = acc[:, hh * HEAD_DIM:(hh + 1) * HEAD_DIM]
        outs.append(xh * cos + pltpu.roll(xh, HEAD_DIM // 2, 1) * sin_signed)
    return jnp.concatenate(outs, axis=1)


def _in_proj_kernel(x_ref, g_ref, w_ref, wfl_ref, bf_ref, cos_ref, sin_ref,
                    q_ref, mkv_ref, fkv_ref, gate_ref, logf_ref, c_ref,
                    h_scr, carry_scr, *, tiles_per_seq, cumsum):
    i = pl.program_id(0)
    j = pl.program_id(1)

    @pl.when(j == 0)
    def _():
        h = _rms(x_ref[...], g_ref[...]).astype(BF16)
        h_scr[...] = h
        logf = _log_sigmoid(_dot(h, wfl_ref[...]) + bf_ref[...])
        logf_ref[...] = logf
        if cumsum:
            @pl.when(i % tiles_per_seq == 0)
            def _():
                carry_scr[...] = jnp.zeros_like(carry_scr)
            c, carry = _cumsum_rows(logf, carry_scr[...])
            c_ref[...] = c
            carry_scr[...] = carry
        else:
            c_ref[...] = logf

    acc = _dot(h_scr[...], w_ref[...])

    @pl.when(j == 0)
    def _():
        q_ref[...] = (_rope(acc, cos_ref[...], sin_ref[...]) * SCALE).astype(q_ref.dtype)

    @pl.when(j == 1)
    def _():
        mkv_ref[...] = _rope(acc, cos_ref[...], sin_ref[...])

    @pl.when(j == 2)
    def _():
        mkv_ref[...] = acc

    @pl.when(jnp.logical_or(j == 3, j == 6))
    def _():
        q_ref[...] = (acc * SCALE).astype(q_ref.dtype)

    @pl.when(jnp.logical_or(j == 4, j == 5))
    def _():
        fkv_ref[...] = acc

    @pl.when(j >= 7)
    def _():
        gate_ref[...] = jax.nn.sigmoid(acc).astype(gate_ref.dtype)


def _in_proj(x, g, w_main, w_fl, b_f, cos, sin, *, tm, seq_len, q_dtype):
    m, d = x.shape
    n_tiles = w_main.shape[1] // HEAD_TILE
    n_gate = n_tiles - 7
    table_tiles = cos.shape[0] // tm
    cumsum = seq_len > 1
    tiles_per_seq = max(seq_len // tm, 1)
    kern = functools.partial(_in_proj_kernel, tiles_per_seq=tiles_per_seq, cumsum=cumsum)
    row = lambda i, j: (i, 0)
    return pl.pallas_call(
        kern,
        grid=(m // tm, n_tiles),
        in_specs=[
            pl.BlockSpec((tm, d), row),
            pl.BlockSpec((1, d), lambda i, j: (0, 0)),
            pl.BlockSpec((d, HEAD_TILE), lambda i, j: (0, j)),
            pl.BlockSpec((d, LANES), lambda i, j: (0, 0)),
            pl.BlockSpec((1, LANES), lambda i, j: (0, 0)),
            pl.BlockSpec((tm, LANES), lambda i, j: (i % table_tiles, 0)),
            pl.BlockSpec((tm, LANES), lambda i, j: (i % table_tiles, 0)),
        ],
        out_specs=[
            pl.BlockSpec((tm, HEAD_TILE), lambda i, j: (i, jnp.minimum(j // 3, 2))),
            pl.BlockSpec((tm, HEAD_TILE), lambda i, j: (i, jnp.clip(j - 1, 0, 1))),
            pl.BlockSpec((tm, HEAD_TILE), lambda i, j: (i, jnp.clip(j - 4, 0, 1))),
            pl.BlockSpec((tm, HEAD_TILE), lambda i, j: (i, jnp.clip(j - 7, 0, n_gate - 1))),
            pl.BlockSpec((tm, LANES), row),
            pl.BlockSpec((tm, LANES), row),
        ],
        out_shape=[
            jax.ShapeDtypeStruct((m, 3 * HEAD_TILE), q_dtype),
            jax.ShapeDtypeStruct((m, 2 * HEAD_TILE), F32),
            jax.ShapeDtypeStruct((m, 2 * HEAD_TILE), F32),
            jax.ShapeDtypeStruct((m, n_gate * HEAD_TILE), BF16),
            jax.ShapeDtypeStruct((m, LANES), F32),
            jax.ShapeDtypeStruct((m, LANES), F32),
        ],
        scratch_shapes=[pltpu.VMEM((tm, d), BF16), pltpu.VMEM((1, LANES), F32)],
        compiler_params=_params(("arbitrary", "arbitrary")),
        name="in_proj",
    )(x, g, w_main, w_fl, b_f, cos, sin)


def _norm_mm_kernel(x_ref, g_ref, w_ref, o_ref):
    h = _rms(x_ref[...], g_ref[...]).astype(BF16)
    o_ref[...] = _dot(h, w_ref[...])


def _norm_mm(x, g, w, *, tm):
    m, d = x.shape
    n = w.shape[1]
    return pl.pallas_call(
        _norm_mm_kernel,
        grid=(m // tm,),
        in_specs=[pl.BlockSpec((tm, d), lambda i: (i, 0)),
                  pl.BlockSpec((1, d), lambda i: (0, 0)),
                  pl.BlockSpec((d, n), lambda i: (0, 0))],
        out_specs=pl.BlockSpec((tm, n), lambda i: (i, 0)),
        out_shape=jax.ShapeDtypeStruct((m, n), F32),
        compiler_params=_params(("arbitrary",)),
        name="mem_kv_proj",
    )(x, g, w)


def _attn_kernel(*refs, mode, blk, n_blk):
    if mode == "fox":
        q_ref, k_ref, v_ref, cq_ref, ck_ref, o_ref, kb, vb, m_scr, l_scr, acc_scr = refs
    elif mode == "moba":
        q_ref, k_ref, v_ref, o_ref, kb, vb, km_scr, m_scr, l_scr, acc_scr = refs
    else:
        q_ref, k_ref, v_ref, o_ref, kb, vb = refs
    h = pl.program_id(1)
    qi = pl.program_id(2)
    tq = q_ref.shape[0]

    @pl.when(qi == 0)
    def _():
        kf = k_ref[...]
        kb[...] = kf.astype(BF16)
        vb[...] = v_ref[...].astype(BF16)
        if mode == "moba":
            km_scr[...] = jnp.zeros_like(km_scr)
            for n in range(n_blk):
                km_scr[n:n + 1, :] = jnp.sum(kf[n * blk:(n + 1) * blk], axis=0, keepdims=True) * (1.0 / blk)

    q = q_ref[...].astype(BF16)

    if mode == "mem":
        s = _dot_nt(q, kb[...])
        m0 = jnp.max(s, axis=1, keepdims=True)
        p = jnp.exp(s - m0)
        l0 = jnp.sum(p, axis=1, keepdims=True)
        o_ref[...] = (_dot(p.astype(BF16), vb[...]) * (1.0 / l0)).astype(o_ref.dtype)
        return

    if mode == "moba":
        nb8 = km_scr.shape[0]
        bs = sum(_dot_nt(part, q) for part in _split3(km_scr[...]))
        blk_id = lax.broadcasted_iota(jnp.int32, (nb8, tq), 0)
        cnt = jnp.zeros((nb8, tq), F32)
        for mth in range(n_blk - 1):
            sm = bs[mth:mth + 1, :]
            beats = jnp.logical_or(sm > bs, jnp.logical_and(sm == bs, mth < blk_id))
            cnt = cnt + jnp.where(beats, 1.0, 0.0) * (mth < qi).astype(F32)
        sel = jnp.where(jnp.logical_and(blk_id < qi, cnt < MOBA_TOPK), 1.0, 0.0)
        sel = jnp.concatenate([sel, jnp.zeros((LANES - nb8, tq), F32)], axis=0).astype(BF16)
        eye = (lax.broadcasted_iota(jnp.int32, (tq, tq), 0)
               == lax.broadcasted_iota(jnp.int32, (tq, tq), 1)).astype(BF16)
        selc = _dot_nt(eye, sel)
    else:
        lane = lax.broadcasted_iota(jnp.int32, (tq, LANES), 1)
        cq = jnp.sum(jnp.where(lane == h, cq_ref[...], 0.0), axis=1, keepdims=True)

    start = pl.multiple_of(qi * blk, blk)
    s = _dot_nt(q, kb[pl.ds(start, blk), :])
    if mode == "fox":
        s = s + cq - ck_ref[pl.ds(qi, 1), :]
    row = lax.broadcasted_iota(jnp.int32, (tq, blk), 0)
    col = lax.broadcasted_iota(jnp.int32, (tq, blk), 1)
    s = jnp.where(col <= row, s, NEG)
    m0 = jnp.max(s, axis=1, keepdims=True)
    p = jnp.exp(s - m0)
    m_scr[...] = m0
    l_scr[...] = jnp.sum(p, axis=1, keepdims=True)
    acc_scr[...] = _dot(p.astype(BF16), vb[pl.ds(start, blk), :])

    for n in range(n_blk - 1):
        @pl.when(n < qi)
        def _(n=n):
            s = _dot_nt(q, kb[n * blk:(n + 1) * blk, :])
            if mode == "fox":
                s = s + cq - ck_ref[n:n + 1, :]
            else:
                s = jnp.where(selc[:, n:n + 1] > 0.5, s, NEG)
            m_old = m_scr[...]
            m_new = jnp.maximum(m_old, jnp.max(s, axis=1, keepdims=True))
            alpha = jnp.exp(m_old - m_new)
            p = jnp.exp(s - m_new)
            l_scr[...] = alpha * l_scr[...] + jnp.sum(p, axis=1, keepdims=True)
            acc_scr[...] = alpha * acc_scr[...] + _dot(p.astype(BF16), vb[n * blk:(n + 1) * blk, :])
            m_scr[...] = m_new

    o_ref[...] = (acc_scr[...] * (1.0 / l_scr[...])).astype(o_ref.dtype)


def _attention(q, kv, *, mode, heads, q_col0, tq, cq=None, ck=None):
    b, s, _ = q.shape
    skv = kv.shape[1]
    blk = MOBA_BLOCK if mode != "mem" else skv
    n_blk = skv // blk
    if mode != "mem":
        assert tq == blk and s == skv and n_blk <= 16
    kern = functools.partial(_attn_kernel, mode=mode, blk=blk, n_blk=n_blk)
    in_specs = [
        pl.BlockSpec((None, tq, HEAD_DIM), lambda bi, h, qi: (bi, qi, q_col0 + h)),
        pl.BlockSpec((None, skv, HEAD_DIM), lambda bi, h, qi: (bi, 0, h)),
        pl.BlockSpec((None, skv, HEAD_DIM), lambda bi, h, qi: (bi, 0, heads + h)),
    ]
    args = [q, kv, kv]
    scratch = [pltpu.VMEM((skv, HEAD_DIM), BF16), pltpu.VMEM((skv, HEAD_DIM), BF16)]
    if mode == "fox":
        in_specs += [pl.BlockSpec((None, tq, LANES), lambda bi, h, qi: (bi, qi, 0)),
                     pl.BlockSpec((None, None, n_blk, blk), lambda bi, h, qi: (bi, h, 0, 0))]
        args += [cq, ck]
    if mode == "moba":
        nb8 = -(-n_blk // SUBLANES) * SUBLANES
        scratch += [pltpu.VMEM((nb8, HEAD_DIM), F32)]
    if mode != "mem":
        scratch += [pltpu.VMEM((tq, 1), F32), pltpu.VMEM((tq, 1), F32), pltpu.VMEM((tq, HEAD_DIM), F32)]
    return pl.pallas_call(
        kern,
        grid=(b, heads, s // tq),
        in_specs=in_specs,
        out_specs=pl.BlockSpec((None, tq, HEAD_DIM), lambda bi, h, qi: (bi, qi, h)),
        out_shape=jax.ShapeDtypeStruct((b, s, heads * HEAD_DIM), BF16),
        scratch_shapes=scratch,
        compiler_params=_params(("arbitrary", "arbitrary", "arbitrary")),
        name="attn_" + mode,
    )(*args)


def _merge_kernel(om_ref, of_ref, oc_ref, gate_ref, x_ref, wm_ref, wf_ref, wc_ref, wo_ref, o_ref):
    d = x_ref.shape[1]
    merged = gate_ref[:, 0:d].astype(F32) * _dot(om_ref[...], wm_ref[...])
    merged = merged + gate_ref[:, d:2 * d].astype(F32) * _dot(of_ref[...], wf_ref[...])
    merged = merged + gate_ref[:, 2 * d:3 * d].astype(F32) * _dot(oc_ref[...], wc_ref[...])
    o_ref[...] = x_ref[...] + _dot(merged.astype(BF16), wo_ref[...])


def _resident(shape):
    return pl.BlockSpec(shape, lambda i: (0,) * len(shape), pipeline_mode=pl.Buffered(1))


def _merge_out(om, of, oc, gates, x, wm, wf, wc, wo, *, tm):
    m, d = x.shape
    rows = lambda w: pl.BlockSpec((tm, w), lambda i: (i, 0))
    return pl.pallas_call(
        _merge_kernel,
        grid=(m // tm,),
        in_specs=[rows(om.shape[1]), rows(of.shape[1]), rows(oc.shape[1]), rows(gates.shape[1]), rows(d),
                  _resident(wm.shape), _resident(wf.shape), _resident(wc.shape), _resident(wo.shape)],
        out_specs=rows(d),
        out_shape=jax.ShapeDtypeStruct((m, d), F32),
        compiler_params=_params(("arbitrary",)),
        name="merge_out",
    )(om, of, oc, gates, x, wm, wf, wc, wo)


def _ffn_up_kernel(*refs, tiles_per_seq, per_row_state):
    if per_row_state:
        x_ref, g_ref, wa_ref, wb_ref, wc_ref, bc_ref, p0_ref, p1_ref, g_out, a_out, h_scr = refs
    else:
        x_ref, g_ref, wa_ref, wb_ref, wc_ref, bc_ref, g_out, tail_out, h_scr, carry_scr = refs
    i = pl.program_id(0)
    j = pl.program_id(1)

    @pl.when(j == 0)
    def _():
        h_scr[...] = _rms(x_ref[...], g_ref[...]).astype(BF16)

    h = h_scr[...]
    a = _dot(h, wa_ref[...])
    b = _dot(h, wb_ref[...])
    w = wc_ref[...]
    tm = a.shape[0]
    if per_row_state:
        a1 = p1_ref[...]
        a2 = p0_ref[...]
        a_out[...] = a
    else:
        @pl.when(i % tiles_per_seq == 0)
        def _():
            carry_scr[j] = jnp.zeros(carry_scr.shape[1:], F32)
        prev = carry_scr[j]
        row = lax.broadcasted_iota(jnp.int32, a.shape, 0)
        a1 = jnp.where(row == 0, prev[7:8, :], pltpu.roll(a, 1, 0))
        a2 = jnp.where(row == 0, prev[6:7, :], jnp.where(row == 1, prev[7:8, :], pltpu.roll(a, 2, 0)))
        tail = a[tm - SUBLANES:, :]
        carry_scr[j] = tail
        tail_out[...] = tail
    a_conv = bc_ref[...] + a2 * w[0:1, :] + a1 * w[1:2, :] + a * w[2:3, :]
    g_out[...] = (a_conv * jax.nn.sigmoid(a_conv) * b).astype(g_out.dtype)


def _ffn_up(x, g, wa, wb, wconv, bconv, *, tm, tn, seq_len, prev=None):
    m, d = x.shape
    f = wa.shape[1]
    nj = f // tn
    per_row_state = prev is not None
    tiles_per_seq = max(seq_len // tm, 1)
    kern = functools.partial(_ffn_up_kernel, tiles_per_seq=tiles_per_seq, per_row_state=per_row_state)
    col = lambda r: pl.BlockSpec((r, tn), lambda i, j: (0, j))
    in_specs = [pl.BlockSpec((tm, d), lambda i, j: (i, 0)), pl.BlockSpec((1, d), lambda i, j: (0, 0)),
                col(d), col(d), col(SUBLANES), col(1)]
    args = [x, g, wa, wb, wconv, bconv]
    tile = pl.BlockSpec((tm, tn), lambda i, j: (i, j))
    scratch = [pltpu.VMEM((tm, d), BF16)]
    if per_row_state:
        in_specs += [tile, tile]
        args += [prev[0], prev[1]]
        out_specs = [tile, tile]
        out_shape = [jax.ShapeDtypeStruct((m, f), BF16), jax.ShapeDtypeStruct((m, f), F32)]
    else:
        out_specs = [tile, pl.BlockSpec((None, SUBLANES, tn), lambda i, j: (i, 0, j))]
        out_shape = [jax.ShapeDtypeStruct((m, f), BF16), jax.ShapeDtypeStruct((m // tm, SUBLANES, f), F32)]
        scratch += [pltpu.VMEM((nj, SUBLANES, tn), F32)]
    return pl.pallas_call(
        kern, grid=(m // tm, nj), in_specs=in_specs, out_specs=out_specs, out_shape=out_shape,
        scratch_shapes=scratch, compiler_params=_params(("arbitrary", "arbitrary")), name="ffn_up",
    )(*args)


def _ffn_down_kernel(g_ref, w_ref, x_ref, gf_ref, o_ref, y_ref):
    xo = x_ref[...] + _dot(g_ref[...], w_ref[...])
    o_ref[...] = xo
    y_ref[...] = _rms(xo, gf_ref[...])


def _ffn_down(gact, w, x, g_final, *, tm):
    m, d = x.shape
    f = gact.shape[1]
    rows = lambda wd: pl.BlockSpec((tm, wd), lambda i: (i, 0))
    return pl.pallas_call(
        _ffn_down_kernel,
        grid=(m // tm,),
        in_specs=[rows(f), _resident(w.shape), rows(d), pl.BlockSpec((1, d), lambda i: (0, 0))],
        out_specs=[rows(d), rows(d)],
        out_shape=[jax.ShapeDtypeStruct((m, d), F32), jax.ShapeDtypeStruct((m, d), F32)],
        compiler_params=_params(("arbitrary",)),
        name="ffn_down",
    )(gact, w, x, g_final)


def _head_mask(rows, width):
    r = lax.broadcasted_iota(jnp.int32, (rows, width), 0)
    c = lax.broadcasted_iota(jnp.int32, (rows, width), 1)
    return (c // HEAD_DIM) == r


def _fox_dec_kernel(pt_ref, q_ref, kvn_ref, lfn_ref, *refs, pages_per_step, heads):
    del pt_ref
    kv_refs = refs[:pages_per_step]
    lf_refs = refs[pages_per_step:2 * pages_per_step]
    o_ref, m_scr, l_scr, acc_scr, run_scr = refs[2 * pages_per_step:]
    c = pl.program_id(1)
    w = heads * HEAD_DIM
    hm = _head_mask(SUBLANES, w)
    qbd = jnp.where(hm, jnp.broadcast_to(q_ref[...], (SUBLANES, w)), 0.0)

    @pl.when(c == 0)
    def _():
        kvn = kvn_ref[...]
        m_scr[...] = jnp.sum(qbd * kvn[:, :w], axis=1, keepdims=True)
        l_scr[...] = jnp.ones_like(l_scr)
        acc_scr[...] = jnp.broadcast_to(kvn[:, w:], (SUBLANES, w))
        run_scr[...] = lfn_ref[...]

    r = lax.broadcasted_iota(jnp.int32, (LANES, LANES), 0)
    cc = lax.broadcasted_iota(jnp.int32, (LANES, LANES), 1)
    upper = (r > cc).astype(BF16)
    qb = qbd.astype(BF16)
    m, l, acc, run = m_scr[...], l_scr[...], acc_scr[...], run_scr[...]
    for u in range(pages_per_step):
        kv = kv_refs[u][...]
        lf = lf_refs[u][...]
        s = _dot_nt(qb, kv[:, :w].astype(BF16))
        suffix = sum(_dot(part, upper) for part in _split3(lf))
        s = s + run + suffix
        m_new = jnp.maximum(m, jnp.max(s, axis=1, keepdims=True))
        alpha = jnp.exp(m - m_new)
        p = jnp.exp(s - m_new)
        l = alpha * l + jnp.sum(p, axis=1, keepdims=True)
        acc = alpha * acc + _dot(p.astype(BF16), kv[:, w:].astype(BF16))
        m = m_new
        run = run + jnp.sum(lf, axis=1, keepdims=True)
    m_scr[...], l_scr[...], acc_scr[...], run_scr[...] = m, l, acc, run

    @pl.when(c == pl.num_programs(1) - 1)
    def _():
        o = jnp.where(hm, acc * (1.0 / l), 0.0)
        o_ref[...] = jnp.sum(o, axis=0, keepdims=True).astype(o_ref.dtype)


def _fox_decode(q, kv_new, lf_new, cache_kv, cache_lf_t, page_table, page0, *, pages_per_step):
    db, n_pages = page_table.shape
    heads = FOX_HEADS
    w = heads * HEAD_DIM
    steps = n_pages // pages_per_step

    def page_map(u):
        def f(b, c, pt):
            return (page0 + pt[b * n_pages + (n_pages - 1 - (c * pages_per_step + u))], 0, 0)
        return f

    in_specs = [pl.BlockSpec((None, 1, w), lambda b, c, pt: (b, 0, 0)),
                pl.BlockSpec((None, 1, 2 * w), lambda b, c, pt: (b, 0, 0)),
                pl.BlockSpec((None, SUBLANES, 1), lambda b, c, pt: (b, 0, 0))]
    in_specs += [pl.BlockSpec((None, LANES, 2 * w), page_map(u)) for u in range(pages_per_step)]
    in_specs += [pl.BlockSpec((None, SUBLANES, LANES), page_map(u)) for u in range(pages_per_step)]
    kern = functools.partial(_fox_dec_kernel, pages_per_step=pages_per_step, heads=heads)
    return pl.pallas_call(
        kern,
        grid_spec=pltpu.PrefetchScalarGridSpec(
            num_scalar_prefetch=1, grid=(db, steps), in_specs=in_specs,
            out_specs=pl.BlockSpec((None, 1, w), lambda b, c, pt: (b, 0, 0)),
            scratch_shapes=[pltpu.VMEM((SUBLANES, 1), F32), pltpu.VMEM((SUBLANES, 1), F32),
                            pltpu.VMEM((SUBLANES, w), F32), pltpu.VMEM((SUBLANES, 1), F32)]),
        out_shape=jax.ShapeDtypeStruct((db, 1, w), BF16),
        compiler_params=_params(("arbitrary", "arbitrary")),
        name="fox_decode",
    )(page_table.reshape(-1), q, kv_new, lf_new,
      *([cache_kv] * pages_per_step), *([cache_lf_t] * pages_per_step))


def _moba_sel_kernel(pt_ref, q_ref, *refs, pages_per_step, heads, n_blk):
    del pt_ref
    k_refs = refs[:pages_per_step]
    sel_ref, km_scr = refs[pages_per_step:]
    c = pl.program_id(1)
    w = heads * HEAD_DIM
    pages_per_blk = MOBA_BLOCK // LANES
    blks = pages_per_step // pages_per_blk

    @pl.when(c == 0)
    def _():
        km_scr[...] = jnp.zeros_like(km_scr)

    rows = []
    for bb in range(blks):
        tot = jnp.zeros((1, w), F32)
        for pp in range(pages_per_blk):
            tot = tot + jnp.sum(k_refs[bb * pages_per_blk + pp][...], axis=0, keepdims=True)
        rows.append(tot * (1.0 / MOBA_BLOCK))
    km_scr[pl.ds(pl.multiple_of(c * blks, blks), blks), :] = jnp.concatenate(rows, axis=0)

    @pl.when(c == pl.num_programs(1) - 1)
    def _():
        hm = _head_mask(SUBLANES, w)
        qbd = jnp.where(hm, jnp.broadcast_to(q_ref[...], (SUBLANES, w)), 0.0)
        km = km_scr[...]
        bs = jnp.zeros((SUBLANES, LANES), F32)
        for qp in _split3(qbd):
            for kp in _split3(km):
                bs = bs + _dot_nt(qp, kp)
        lane = lax.broadcasted_iota(jnp.int32, (SUBLANES, LANES), 1)
        bs = jnp.where(lane < n_blk, bs, -jnp.inf)
        out = jnp.zeros((SUBLANES, LANES), jnp.int32)
        for t in range(MOBA_TOPK):
            mx = jnp.max(bs, axis=1, keepdims=True)
            idx = jnp.min(jnp.where(bs == mx, lane, LANES), axis=1, keepdims=True)
            out = jnp.where(lane == t, idx, out)
            bs = jnp.where(lane == idx, -jnp.inf, bs)
        sel_ref[...] = out


def _moba_select(q, cache_kv, page_table, layer_page0, *, pages_per_step):
    db, n_pages = page_table.shape
    heads = MOBA_HEADS
    w = heads * HEAD_DIM
    n_blk = n_pages * LANES // MOBA_BLOCK
    assert MOBA_TOPK <= n_blk <= LANES and pages_per_step % (SUBLANES * MOBA_BLOCK // LANES) == 0
    steps = n_pages // pages_per_step

    def page_map(u):
        return lambda b, c, pt: (layer_page0 + pt[b * n_pages + c * pages_per_step + u], 0, 0)

    in_specs = [pl.BlockSpec((None, 1, w), lambda b, c, pt: (b, 0, 0))]
    in_specs += [pl.BlockSpec((None, LANES, w), page_map(u)) for u in range(pages_per_step)]
    kern = functools.partial(_moba_sel_kernel, pages_per_step=pages_per_step, heads=heads, n_blk=n_blk)
    return pl.pallas_call(
        kern,
        grid_spec=pltpu.PrefetchScalarGridSpec(
            num_scalar_prefetch=1, grid=(db, steps), in_specs=in_specs,
            out_specs=pl.BlockSpec((None, SUBLANES, LANES), lambda b, c, pt: (b, 0, 0)),
            scratch_shapes=[pltpu.VMEM((LANES, w), F32)]),
        out_shape=jax.ShapeDtypeStruct((db, SUBLANES, LANES), jnp.int32),
        compiler_params=_params(("arbitrary", "arbitrary")),
        name="moba_select",
    )(page_table.reshape(-1), q, *([cache_kv] * pages_per_step))


def _moba_dec_kernel(sel_ref, pt_ref, q_ref, kn_ref, vn_ref, *refs, n_pages_sel):
    del sel_ref, pt_ref
    k_refs = refs[:n_pages_sel]
    v_refs = refs[n_pages_sel:2 * n_pages_sel]
    o_ref = refs[2 * n_pages_sel]
    q = q_ref[...]
    qb = jnp.broadcast_to(q, (SUBLANES, HEAD_DIM)).astype(BF16)
    s_self = jnp.sum(q * kn_ref[...], axis=1, keepdims=True)
    ss = [_dot_nt(qb, k_refs[u][...].astype(BF16)) for u in range(n_pages_sel)]
    m = s_self
    for s in ss:
        m = jnp.maximum(m, jnp.max(s, axis=1, keepdims=True))
    l = jnp.exp(s_self - m)
    acc = l * vn_ref[...]
    for u, s in enumerate(ss):
        p = jnp.exp(s - m)
        l = l + jnp.sum(p, axis=1, keepdims=True)
        acc = acc + _dot(p.astype(BF16), v_refs[u][...].astype(BF16))
    o_ref[...] = (acc * (1.0 / l))[0:1, :].astype(o_ref.dtype)


def _moba_decode(q, kv_new, sel, cache_kv, page_table, layer_page0):
    db, n_pages = page_table.shape
    heads = MOBA_HEADS
    pages_per_blk = MOBA_BLOCK // LANES
    n_pages_sel = MOBA_TOPK * pages_per_blk

    def page_map(u, col0):
        jsel, pg = divmod(u, pages_per_blk)

        def f(b, h, sel_s, pt):
            blk = sel_s[(b * heads + h) * MOBA_TOPK + jsel]
            return (layer_page0 + pt[b * n_pages + blk * pages_per_blk + pg], 0, col0 + h)
        return f

    one = lambda col0: pl.BlockSpec((None, 1, HEAD_DIM), lambda b, h, sel_s, pt: (b, 0, col0 + h))
    in_specs = [one(0), one(0), one(heads)]
    in_specs += [pl.BlockSpec((None, LANES, HEAD_DIM), page_map(u, 0)) for u in range(n_pages_sel)]
    in_specs += [pl.BlockSpec((None, LANES, HEAD_DIM), page_map(u, heads)) for u in range(n_pages_sel)]
    kern = functools.partial(_moba_dec_kernel, n_pages_sel=n_pages_sel)
    return pl.pallas_call(
        kern,
        grid_spec=pltpu.PrefetchScalarGridSpec(
            num_scalar_prefetch=2, grid=(db, heads), in_specs=in_specs,
            out_specs=pl.BlockSpec((None, 1, HEAD_DIM), lambda b, h, sel_s, pt: (b, 0, h))),
        out_shape=jax.ShapeDtypeStruct((db, 1, heads * HEAD_DIM), BF16),
        compiler_params=_params(("arbitrary", "arbitrary")),
        name="moba_decode",
    )(sel.reshape(-1), page_table.reshape(-1), q, kv_new, kv_new, *([cache_kv] * (2 * n_pages_sel)))


def _rope_tables(pos):
    half = HEAD_DIM // 2
    inv_freq = ROPE_THETA ** (-jnp.arange(half, dtype=F32) / half)
    ang = pos.astype(F32)[:, None] * inv_freq[None, :]
    cos, sin = jnp.cos(ang), jnp.sin(ang)
    return jnp.concatenate([cos, cos], axis=-1), jnp.concatenate([-sin, sin], axis=-1)


def _pad_cols(a, n):
    return jnp.pad(a, ((0, 0), (0, n - a.shape[1])))


def _prep_layer(l, g_attn, w_in, b_f, w_br_moba, w_br_fox, w_br_mem, w_out, g_mem, w_mem_kv,
                g_ffn, w_up, w_conv, b_conv, w_down, ff_tile):
    d = w_in.shape[1]
    mw, fw, cw = MOBA_HEADS * HEAD_DIM, FOX_HEADS * HEAD_DIM, MEM_HEADS * HEAD_DIM
    o_fl = 3 * mw + 3 * fw
    o_qc = o_fl + FOX_HEADS
    o_gl = o_qc + cw
    wi = w_in[l]
    w_main = jnp.concatenate(
        [wi[:, :o_fl], _pad_cols(wi[:, o_qc:o_gl], HEAD_TILE), wi[:, o_gl:]], axis=1).astype(BF16)
    w_fl = _pad_cols(wi[:, o_fl:o_qc], LANES).astype(BF16)
    f = w_down.shape[1]
    fp = -(-f // ff_tile) * ff_tile
    wu = w_up[l]
    return dict(
        g_attn=g_attn[l][None], w_main=w_main, w_fl=w_fl, b_f=_pad_cols(b_f[l][None], LANES),
        wm=w_br_moba[l].astype(BF16), wf=w_br_fox[l].astype(BF16), wc=w_br_mem[l].astype(BF16),
        wo=w_out[l].astype(BF16), g_mem=g_mem[l][None], w_mem=w_mem_kv[l].astype(BF16),
        g_ffn=g_ffn[l][None], wa=_pad_cols(wu[:, :f], fp).astype(BF16), wb=_pad_cols(wu[:, f:], fp).astype(BF16),
        w_conv=jnp.pad(w_conv[l], ((0, SUBLANES - CONV_W), (0, fp - f))), b_conv=_pad_cols(b_conv[l][None], fp),
        w_down=jnp.pad(w_down[l], ((0, fp - f), (0, 0))).astype(BF16), f=f)


def kernel(x_prompt, x_sample, cache_moba_kv, cache_fox_kv, cache_fox_logf, cache_mem_kv, state_ffn_conv,
           page_table, mem_prompt, g_attn, w_in, b_f, w_br_moba, w_br_fox, w_br_mem, w_out, g_mem,
           w_mem_kv, g_ffn, w_up, w_conv, b_conv, w_down, g_final):
    bsz, seq, d = x_prompt.shape
    db = x_sample.shape[0]
    depth, n_phys, page = cache_moba_kv.shape[:3]
    n_pages = page_table.shape[1]
    past_len = n_pages * page
    n_mem = mem_prompt.shape[1]
    mw, fw, cw = MOBA_HEADS * HEAD_DIM, FOX_HEADS * HEAD_DIM, MEM_HEADS * HEAD_DIM
    assert page == LANES and db == SUBLANES and x_sample.shape[1] == 1
    ff_tile = 512
    tm = min(512, seq)
    tq = MOBA_BLOCK

    cos_p, sin_p = _rope_tables(jnp.arange(seq))
    cos_s, sin_s = _rope_tables(jnp.full((db,), past_len))
    gf = g_final[None]

    moba_pages = cache_moba_kv.reshape(depth * n_phys, page, 2 * mw)
    fox_pages = cache_fox_kv.reshape(depth * n_phys, page, 2 * fw)
    lf_pages = jnp.pad(jnp.swapaxes(cache_fox_logf, 2, 3),
                       ((0, 0), (0, 0), (0, SUBLANES - FOX_HEADS), (0, 0))).reshape(depth * n_phys, SUBLANES, page)

    xp = x_prompt.reshape(bsz * seq, d)
    xs = x_sample.reshape(db, d)
    outs = {k: [] for k in ("p_moba", "p_fox", "p_logf", "p_mem", "p_conv", "s_moba", "s_fox", "s_logf", "s_conv")}
    yp = ys = None
    for l in range(depth):
        w = _prep_layer(l, g_attn, w_in, b_f, w_br_moba, w_br_fox, w_br_mem, w_out, g_mem, w_mem_kv,
                        g_ffn, w_up, w_conv, b_conv, w_down, ff_tile)
        f = w["f"]

        mkv = _norm_mm(mem_prompt.reshape(bsz * n_mem, d), w["g_mem"], w["w_mem"], tm=min(512, bsz * n_mem))
        q_all, moba_kv, fox_kv, gates, logf, c = _in_proj(
            xp, w["g_attn"], w["w_main"], w["w_fl"], w["b_f"], cos_p, sin_p, tm=tm, seq_len=seq, q_dtype=BF16)
        q3 = q_all.reshape(bsz, seq, 3 * HEAD_TILE)
        c3 = c.reshape(bsz, seq, LANES)
        ck = jnp.swapaxes(c3[:, :, :FOX_HEADS], 1, 2).reshape(bsz, FOX_HEADS, seq // tq, tq)
        o_m = _attention(q3, moba_kv.reshape(bsz, seq, 2 * mw), mode="moba", heads=MOBA_HEADS, q_col0=0, tq=tq)
        o_f = _attention(q3, fox_kv.reshape(bsz, seq, 2 * fw), mode="fox", heads=FOX_HEADS,
                         q_col0=MOBA_HEADS, tq=tq, cq=c3, ck=ck)
        o_c = _attention(q3, mkv.reshape(bsz, n_mem, 2 * cw), mode="mem", heads=MEM_HEADS,
                         q_col0=MOBA_HEADS + FOX_HEADS, tq=tq)
        xp = _merge_out(o_m.reshape(bsz * seq, mw), o_f.reshape(bsz * seq, fw), o_c.reshape(bsz * seq, cw),
                        gates, xp, w["wm"], w["wf"], w["wc"], w["wo"], tm=256)
        gact, tails = _ffn_up(xp, w["g_ffn"], w["wa"], w["wb"], w["w_conv"], w["b_conv"],
                              tm=tm, tn=ff_tile, seq_len=seq)
        xp, yp = _ffn_down(gact, w["w_down"], xp, gf, tm=256)
        outs["p_moba"].append(moba_kv.reshape(bsz, seq, 2, MOBA_HEADS, HEAD_DIM))
        outs["p_fox"].append(fox_kv.reshape(bsz, seq, 2, FOX_HEADS, HEAD_DIM))
        outs["p_logf"].append(logf[:, :FOX_HEADS].reshape(bsz, seq, FOX_HEADS))
        outs["p_mem"].append(mkv.reshape(bsz, n_mem, 2, MEM_HEADS, HEAD_DIM))
        tiles_per_seq = seq // tm
        outs["p_conv"].append(tails[tiles_per_seq - 1::tiles_per_seq, SUBLANES - (CONV_W - 1):, :f])

        q_s, moba_s, fox_s, gates_s, logf_s, _ = _in_proj(
            xs, w["g_attn"], w["w_main"], w["w_fl"], w["b_f"], cos_s, sin_s, tm=db, seq_len=1, q_dtype=F32)
        q_s3 = q_s.reshape(db, 1, 3 * HEAD_TILE)
        sel = _moba_select(q_s3, moba_pages, page_table, l * n_phys, pages_per_step=16)
        o_ms = _moba_decode(q_s3, moba_s.reshape(db, 1, 2 * mw), sel[:, :MOBA_HEADS, :MOBA_TOPK],
                            moba_pages, page_table, l * n_phys)
        o_fs = _fox_decode(q_s3[:, :, mw:mw + fw], fox_s.reshape(db, 1, 2 * fw),
                           logf_s[:, :SUBLANES].reshape(db, SUBLANES, 1), fox_pages, lf_pages, page_table,
                           l * n_phys, pages_per_step=8)
        q_pad = jnp.pad(q_s3.astype(BF16), ((0, 0), (0, SUBLANES - 1), (0, 0)))
        o_cs = _attention(q_pad, cache_mem_kv[l].reshape(db, n_mem, 2 * cw), mode="mem", heads=MEM_HEADS,
                          q_col0=MOBA_HEADS + FOX_HEADS, tq=SUBLANES)[:, 0]
        xs = _merge_out(o_ms.reshape(db, mw), o_fs.reshape(db, fw), o_cs, gates_s, xs,
                        w["wm"], w["wf"], w["wc"], w["wo"], tm=db)
        fp = w["wa"].shape[1]
        prev = jnp.pad(state_ffn_conv[l], ((0, 0), (0, 0), (0, fp - f)))
        gact_s, a_s = _ffn_up(xs, w["g_ffn"], w["wa"], w["wb"], w["w_conv"], w["b_conv"],
                              tm=db, tn=ff_tile, seq_len=1, prev=(prev[:, 0], prev[:, 1]))
        xs, ys = _ffn_down(gact_s, w["w_down"], xs, gf, tm=db)
        outs["s_moba"].append(moba_s.reshape(db, 1, 2, MOBA_HEADS, HEAD_DIM))
        outs["s_fox"].append(fox_s.reshape(db, 1, 2, FOX_HEADS, HEAD_DIM))
        outs["s_logf"].append(logf_s[:, :FOX_HEADS].reshape(db, 1, FOX_HEADS))
        outs["s_conv"].append(jnp.stack([state_ffn_conv[l][:, 1], a_s[:, :f]], axis=1))

    st = lambda k: jnp.stack(outs[k])
    return (yp.reshape(bsz, seq, d), ys.reshape(db, 1, d),
            st("p_moba"), st("p_fox"), st("p_logf"), st("p_mem"), st("p_conv"),
            st("s_moba"), st("s_fox"), st("s_logf"), st("s_conv"))
```

```python
import functools

import jax
import jax.numpy as jnp
from jax import lax
from jax.experimental import pallas as pl
from jax.experimental.pallas import tpu as pltpu

F32 = jnp.float32
BF16 = jnp.bfloat16

HEAD_DIM = 128
MOBA_HEADS = 6
FOX_HEADS = 6
MEM_HEADS = 4
MOBA_BLOCK = 256
MOBA_TOPK = 3
CONV_W = 3
ROPE_THETA = 10000.0
NORM_EPS = 1e-6
NEG = -1e30

LANES = 128
SUBLANES = 8
HEAD_TILE = MOBA_HEADS * HEAD_DIM
VMEM_LIMIT = 56 * 1024 * 1024
ROW_CHUNK = 256
SCALE = HEAD_DIM ** -0.5

_NT = (((1,), (1,)), ((), ()))


def _params(sem):
    return pltpu.CompilerParams(dimension_semantics=sem, vmem_limit_bytes=VMEM_LIMIT)


def _rms(x, g):
    return x * lax.rsqrt(jnp.mean(x * x, axis=-1, keepdims=True) + NORM_EPS) * g


def _log_sigmoid(x):
    return jnp.minimum(x, 0.0) - jnp.log1p(jnp.exp(-jnp.abs(x)))


def _split3(v):
    hi = v.astype(BF16)
    r1 = v - hi.astype(F32)
    mid = r1.astype(BF16)
    lo = (r1 - mid.astype(F32)).astype(BF16)
    return hi, mid, lo


def _dot(a, b):
    return jnp.dot(a, b, preferred_element_type=F32)


def _dot_nt(a, b):
    return lax.dot_general(a, b, _NT, preferred_element_type=F32)


def _cumsum_rows(v, carry, blk=256):
    tm = v.shape[0]
    blk = min(blk, tm)
    r = lax.broadcasted_iota(jnp.int32, (blk, blk), 0)
    c = lax.broadcasted_iota(jnp.int32, (blk, blk), 1)
    tri = (c <= r).astype(BF16)
    outs = []
    for s in range(0, tm, blk):
        hi, mid, lo = _split3(v[s:s + blk])
        cs = _dot(tri, hi) + _dot(tri, mid) + _dot(tri, lo) + carry
        outs.append(cs)
        carry = cs[blk - 1:blk, :]
    return jnp.concatenate(outs, axis=0), carry


def _rope(acc, cos, sin_signed):
    outs = []
    for hh in range(acc.shape[1] // HEAD_DIM):
        xh = acc[:, hh * HEAD_DIM:(hh + 1) * HEAD_DIM]
        outs.append(xh * cos + pltpu.roll(xh, HEAD_DIM // 2, 1) * sin_signed)
    return jnp.concatenate(outs, axis=1)


def _head_cols(h):
    return slice(h * HEAD_DIM, (h + 1) * HEAD_DIM)


def _in_proj_kernel(x_ref, g_ref, w_ref, wfl_ref, bf_ref, cos_ref, sin_ref,
                    q_ref, mkv_ref, fkv_ref, gate_ref, logf_ref, c_ref,
                    h_scr, carry_scr, *, tiles_per_seq, cumsum):
    i = pl.program_id(0)
    j = pl.program_id(1)

    @pl.when(j == 0)
    def _():
        h = _rms(x_ref[...], g_ref[...]).astype(BF16)
        h_scr[...] = h
        logf = _log_sigmoid(_dot(h, wfl_ref[...]) + bf_ref[...])
        logf_ref[...] = logf
        if cumsum:
            @pl.when(i % tiles_per_seq == 0)
            def _():
                carry_scr[...] = jnp.zeros_like(carry_scr)
            c, carry = _cumsum_rows(logf, carry_scr[...])
            c_ref[...] = c
            carry_scr[...] = carry
        else:
            c_ref[...] = logf

    tm = x_ref.shape[0]
    rc = min(ROW_CHUNK, tm)

    def tile(out_ref, epilogue):
        for r0 in range(0, tm, rc):
            rows = slice(r0, r0 + rc)
            acc = _dot(h_scr[rows, :], w_ref[...])
            out_ref[rows, :] = epilogue(acc, rows).astype(out_ref.dtype)

    rope = lambda acc, rows: _rope(acc, cos_ref[rows, :], sin_ref[rows, :])

    @pl.when(j == 0)
    def _():
        tile(q_ref, lambda acc, rows: rope(acc, rows) * SCALE)

    @pl.when(j == 1)
    def _():
        tile(mkv_ref, rope)

    @pl.when(j == 2)
    def _():
        tile(mkv_ref, lambda acc, rows: acc)

    @pl.when(jnp.logical_or(j == 3, j == 6))
    def _():
        tile(q_ref, lambda acc, rows: acc * SCALE)

    @pl.when(jnp.logical_or(j == 4, j == 5))
    def _():
        tile(fkv_ref, lambda acc, rows: acc)

    @pl.when(j >= 7)
    def _():
        tile(gate_ref, lambda acc, rows: jax.nn.sigmoid(acc))


def _in_proj(x, g, w_main, w_fl, b_f, cos, sin, *, tm, seq_len, q_dtype):
    m, d = x.shape
    n_tiles = w_main.shape[1] // HEAD_TILE
    n_gate = n_tiles - 7
    table_tiles = cos.shape[0] // tm
    cumsum = seq_len > 1
    tiles_per_seq = max(seq_len // tm, 1)
    kern = functools.partial(_in_proj_kernel, tiles_per_seq=tiles_per_seq, cumsum=cumsum)
    row = lambda i, j: (i, 0)
    return pl.pallas_call(
        kern,
        grid=(m // tm, n_tiles),
        in_specs=[
            pl.BlockSpec((tm, d), row),
            pl.BlockSpec((1, d), lambda i, j: (0, 0)),
            pl.BlockSpec((d, HEAD_TILE), lambda i, j: (0, j)),
            pl.BlockSpec((d, LANES), lambda i, j: (0, 0)),
            pl.BlockSpec((1, LANES), lambda i, j: (0, 0)),
            pl.BlockSpec((tm, LANES), lambda i, j: (i % table_tiles, 0)),
            pl.BlockSpec((tm, LANES), lambda i, j: (i % table_tiles, 0)),
        ],
        out_specs=[
            pl.BlockSpec((tm, HEAD_TILE), lambda i, j: (i, jnp.minimum(j // 3, 2))),
            pl.BlockSpec((tm, HEAD_TILE), lambda i, j: (i, jnp.clip(j - 1, 0, 1))),
            pl.BlockSpec((tm, HEAD_TILE), lambda i, j: (i, jnp.clip(j - 4, 0, 1))),
            pl.BlockSpec((tm, HEAD_TILE), lambda i, j: (i, jnp.clip(j - 7, 0, n_gate - 1))),
            pl.BlockSpec((tm, LANES), row),
            pl.BlockSpec((tm, LANES), row),
        ],
        out_shape=[
            jax.ShapeDtypeStruct((m, 3 * HEAD_TILE), q_dtype),
            jax.ShapeDtypeStruct((m, 2 * HEAD_TILE), F32),
            jax.ShapeDtypeStruct((m, 2 * HEAD_TILE), F32),
            jax.ShapeDtypeStruct((m, n_gate * HEAD_TILE), BF16),
            jax.ShapeDtypeStruct((m, LANES), F32),
            jax.ShapeDtypeStruct((m, LANES), F32),
        ],
        scratch_shapes=[pltpu.VMEM((tm, d), BF16), pltpu.VMEM((1, LANES), F32)],
        compiler_params=_params(("arbitrary", "arbitrary")),
        name="in_proj",
    )(x, g, w_main, w_fl, b_f, cos, sin)


def _norm_mm_kernel(x_ref, g_ref, w_ref, o_ref):
    h = _rms(x_ref[...], g_ref[...]).astype(BF16)
    o_ref[...] = _dot(h, w_ref[...])


def _norm_mm(x, g, w, *, tm):
    m, d = x.shape
    n = w.shape[1]
    return pl.pallas_call(
        _norm_mm_kernel,
        grid=(m // tm,),
        in_specs=[pl.BlockSpec((tm, d), lambda i: (i, 0)),
                  pl.BlockSpec((1, d), lambda i: (0, 0)),
                  pl.BlockSpec((d, n), lambda i: (0, 0))],
        out_specs=pl.BlockSpec((tm, n), lambda i: (i, 0)),
        out_shape=jax.ShapeDtypeStruct((m, n), F32),
        compiler_params=_params(("arbitrary",)),
        name="mem_kv_proj",
    )(x, g, w)


def _attn_kernel(*refs, mode, heads, blk, n_blk):
    if mode == "fox":
        q_ref, kv_ref, cq_ref, ck_ref, o_ref, kb, vb, m_scr, l_scr, acc_scr = refs
    elif mode == "moba":
        q_ref, kv_ref, o_ref, kb, vb, km_scr, m_scr, l_scr, acc_scr = refs
    else:
        q_ref, kv_ref, o_ref, kb, vb = refs
    qi = pl.program_id(1)
    tq = q_ref.shape[0]
    w = heads * HEAD_DIM

    @pl.when(qi == 0)
    def _():
        kb[...] = kv_ref[:, :w].astype(BF16)
        vb[...] = kv_ref[:, w:].astype(BF16)
        if mode == "moba":
            km_scr[...] = jnp.zeros_like(km_scr)
            for n in range(n_blk):
                ksum = jnp.sum(kv_ref[n * blk:(n + 1) * blk, :w], axis=0, keepdims=True) * (1.0 / blk)
                for h in range(heads):
                    km_scr[h, n:n + 1, :] = ksum[:, _head_cols(h)]

    qs = [q_ref[:, _head_cols(h)].astype(BF16) for h in range(heads)]

    if mode == "mem":
        for h in range(heads):
            s = _dot_nt(qs[h], kb[:, _head_cols(h)])
            p = jnp.exp(s - jnp.max(s, axis=1, keepdims=True))
            l0 = jnp.sum(p, axis=1, keepdims=True)
            o_ref[:, _head_cols(h)] = (_dot(p.astype(BF16), vb[:, _head_cols(h)]) * (1.0 / l0)).astype(o_ref.dtype)
        return

    if mode == "moba":
        nb8 = km_scr.shape[1]
        blk_id = lax.broadcasted_iota(jnp.int32, (nb8, tq), 0)
        eye = (lax.broadcasted_iota(jnp.int32, (tq, tq), 0)
               == lax.broadcasted_iota(jnp.int32, (tq, tq), 1)).astype(BF16)
        selc = []
        for h in range(heads):
            bs = sum(_dot_nt(part, qs[h]) for part in _split3(km_scr[h]))
            cnt = jnp.zeros((nb8, tq), F32)
            for mth in range(n_blk - 1):
                sm = bs[mth:mth + 1, :]
                beats = jnp.logical_or(sm > bs, jnp.logical_and(sm == bs, mth < blk_id))
                cnt = cnt + jnp.where(beats, 1.0, 0.0) * (mth < qi).astype(F32)
            sel = jnp.where(jnp.logical_and(blk_id < qi, cnt < MOBA_TOPK), 1.0, 0.0)
            sel = jnp.concatenate([sel, jnp.zeros((LANES - nb8, tq), F32)], axis=0).astype(BF16)
            selc.append(_dot_nt(eye, sel))
    else:
        cqs = [cq_ref[:, h:h + 1] for h in range(heads)]

    start = pl.multiple_of(qi * blk, blk)
    row = lax.broadcasted_iota(jnp.int32, (tq, blk), 0)
    col = lax.broadcasted_iota(jnp.int32, (tq, blk), 1)
    for h in range(heads):
        s = _dot_nt(qs[h], kb[pl.ds(start, blk), _head_cols(h)])
        if mode == "fox":
            s = s + cqs[h] - ck_ref[h, pl.ds(qi, 1), :]
        s = jnp.where(col <= row, s, NEG)
        m0 = jnp.max(s, axis=1, keepdims=True)
        p = jnp.exp(s - m0)
        m_scr[h] = m0
        l_scr[h] = jnp.sum(p, axis=1, keepdims=True)
        acc_scr[h] = _dot(p.astype(BF16), vb[pl.ds(start, blk), _head_cols(h)])

    for n in range(n_blk - 1):
        @pl.when(n < qi)
        def _(n=n):
            for h in range(heads):
                s = _dot_nt(qs[h], kb[n * blk:(n + 1) * blk, _head_cols(h)])
                if mode == "fox":
                    s = s + cqs[h] - ck_ref[h, n:n + 1, :]
                else:
                    s = jnp.where(selc[h][:, n:n + 1] > 0.5, s, NEG)
                m_old = m_scr[h]
                m_new = jnp.maximum(m_old, jnp.max(s, axis=1, keepdims=True))
                alpha = jnp.exp(m_old - m_new)
                p = jnp.exp(s - m_new)
                l_scr[h] = alpha * l_scr[h] + jnp.sum(p, axis=1, keepdims=True)
                acc_scr[h] = alpha * acc_scr[h] + _dot(p.astype(BF16), vb[n * blk:(n + 1) * blk, _head_cols(h)])
                m_scr[h] = m_new

    for h in range(heads):
        o_ref[:, _head_cols(h)] = (acc_scr[h] * (1.0 / l_scr[h])).astype(o_ref.dtype)


def _attention(q, kv, *, mode, heads, q_col_block, tq, cq=None, ck=None):
    b, s, _ = q.shape
    skv = kv.shape[1]
    w = heads * HEAD_DIM
    blk = MOBA_BLOCK if mode != "mem" else skv
    n_blk = skv // blk
    if mode != "mem":
        assert tq == blk and s == skv and n_blk <= 16
    kern = functools.partial(_attn_kernel, mode=mode, heads=heads, blk=blk, n_blk=n_blk)
    in_specs = [
        pl.BlockSpec((None, tq, HEAD_TILE), lambda bi, qi: (bi, qi, q_col_block)),
        pl.BlockSpec((None, skv, 2 * w), lambda bi, qi: (bi, 0, 0)),
    ]
    args = [q, kv]
    scratch = [pltpu.VMEM((skv, w), BF16), pltpu.VMEM((skv, w), BF16)]
    if mode == "fox":
        in_specs += [pl.BlockSpec((None, tq, LANES), lambda bi, qi: (bi, qi, 0)),
                     pl.BlockSpec((None, heads, n_blk, blk), lambda bi, qi: (bi, 0, 0, 0))]
        args += [cq, ck]
    if mode == "moba":
        nb8 = -(-n_blk // SUBLANES) * SUBLANES
        scratch += [pltpu.VMEM((heads, nb8, HEAD_DIM), F32)]
    if mode != "mem":
        scratch += [pltpu.VMEM((heads, tq, 1), F32), pltpu.VMEM((heads, tq, 1), F32),
                    pltpu.VMEM((heads, tq, HEAD_DIM), F32)]
    return pl.pallas_call(
        kern,
        grid=(b, s // tq),
        in_specs=in_specs,
        out_specs=pl.BlockSpec((None, tq, w), lambda bi, qi: (bi, qi, 0)),
        out_shape=jax.ShapeDtypeStruct((b, s, w), BF16),
        scratch_shapes=scratch,
        compiler_params=_params(("arbitrary", "arbitrary")),
        name="attn_" + mode,
    )(*args)


def _merge_kernel(om_ref, of_ref, oc_ref, gate_ref, x_ref, wm_ref, wf_ref, wc_ref, wo_ref, o_ref):
    d = x_ref.shape[1]
    merged = gate_ref[:, 0:d].astype(F32) * _dot(om_ref[...], wm_ref[...])
    merged = merged + gate_ref[:, d:2 * d].astype(F32) * _dot(of_ref[...], wf_ref[...])
    merged = merged + gate_ref[:, 2 * d:3 * d].astype(F32) * _dot(oc_ref[...], wc_ref[...])
    o_ref[...] = x_ref[...] + _dot(merged.astype(BF16), wo_ref[...])


def _resident(shape):
    return pl.BlockSpec(shape, lambda i: (0,) * len(shape), pipeline_mode=pl.Buffered(1))


def _merge_out(om, of, oc, gates, x, wm, wf, wc, wo, *, tm):
    m, d = x.shape
    rows = lambda w: pl.BlockSpec((tm, w), lambda i: (i, 0))
    return pl.pallas_call(
        _merge_kernel,
        grid=(m // tm,),
        in_specs=[rows(om.shape[1]), rows(of.shape[1]), rows(oc.shape[1]), rows(gates.shape[1]), rows(d),
                  _resident(wm.shape), _resident(wf.shape), _resident(wc.shape), _resident(wo.shape)],
        out_specs=rows(d),
        out_shape=jax.ShapeDtypeStruct((m, d), F32),
        compiler_params=_params(("arbitrary",)),
        name="merge_out",
    )(om, of, oc, gates, x, wm, wf, wc, wo)


def _ffn_up_kernel(*refs, tiles_per_seq, per_row_state):
    if per_row_state:
        x_ref, g_ref, wa_ref, wb_ref, wc_ref, bc_ref, p0_ref, p1_ref, g_out, a_out, h_scr = refs
    else:
        x_ref, g_ref, wa_ref, wb_ref, wc_ref, bc_ref, g_out, tail_out, h_scr, carry_scr = refs
    i = pl.program_id(0)
    j = pl.program_id(1)

    @pl.when(j == 0)
    def _():
        h_scr[...] = _rms(x_ref[...], g_ref[...]).astype(BF16)

    w = wc_ref[...]
    tm = x_ref.shape[0]
    rc = min(ROW_CHUNK, tm)
    if not per_row_state:
        @pl.when(i % tiles_per_seq == 0)
        def _():
            carry_scr[j] = jnp.zeros(carry_scr.shape[1:], F32)
        prev = carry_scr[j]
    for r0 in range(0, tm, rc):
        rows = slice(r0, r0 + rc)
        h = h_scr[rows, :]
        a = _dot(h, wa_ref[...])
        b = _dot(h, wb_ref[...])
        if per_row_state:
            a1 = p1_ref[rows, :]
            a2 = p0_ref[rows, :]
            a_out[rows, :] = a
        else:
            row = lax.broadcasted_iota(jnp.int32, a.shape, 0)
            a1 = jnp.where(row == 0, prev[7:8, :], pltpu.roll(a, 1, 0))
            a2 = jnp.where(row == 0, prev[6:7, :], jnp.where(row == 1, prev[7:8, :], pltpu.roll(a, 2, 0)))
            prev = a[rc - SUBLANES:, :]
        a_conv = bc_ref[...] + a2 * w[0:1, :] + a1 * w[1:2, :] + a * w[2:3, :]
        g_out[rows, :] = (a_conv * jax.nn.sigmoid(a_conv) * b).astype(g_out.dtype)
    if not per_row_state:
        carry_scr[j] = prev
        tail_out[...] = prev


def _ffn_up(x, g, wa, wb, wconv, bconv, *, tm, tn, seq_len, prev=None):
    m, d = x.shape
    f = wa.shape[1]
    nj = f // tn
    per_row_state = prev is not None
    tiles_per_seq = max(seq_len // tm, 1)
    kern = functools.partial(_ffn_up_kernel, tiles_per_seq=tiles_per_seq, per_row_state=per_row_state)
    col = lambda r: pl.BlockSpec((r, tn), lambda i, j: (0, j))
    in_specs = [pl.BlockSpec((tm, d), lambda i, j: (i, 0)), pl.BlockSpec((1, d), lambda i, j: (0, 0)),
                col(d), col(d), col(SUBLANES), col(1)]
    args = [x, g, wa, wb, wconv, bconv]
    tile = pl.BlockSpec((tm, tn), lambda i, j: (i, j))
    scratch = [pltpu.VMEM((tm, d), BF16)]
    if per_row_state:
        in_specs += [tile, tile]
        args += [prev[0], prev[1]]
        out_specs = [tile, tile]
        out_shape = [jax.ShapeDtypeStruct((m, f), BF16), jax.ShapeDtypeStruct((m, f), F32)]
    else:
        out_specs = [tile, pl.BlockSpec((None, SUBLANES, tn), lambda i, j: (i, 0, j))]
        out_shape = [jax.ShapeDtypeStruct((m, f), BF16), jax.ShapeDtypeStruct((m // tm, SUBLANES, f), F32)]
        scratch += [pltpu.VMEM((nj, SUBLANES, tn), F32)]
    return pl.pallas_call(
        kern, grid=(m // tm, nj), in_specs=in_specs, out_specs=out_specs, out_shape=out_shape,
        scratch_shapes=scratch, compiler_params=_params(("arbitrary", "arbitrary")), name="ffn_up",
    )(*args)


def _ffn_down_kernel(*refs, final):
    if final:
        g_ref, w_ref, x_ref, gf_ref, o_ref, y_ref = refs
    else:
        g_ref, w_ref, x_ref, o_ref = refs
    xo = x_ref[...] + _dot(g_ref[...], w_ref[...])
    o_ref[...] = xo
    if final:
        y_ref[...] = _rms(xo, gf_ref[...])


def _ffn_down(gact, w, x, g_final, *, tm):
    m, d = x.shape
    f = gact.shape[1]
    final = g_final is not None
    rows = lambda wd: pl.BlockSpec((tm, wd), lambda i: (i, 0))
    in_specs = [rows(f), _resident(w.shape), rows(d)]
    args = [gact, w, x]
    out_specs = [rows(d)]
    out_shape = [jax.ShapeDtypeStruct((m, d), F32)]
    if final:
        in_specs += [pl.BlockSpec((1, d), lambda i: (0, 0))]
        args += [g_final]
        out_specs += [rows(d)]
        out_shape += [jax.ShapeDtypeStruct((m, d), F32)]
    res = pl.pallas_call(
        functools.partial(_ffn_down_kernel, final=final),
        grid=(m // tm,), in_specs=in_specs, out_specs=out_specs, out_shape=out_shape,
        compiler_params=_params(("arbitrary",)), name="ffn_down",
    )(*args)
    return (res[0], res[1]) if final else (res[0], None)


def _head_mask(rows, width):
    r = lax.broadcasted_iota(jnp.int32, (rows, width), 0)
    c = lax.broadcasted_iota(jnp.int32, (rows, width), 1)
    return (c // HEAD_DIM) == r


def _page_heads(page_ref, kv, heads):
    return jnp.concatenate(
        [page_ref[pl.ds(2 * h + kv, LANES, stride=2 * heads), :] for h in range(heads)], axis=1)


def _fox_dec_kernel(pt_ref, q_ref, kvn_ref, lfn_ref, lf_ref, *refs, pages_per_step, heads, n_pages):
    page_refs = refs[:pages_per_step]
    o_ref, m_scr, l_scr, acc_scr, run_scr = refs[pages_per_step:]
    b = pl.program_id(0)
    c = pl.program_id(1)
    w = heads * HEAD_DIM
    hm = _head_mask(SUBLANES, w)
    qbd = jnp.where(hm, jnp.broadcast_to(q_ref[...], (SUBLANES, w)), 0.0)

    @pl.when(c == 0)
    def _():
        kvn = kvn_ref[...]
        m_scr[...] = jnp.sum(qbd * kvn[:, :w], axis=1, keepdims=True)
        l_scr[...] = jnp.ones_like(l_scr)
        acc_scr[...] = jnp.broadcast_to(kvn[:, w:], (SUBLANES, w))
        run_scr[...] = lfn_ref[...]

    r = lax.broadcasted_iota(jnp.int32, (LANES, LANES), 0)
    cc = lax.broadcasted_iota(jnp.int32, (LANES, LANES), 1)
    upper = (r > cc).astype(BF16)
    qb = qbd.astype(BF16)
    run = run_scr[...]
    pad = jnp.zeros((SUBLANES - heads, LANES), F32)
    ss = []
    for u in range(pages_per_step):
        pid = pt_ref[b * n_pages + (n_pages - 1 - (c * pages_per_step + u))]
        lf = jnp.concatenate([lf_ref[h, pl.ds(pid, 1), :] for h in range(heads)] + [pad], axis=0)
        s = _dot_nt(qb, _page_heads(page_refs[u], 0, heads).astype(BF16))
        suffix = sum(_dot(part, upper) for part in _split3(lf))
        ss.append(s + run + suffix)
        run = run + jnp.sum(lf, axis=1, keepdims=True)
    s = jnp.concatenate(ss, axis=1)
    m_old = m_scr[...]
    m_new = jnp.maximum(m_old, jnp.max(s, axis=1, keepdims=True))
    alpha = jnp.exp(m_old - m_new)
    p = jnp.exp(s - m_new).astype(BF16)
    l_scr[...] = alpha * l_scr[...] + jnp.sum(p.astype(F32), axis=1, keepdims=True)
    pv = sum(_dot(p[:, u * LANES:(u + 1) * LANES], _page_heads(page_refs[u], 1, heads).astype(BF16))
             for u in range(pages_per_step))
    acc_scr[...] = alpha * acc_scr[...] + pv
    m_scr[...] = m_new
    run_scr[...] = run

    @pl.when(c == pl.num_programs(1) - 1)
    def _():
        o = jnp.where(hm, acc_scr[...] * (1.0 / l_scr[...]), 0.0)
        o_ref[...] = jnp.sum(o, axis=0, keepdims=True).astype(o_ref.dtype)


def _fox_decode(q, kv_new, lf_new, cache_pages, cache_lf, page_table, layer, *, pages_per_step):
    db, n_pages = page_table.shape
    heads = FOX_HEADS
    w = heads * HEAD_DIM
    n_phys = cache_lf.shape[2]
    steps = n_pages // pages_per_step

    def page_map(u):
        def f(b, c, pt):
            return (layer * n_phys + pt[b * n_pages + (n_pages - 1 - (c * pages_per_step + u))], 0, 0)
        return f

    in_specs = [pl.BlockSpec((None, 1, w), lambda b, c, pt: (b, 0, 0)),
                pl.BlockSpec((None, 1, 2 * w), lambda b, c, pt: (b, 0, 0)),
                pl.BlockSpec((None, SUBLANES, 1), lambda b, c, pt: (b, 0, 0)),
                pl.BlockSpec((None, heads, n_phys, LANES), lambda b, c, pt: (layer, 0, 0, 0))]
    in_specs += [pl.BlockSpec((None, 2 * w, LANES), page_map(u)) for u in range(pages_per_step)]
    kern = functools.partial(_fox_dec_kernel, pages_per_step=pages_per_step, heads=heads, n_pages=n_pages)
    return pl.pallas_call(
        kern,
        grid_spec=pltpu.PrefetchScalarGridSpec(
            num_scalar_prefetch=1, grid=(db, steps), in_specs=in_specs,
            out_specs=pl.BlockSpec((None, 1, w), lambda b, c, pt: (b, 0, 0)),
            scratch_shapes=[pltpu.VMEM((SUBLANES, 1), F32), pltpu.VMEM((SUBLANES, 1), F32),
                            pltpu.VMEM((SUBLANES, w), F32), pltpu.VMEM((SUBLANES, 1), F32)]),
        out_shape=jax.ShapeDtypeStruct((db, 1, w), BF16),
        compiler_params=_params(("arbitrary", "arbitrary")),
        name="fox_decode",
    )(page_table.reshape(-1), q, kv_new, lf_new, cache_lf, *([cache_pages] * pages_per_step))


def _moba_sel_kernel(pt_ref, q_ref, *refs, pages_per_step, heads, n_blk):
    del pt_ref
    k_refs = refs[:pages_per_step]
    sel_ref, km_scr = refs[pages_per_step:]
    c = pl.program_id(1)
    w = heads * HEAD_DIM
    pages_per_blk = MOBA_BLOCK // LANES
    blks = pages_per_step // pages_per_blk

    @pl.when(c == 0)
    def _():
        km_scr[...] = jnp.zeros_like(km_scr)

    rows = []
    for bb in range(blks):
        per_head = []
        for h in range(heads):
            tot = jnp.zeros((1, HEAD_DIM), F32)
            for pp in range(pages_per_blk):
                keys = k_refs[bb * pages_per_blk + pp][pl.ds(2 * h, LANES, stride=2 * heads), :]
                tot = tot + jnp.sum(keys, axis=0, keepdims=True)
            per_head.append(tot)
        rows.append(jnp.concatenate(per_head, axis=1) * (1.0 / MOBA_BLOCK))
    km_scr[pl.ds(pl.multiple_of(c * blks, blks), blks), :] = jnp.concatenate(rows, axis=0)

    @pl.when(c == pl.num_programs(1) - 1)
    def _():
        hm = _head_mask(SUBLANES, w)
        qbd = jnp.where(hm, jnp.broadcast_to(q_ref[...], (SUBLANES, w)), 0.0)
        km = km_scr[...]
        bs = jnp.zeros((SUBLANES, LANES), F32)
        for qp in _split3(qbd):
            for kp in _split3(km):
                bs = bs + _dot_nt(qp, kp)
        lane = lax.broadcasted_iota(jnp.int32, (SUBLANES, LANES), 1)
        bs = jnp.where(lane < n_blk, bs, -jnp.inf)
        out = jnp.zeros((SUBLANES, LANES), jnp.int32)
        for t in range(MOBA_TOPK):
            mx = jnp.max(bs, axis=1, keepdims=True)
            idx = jnp.min(jnp.where(bs == mx, lane, LANES), axis=1, keepdims=True)
            out = jnp.where(lane == t, idx, out)
            bs = jnp.where(lane == idx, -jnp.inf, bs)
        sel_ref[...] = out


def _moba_select(q, cache_pages, page_table, layer_page0, *, pages_per_step):
    db, n_pages = page_table.shape
    heads = MOBA_HEADS
    w = heads * HEAD_DIM
    n_blk = n_pages * LANES // MOBA_BLOCK
    assert MOBA_TOPK <= n_blk <= LANES and pages_per_step % (SUBLANES * MOBA_BLOCK // LANES) == 0
    steps = n_pages // pages_per_step

    def page_map(u):
        return lambda b, c, pt: (layer_page0 + pt[b * n_pages + c * pages_per_step + u], 0, 0)

    in_specs = [pl.BlockSpec((None, 1, w), lambda b, c, pt: (b, 0, 0))]
    in_specs += [pl.BlockSpec((None, 2 * w, LANES), page_map(u)) for u in range(pages_per_step)]
    kern = functools.partial(_moba_sel_kernel, pages_per_step=pages_per_step, heads=heads, n_blk=n_blk)
    return pl.pallas_call(
        kern,
        grid_spec=pltpu.PrefetchScalarGridSpec(
            num_scalar_prefetch=1, grid=(db, steps), in_specs=in_specs,
            out_specs=pl.BlockSpec((None, SUBLANES, LANES), lambda b, c, pt: (b, 0, 0)),
            scratch_shapes=[pltpu.VMEM((LANES, w), F32)]),
        out_shape=jax.ShapeDtypeStruct((db, SUBLANES, LANES), jnp.int32),
        compiler_params=_params(("arbitrary", "arbitrary")),
        name="moba_select",
    )(page_table.reshape(-1), q, *([cache_pages] * pages_per_step))


def _moba_dec_kernel(sel_ref, pt_ref, q_ref, kn_ref, vn_ref, *refs, n_pages_sel):
    del sel_ref, pt_ref
    page_refs = refs[:n_pages_sel]
    o_ref = refs[n_pages_sel]
    q = q_ref[...]
    qb = jnp.broadcast_to(q, (SUBLANES, HEAD_DIM)).astype(BF16)
    s_self = jnp.sum(q * kn_ref[...], axis=1, keepdims=True)
    s = jnp.concatenate([_dot_nt(qb, page_refs[u][:, 0, :].astype(BF16)) for u in range(n_pages_sel)], axis=1)
    m = jnp.maximum(s_self, jnp.max(s, axis=1, keepdims=True))
    p_self = jnp.exp(s_self - m)
    p = jnp.exp(s - m).astype(BF16)
    l = p_self + jnp.sum(p.astype(F32), axis=1, keepdims=True)
    acc = p_self * vn_ref[...]
    for u in range(n_pages_sel):
        acc = acc + _dot(p[:, u * LANES:(u + 1) * LANES], page_refs[u][:, 1, :].astype(BF16))
    o_ref[...] = (acc * (1.0 / l))[0:1, :].astype(o_ref.dtype)


def _moba_decode(q, kv_new, sel, cache_rows, page_table, layer_page0):
    db, n_pages = page_table.shape
    heads = MOBA_HEADS
    pages_per_blk = MOBA_BLOCK // LANES
    n_pages_sel = MOBA_TOPK * pages_per_blk

    def page_map(u):
        jsel, pg = divmod(u, pages_per_blk)

        def f(b, h, sel_s, pt):
            blk = sel_s[(b * heads + h) * MOBA_TOPK + jsel]
            return (layer_page0 + pt[b * n_pages + blk * pages_per_blk + pg], h, 0, 0)
        return f

    one = lambda col0: pl.BlockSpec((None, 1, HEAD_DIM), lambda b, h, sel_s, pt: (b, 0, col0 + h))
    in_specs = [one(0), one(0), one(heads)]
    in_specs += [pl.BlockSpec((LANES, None, 2, HEAD_DIM), page_map(u)) for u in range(n_pages_sel)]
    kern = functools.partial(_moba_dec_kernel, n_pages_sel=n_pages_sel)
    return pl.pallas_call(
        kern,
        grid_spec=pltpu.PrefetchScalarGridSpec(
            num_scalar_prefetch=2, grid=(db, heads), in_specs=in_specs,
            out_specs=pl.BlockSpec((None, 1, HEAD_DIM), lambda b, h, sel_s, pt: (b, 0, h))),
        out_shape=jax.ShapeDtypeStruct((db, 1, heads * HEAD_DIM), BF16),
        compiler_params=_params(("arbitrary", "arbitrary")),
        name="moba_decode",
    )(sel.reshape(-1), page_table.reshape(-1), q, kv_new, kv_new, *([cache_rows] * n_pages_sel))


def _mem_dec_kernel(q_ref, kv_ref, o_ref, *, heads, n_mem):
    outs = []
    for h in range(heads):
        qb = jnp.broadcast_to(q_ref[:, _head_cols(h)], (SUBLANES, HEAD_DIM)).astype(BF16)
        k = kv_ref[pl.ds(h, n_mem, stride=2 * heads), :].astype(BF16)
        v = kv_ref[pl.ds(heads + h, n_mem, stride=2 * heads), :].astype(BF16)
        s = _dot_nt(qb, k)
        p = jnp.exp(s - jnp.max(s, axis=1, keepdims=True))
        l = jnp.sum(p, axis=1, keepdims=True)
        outs.append((_dot(p.astype(BF16), v) * (1.0 / l))[0:1, :])
    o_ref[...] = jnp.concatenate(outs, axis=1).astype(o_ref.dtype)


def _mem_decode(q, mem_rows, layer, *, q_col_block):
    db = q.shape[0]
    heads = MEM_HEADS
    n_mem = mem_rows.shape[1] // (2 * heads)
    return pl.pallas_call(
        functools.partial(_mem_dec_kernel, heads=heads, n_mem=n_mem),
        grid=(db,),
        in_specs=[pl.BlockSpec((None, 1, HEAD_TILE), lambda b: (b, 0, q_col_block)),
                  pl.BlockSpec((None, mem_rows.shape[1], HEAD_DIM), lambda b: (layer * db + b, 0, 0))],
        out_specs=pl.BlockSpec((None, 1, heads * HEAD_DIM), lambda b: (b, 0, 0)),
        out_shape=jax.ShapeDtypeStruct((db, 1, heads * HEAD_DIM), BF16),
        compiler_params=_params(("arbitrary",)),
        name="mem_decode",
    )(q, mem_rows)


def _rope_tables(pos):
    half = HEAD_DIM // 2
    inv_freq = ROPE_THETA ** (-jnp.arange(half, dtype=F32) / half)
    ang = pos.astype(F32)[:, None] * inv_freq[None, :]
    cos, sin = jnp.cos(ang), jnp.sin(ang)
    return jnp.concatenate([cos, cos], axis=-1), jnp.concatenate([-sin, sin], axis=-1)


def _pad_cols(a, n):
    return jnp.pad(a, ((0, 0), (0, n - a.shape[1])))


def _prep_layer(l, g_attn, w_in, b_f, w_br_moba, w_br_fox, w_br_mem, w_out, g_mem, w_mem_kv,
                g_ffn, w_up, w_conv, b_conv, w_down, ff_tile):
    mw, fw, cw = MOBA_HEADS * HEAD_DIM, FOX_HEADS * HEAD_DIM, MEM_HEADS * HEAD_DIM
    o_fl = 3 * mw + 3 * fw
    o_qc = o_fl + FOX_HEADS
    o_gl = o_qc + cw
    wi = w_in[l]
    w_main = jnp.concatenate(
        [wi[:, :o_fl], _pad_cols(wi[:, o_qc:o_gl], HEAD_TILE), wi[:, o_gl:]], axis=1).astype(BF16)
    w_fl = _pad_cols(wi[:, o_fl:o_qc], LANES).astype(BF16)
    f = w_down.shape[1]
    fp = -(-f // ff_tile) * ff_tile
    wu = w_up[l]
    return dict(
        g_attn=g_attn[l][None], w_main=w_main, w_fl=w_fl, b_f=_pad_cols(b_f[l][None], LANES),
        wm=w_br_moba[l].astype(BF16), wf=w_br_fox[l].astype(BF16), wc=w_br_mem[l].astype(BF16),
        wo=w_out[l].astype(BF16), g_mem=g_mem[l][None], w_mem=w_mem_kv[l].astype(BF16),
        g_ffn=g_ffn[l][None], wa=_pad_cols(wu[:, :f], fp).astype(BF16), wb=_pad_cols(wu[:, f:], fp).astype(BF16),
        w_conv=jnp.pad(w_conv[l], ((0, SUBLANES - CONV_W), (0, fp - f))), b_conv=_pad_cols(b_conv[l][None], fp),
        w_down=jnp.pad(w_down[l], ((0, fp - f), (0, 0))).astype(BF16), f=f)


def kernel(x_prompt, x_sample, cache_moba_kv, cache_fox_kv, cache_fox_logf, cache_mem_kv, state_ffn_conv,
           page_table, mem_prompt, g_attn, w_in, b_f, w_br_moba, w_br_fox, w_br_mem, w_out, g_mem,
           w_mem_kv, g_ffn, w_up, w_conv, b_conv, w_down, g_final):
    bsz, seq, d = x_prompt.shape
    db = x_sample.shape[0]
    depth, n_phys, page = cache_moba_kv.shape[:3]
    n_pages = page_table.shape[1]
    past_len = n_pages * page
    n_mem = mem_prompt.shape[1]
    mw, fw, cw = MOBA_HEADS * HEAD_DIM, FOX_HEADS * HEAD_DIM, MEM_HEADS * HEAD_DIM
    assert page == LANES and db == SUBLANES and x_sample.shape[1] == 1
    ff_tile = 512
    tm = min(512, seq)
    tq = MOBA_BLOCK

    cos_p, sin_p = _rope_tables(jnp.arange(seq))
    cos_s, sin_s = _rope_tables(jnp.full((db,), past_len))
    gf = g_final[None]

    moba_t = jnp.transpose(cache_moba_kv, (0, 1, 2, 4, 3, 5))
    moba_pages = moba_t.reshape(depth * n_phys, page * 2 * MOBA_HEADS, HEAD_DIM)
    moba_rows = moba_t.reshape(depth * n_phys * page, MOBA_HEADS, 2, HEAD_DIM)
    fox_pages = jnp.transpose(cache_fox_kv, (0, 1, 2, 4, 3, 5)).reshape(depth * n_phys, page * 2 * FOX_HEADS, HEAD_DIM)
    lf_cache = jnp.transpose(cache_fox_logf, (0, 3, 1, 2))
    mem_rows = cache_mem_kv.reshape(depth * db, n_mem * 2 * MEM_HEADS, HEAD_DIM)

    xp = x_prompt.reshape(bsz * seq, d)
    xs = x_sample.reshape(db, d)
    outs = {k: [] for k in ("p_moba", "p_fox", "p_logf", "p_mem", "p_conv", "s_moba", "s_fox", "s_logf", "s_conv")}
    yp = ys = None
    for l in range(depth):
        w = _prep_layer(l, g_attn, w_in, b_f, w_br_moba, w_br_fox, w_br_mem, w_out, g_mem, w_mem_kv,
                        g_ffn, w_up, w_conv, b_conv, w_down, ff_tile)
        f = w["f"]
        g_last = gf if l == depth - 1 else None

        mkv = _norm_mm(mem_prompt.reshape(bsz * n_mem, d), w["g_mem"], w["w_mem"], tm=min(512, bsz * n_mem))
        q_all, moba_kv, fox_kv, gates, logf, c = _in_proj(
            xp, w["g_attn"], w["w_main"], w["w_fl"], w["b_f"], cos_p, sin_p, tm=tm, seq_len=seq, q_dtype=BF16)
        q3 = q_all.reshape(bsz, seq, 3 * HEAD_TILE)
        c3 = c.reshape(bsz, seq, LANES)
        ck = jnp.swapaxes(c3[:, :, :FOX_HEADS], 1, 2).reshape(bsz, FOX_HEADS, seq // tq, tq)
        o_m = _attention(q3, moba_kv.reshape(bsz, seq, 2 * mw), mode="moba", heads=MOBA_HEADS, q_col_block=0, tq=tq)
        o_f = _attention(q3, fox_kv.reshape(bsz, seq, 2 * fw), mode="fox", heads=FOX_HEADS,
                         q_col_block=1, tq=tq, cq=c3, ck=ck)
        o_c = _attention(q3, mkv.reshape(bsz, n_mem, 2 * cw), mode="mem", heads=MEM_HEADS, q_col_block=2, tq=tq)
        xp = _merge_out(o_m.reshape(bsz * seq, mw), o_f.reshape(bsz * seq, fw), o_c.reshape(bsz * seq, cw),
                        gates, xp, w["wm"], w["wf"], w["wc"], w["wo"], tm=256)
        gact, tails = _ffn_up(xp, w["g_ffn"], w["wa"], w["wb"], w["w_conv"], w["b_conv"],
                              tm=tm, tn=ff_tile, seq_len=seq)
        xp, yp = _ffn_down(gact, w["w_down"], xp, g_last, tm=256)
        outs["p_moba"].append(moba_kv.reshape(bsz, seq, 2, MOBA_HEADS, HEAD_DIM))
        outs["p_fox"].append(fox_kv.reshape(bsz, seq, 2, FOX_HEADS, HEAD_DIM))
        outs["p_logf"].append(logf[:, :FOX_HEADS].reshape(bsz, seq, FOX_HEADS))
        outs["p_mem"].append(mkv.reshape(bsz, n_mem, 2, MEM_HEADS, HEAD_DIM))
        tiles_per_seq = seq // tm
        outs["p_conv"].append(tails[tiles_per_seq - 1::tiles_per_seq, SUBLANES - (CONV_W - 1):, :f])

        q_s, moba_s, fox_s, gates_s, logf_s, _ = _in_proj(
            xs, w["g_attn"], w["w_main"], w["w_fl"], w["b_f"], cos_s, sin_s, tm=db, seq_len=1, q_dtype=F32)
        q_s3 = q_s.reshape(db, 1, 3 * HEAD_TILE)
        sel = _moba_select(q_s3, moba_pages, page_table, l * n_phys, pages_per_step=16)
        o_ms = _moba_decode(q_s3, moba_s.reshape(db, 1, 2 * mw), sel[:, :MOBA_HEADS, :MOBA_TOPK],
                            moba_rows, page_table, l * n_phys)
        o_fs = _fox_decode(q_s3[:, :, mw:mw + fw], fox_s.reshape(db, 1, 2 * fw),
                           logf_s[:, :SUBLANES].reshape(db, SUBLANES, 1), fox_pages, lf_cache, page_table,
                           l, pages_per_step=8)
        o_cs = _mem_decode(q_s3, mem_rows, l, q_col_block=2)
        xs = _merge_out(o_ms.reshape(db, mw), o_fs.reshape(db, fw), o_cs.reshape(db, cw), gates_s, xs,
                        w["wm"], w["wf"], w["wc"], w["wo"], tm=db)
        fp = w["wa"].shape[1]
        prev = jnp.pad(state_ffn_conv[l], ((0, 0), (0, 0), (0, fp - f)))
        gact_s, a_s = _ffn_up(xs, w["g_ffn"], w["wa"], w["wb"], w["w_conv"], w["b_conv"],
                              tm=db, tn=ff_tile, seq_len=1, prev=(prev[:, 0], prev[:, 1]))
        xs, ys = _ffn_down(gact_s, w["w_down"], xs, g_last, tm=db)
        outs["s_moba"].append(moba_s.reshape(db, 1, 2, MOBA_HEADS, HEAD_DIM))
        outs["s_fox"].append(fox_s.reshape(db, 1, 2, FOX_HEADS, HEAD_DIM))
        outs["s_logf"].append(logf_s[:, :FOX_HEADS].reshape(db, 1, FOX_HEADS))
        outs["s_conv"].append(jnp.stack([state_ffn_conv[l][:, 1], a_s[:, :f]], axis=1))

    st = lambda k: jnp.stack(outs[k])
    return (yp.reshape(bsz, seq, d), ys.reshape(db, 1, d),
            st("p_moba"), st("p_fox"), st("p_logf"), st("p_mem"), st("p_conv"),
            st("s_moba"), st("s_fox"), st("s_logf"), st("s_conv"))
```

```python
import functools

import jax
import jax.numpy as jnp
from jax import lax
from jax.experimental import pallas as pl
from jax.experimental.pallas import tpu as pltpu

F32 = jnp.float32
BF16 = jnp.bfloat16

HEAD_DIM = 128
MOBA_HEADS = 6
FOX_HEADS = 6
MEM_HEADS = 4
MOBA_BLOCK = 256
MOBA_TOPK = 3
CONV_W = 3
ROPE_THETA = 10000.0
NORM_EPS = 1e-6
NEG = -1e30

LANES = 128
SUBLANES = 8
HEAD_TILE = MOBA_HEADS * HEAD_DIM
VMEM_LIMIT = 56 * 1024 * 1024
ROW_CHUNK = 256
SCALE = HEAD_DIM ** -0.5

_NT = (((1,), (1,)), ((), ()))


def _params(sem):
    return pltpu.CompilerParams(dimension_semantics=sem, vmem_limit_bytes=VMEM_LIMIT)


def _rms(x, g):
    return x * lax.rsqrt(jnp.mean(x * x, axis=-1, keepdims=True) + NORM_EPS) * g


def _log_sigmoid(x):
    return jnp.minimum(x, 0.0) - jnp.log1p(jnp.exp(-jnp.abs(x)))


def _split3(v):
    hi = v.astype(BF16)
    r1 = v - hi.astype(F32)
    mid = r1.astype(BF16)
    lo = (r1 - mid.astype(F32)).astype(BF16)
    return hi, mid, lo


def _dot(a, b):
    return jnp.dot(a, b, preferred_element_type=F32)


def _dot_nt(a, b):
    return lax.dot_general(a, b, _NT, preferred_element_type=F32)


def _cumsum_rows(v, carry, blk=256):
    tm = v.shape[0]
    blk = min(blk, tm)
    r = lax.broadcasted_iota(jnp.int32, (blk, blk), 0)
    c = lax.broadcasted_iota(jnp.int32, (blk, blk), 1)
    tri = (c <= r).astype(BF16)
    outs = []
    for s in range(0, tm, blk):
        hi, mid, lo = _split3(v[s:s + blk])
        cs = _dot(tri, hi) + _dot(tri, mid) + _dot(tri, lo) + carry
        outs.append(cs)
        carry = cs[blk - 1:blk, :]
    return jnp.concatenate(outs, axis=0), carry


def _rope(acc, cos, sin_signed):
    outs = []
    for hh in range(acc.shape[1] // HEAD_DIM):
        xh = acc[:, hh * HEAD_DIM:(hh + 1) * HEAD_DIM]
        outs.append(xh * cos + pltpu.roll(xh, HEAD_DIM // 2, 1) * sin_signed)
    return jnp.concatenate(outs, axis=1)


def _head_cols(h):
    return slice(h * HEAD_DIM, (h + 1) * HEAD_DIM)


W_IN_COLS = 256


def _w_in_kernel(starts_ref, x_ref, o_ref):
    del starts_ref
    cols, kt, layers, _ = x_ref.shape
    for l in range(layers):
        for t in range(kt):
            o_ref[l, t * LANES:(t + 1) * LANES, :] = x_ref[:, t, l, :].T.astype(o_ref.dtype)


def _w_in_layout(w_t, starts, cols):
    _, kt, layers, _ = w_t.shape
    starts = jnp.asarray(starts, jnp.int32)
    return pl.pallas_call(
        _w_in_kernel,
        grid_spec=pltpu.PrefetchScalarGridSpec(
            num_scalar_prefetch=1, grid=(starts.shape[0],),
            in_specs=[pl.BlockSpec((pl.Element(cols), pl.Element(kt), pl.Element(layers), pl.Element(LANES)),
                                   lambda j, st: (st[j], 0, 0, 0))],
            out_specs=pl.BlockSpec((layers, kt * LANES, cols), lambda j, st: (0, 0, j))),
        out_shape=jax.ShapeDtypeStruct((layers, kt * LANES, starts.shape[0] * cols), BF16),
        compiler_params=_params(("arbitrary",)),
        name="w_in_layout",
    )(starts, w_t)


def _in_proj_kernel(x_ref, g_ref, w_ref, wfl_ref, bf_ref, cos_ref, sin_ref,
                    q_ref, mkv_ref, fkv_ref, gate_ref, logf_ref, c_ref,
                    h_scr, carry_scr, *, tiles_per_seq, cumsum):
    i = pl.program_id(0)
    j = pl.program_id(1)

    @pl.when(j == 0)
    def _():
        h = _rms(x_ref[...], g_ref[...]).astype(BF16)
        h_scr[...] = h
        logf = _log_sigmoid(_dot(h, wfl_ref[...]) + bf_ref[...])
        logf_ref[...] = logf
        if cumsum:
            @pl.when(i % tiles_per_seq == 0)
            def _():
                carry_scr[...] = jnp.zeros_like(carry_scr)
            c, carry = _cumsum_rows(logf, carry_scr[...])
            c_ref[...] = c
            carry_scr[...] = carry
        else:
            c_ref[...] = logf

    tm = x_ref.shape[0]
    rc = min(ROW_CHUNK, tm)

    def tile(out_ref, epilogue):
        for r0 in range(0, tm, rc):
            rows = slice(r0, r0 + rc)
            acc = _dot(h_scr[rows, :], w_ref[...])
            out_ref[rows, :] = epilogue(acc, rows).astype(out_ref.dtype)

    rope = lambda acc, rows: _rope(acc, cos_ref[rows, :], sin_ref[rows, :])

    @pl.when(j == 0)
    def _():
        tile(q_ref, lambda acc, rows: rope(acc, rows) * SCALE)

    @pl.when(j == 1)
    def _():
        tile(mkv_ref, rope)

    @pl.when(j == 2)
    def _():
        tile(mkv_ref, lambda acc, rows: acc)

    @pl.when(jnp.logical_or(j == 3, j == 6))
    def _():
        tile(q_ref, lambda acc, rows: acc * SCALE)

    @pl.when(jnp.logical_or(j == 4, j == 5))
    def _():
        tile(fkv_ref, lambda acc, rows: acc)

    @pl.when(j >= 7)
    def _():
        tile(gate_ref, lambda acc, rows: jax.nn.sigmoid(acc))


def _in_proj(x, g, w_main, w_fl, b_f, cos, sin, *, tm, seq_len, q_dtype):
    m, d = x.shape
    n_tiles = w_main.shape[1] // HEAD_TILE
    n_gate = n_tiles - 7
    table_tiles = cos.shape[0] // tm
    cumsum = seq_len > 1
    tiles_per_seq = max(seq_len // tm, 1)
    kern = functools.partial(_in_proj_kernel, tiles_per_seq=tiles_per_seq, cumsum=cumsum)
    row = lambda i, j: (i, 0)
    return pl.pallas_call(
        kern,
        grid=(m // tm, n_tiles),
        in_specs=[
            pl.BlockSpec((tm, d), row, pipeline_mode=pl.Buffered(1)),
            pl.BlockSpec((1, d), lambda i, j: (0, 0)),
            pl.BlockSpec((d, HEAD_TILE), lambda i, j: (0, j)),
            pl.BlockSpec((d, LANES), lambda i, j: (0, 0)),
            pl.BlockSpec((1, LANES), lambda i, j: (0, 0)),
            pl.BlockSpec((tm, LANES), lambda i, j: (i % table_tiles, 0)),
            pl.BlockSpec((tm, LANES), lambda i, j: (i % table_tiles, 0)),
        ],
        out_specs=[
            pl.BlockSpec((tm, HEAD_TILE), lambda i, j: (i, jnp.minimum(j // 3, 2))),
            pl.BlockSpec((tm, HEAD_TILE), lambda i, j: (i, jnp.clip(j - 1, 0, 1))),
            pl.BlockSpec((tm, HEAD_TILE), lambda i, j: (i, jnp.clip(j - 4, 0, 1))),
            pl.BlockSpec((tm, HEAD_TILE), lambda i, j: (i, jnp.clip(j - 7, 0, n_gate - 1))),
            pl.BlockSpec((tm, LANES), row),
            pl.BlockSpec((tm, LANES), row),
        ],
        out_shape=[
            jax.ShapeDtypeStruct((m, 3 * HEAD_TILE), q_dtype),
            jax.ShapeDtypeStruct((m, 2 * HEAD_TILE), F32),
            jax.ShapeDtypeStruct((m, 2 * HEAD_TILE), F32),
            jax.ShapeDtypeStruct((m, n_gate * HEAD_TILE), BF16),
            jax.ShapeDtypeStruct((m, LANES), F32),
            jax.ShapeDtypeStruct((m, LANES), F32),
        ],
        scratch_shapes=[pltpu.VMEM((tm, d), BF16), pltpu.VMEM((1, LANES), F32)],
        compiler_params=_params(("arbitrary", "arbitrary")),
        name="in_proj",
    )(x, g, w_main, w_fl, b_f, cos, sin)


def _norm_mm_kernel(x_ref, g_ref, w_ref, o_ref):
    h = _rms(x_ref[...], g_ref[...]).astype(BF16)
    o_ref[...] = _dot(h, w_ref[...])


def _norm_mm(x, g, w, *, tm):
    m, d = x.shape
    n = w.shape[1]
    return pl.pallas_call(
        _norm_mm_kernel,
        grid=(m // tm,),
        in_specs=[pl.BlockSpec((tm, d), lambda i: (i, 0)),
                  pl.BlockSpec((1, d), lambda i: (0, 0)),
                  pl.BlockSpec((d, n), lambda i: (0, 0))],
        out_specs=pl.BlockSpec((tm, n), lambda i: (i, 0)),
        out_shape=jax.ShapeDtypeStruct((m, n), F32),
        compiler_params=_params(("arbitrary",)),
        name="mem_kv_proj",
    )(x, g, w)


MASK_BIAS = 1e30


def _causal_attn_kernel(*refs, mode, blk, n_blk):
    if mode == "fox":
        q_ref, k_ref, v_ref, c_ref, o_ref = refs
    else:
        q_ref, k_ref, v_ref, o_ref = refs
    h = pl.program_id(1)
    s_len = q_ref.shape[0]
    q = q_ref[...]
    kf = k_ref[...]
    vb = v_ref[...].astype(BF16)
    lane = lax.broadcasted_iota(jnp.int32, (s_len, LANES), 1)

    if mode == "fox":
        c_col = jnp.sum(jnp.where(lane == h, c_ref[...], 0.0), axis=1, keepdims=True)
        hi, mid, lo = [part.astype(F32) for part in _split3(c_col)]
        aug_q = jnp.where(lane == 0, hi, jnp.where(lane == 1, mid, jnp.where(lane == 2, lo,
                          jnp.where(lane < 6, 1.0, 0.0))))
        aug_k = jnp.where(lane < 3, 1.0, jnp.where(lane == 3, -hi, jnp.where(lane == 4, -mid,
                          jnp.where(lane == 5, -lo, 0.0))))
    else:
        nb8 = -(-n_blk // SUBLANES) * SUBLANES
        km = [jnp.sum(kf[n * blk:(n + 1) * blk], axis=0, keepdims=True) * (1.0 / blk) for n in range(n_blk)]
        km = jnp.concatenate(km + [jnp.zeros((1, HEAD_DIM), F32)] * (nb8 - n_blk), axis=0)
        bs = sum(_dot_nt(part, q) for part in _split3(km))
        blk_id = lax.broadcasted_iota(jnp.int32, (nb8, s_len), 0)
        own = lax.broadcasted_iota(jnp.int32, (nb8, s_len), 1) // blk
        cnt = jnp.zeros((nb8, s_len), F32)
        for mth in range(n_blk - 1):
            sm = bs[mth:mth + 1, :]
            beats = jnp.logical_or(sm > bs, jnp.logical_and(sm == bs, mth < blk_id))
            cnt = cnt + jnp.where(jnp.logical_and(beats, mth < own), 1.0, 0.0)
        sel = jnp.where(jnp.logical_and(blk_id < own, cnt < MOBA_TOPK), 1.0, 0.0)
        sel = jnp.concatenate([sel, jnp.zeros((LANES - nb8, s_len), F32)], axis=0).astype(BF16)
        eye = (lax.broadcasted_iota(jnp.int32, (blk, blk), 0)
               == lax.broadcasted_iota(jnp.int32, (blk, blk), 1)).astype(BF16)
        selc = jnp.concatenate([_dot_nt(eye, sel[:, t * blk:(t + 1) * blk]) for t in range(n_blk)], axis=0)
        row_blk = lax.broadcasted_iota(jnp.int32, (s_len, LANES), 0) // blk
        aug_q = jnp.where(lane == row_blk, 0.0, (selc - 1.0) * MASK_BIAS)
        aug_k = jnp.where(lane == row_blk, 1.0, 0.0)

    q_aug = jnp.concatenate([q, aug_q.astype(BF16)], axis=1)
    k_aug = jnp.concatenate([kf.astype(BF16), aug_k.astype(BF16)], axis=1)
    row = lax.broadcasted_iota(jnp.int32, (blk, blk), 0)
    col = lax.broadcasted_iota(jnp.int32, (blk, blk), 1)
    for t in range(n_blk):
        n = (t + 1) * blk
        s = _dot_nt(q_aug[t * blk:(t + 1) * blk], k_aug[:n])
        diag = jnp.where(col <= row, s[:, n - blk:], NEG)
        s = diag if t == 0 else jnp.concatenate([s[:, :n - blk], diag], axis=1)
        p = jnp.exp(s - jnp.max(s, axis=1, keepdims=True))
        l = jnp.sum(p, axis=1, keepdims=True)
        o_ref[t * blk:(t + 1) * blk, :] = (_dot(p.astype(BF16), vb[:n]) * (1.0 / l)).astype(o_ref.dtype)


def _causal_attention(q, kv, *, mode, heads, q_col0, c=None):
    b, s, _ = q.shape
    blk = MOBA_BLOCK
    n_blk = s // blk
    assert s % blk == 0 and n_blk <= LANES and kv.shape[1] == s
    seq_head = lambda col0: pl.BlockSpec((None, s, HEAD_DIM), lambda bi, h: (bi, 0, col0 + h))
    in_specs = [seq_head(q_col0), seq_head(0), seq_head(heads)]
    args = [q, kv, kv]
    if mode == "fox":
        in_specs += [pl.BlockSpec((None, s, LANES), lambda bi, h: (bi, 0, 0))]
        args += [c]
    return pl.pallas_call(
        functools.partial(_causal_attn_kernel, mode=mode, blk=blk, n_blk=n_blk),
        grid=(b, heads),
        in_specs=in_specs,
        out_specs=seq_head(0),
        out_shape=jax.ShapeDtypeStruct((b, s, heads * HEAD_DIM), BF16),
        compiler_params=_params(("arbitrary", "arbitrary")),
        name="attn_" + mode,
    )(*args)


def _mem_attn_kernel(q_ref, kv_ref, o_ref, kb, vb, *, heads):
    w = heads * HEAD_DIM

    @pl.when(pl.program_id(1) == 0)
    def _():
        kb[...] = kv_ref[:, :w].astype(BF16)
        vb[...] = kv_ref[:, w:].astype(BF16)

    for h in range(heads):
        s = _dot_nt(q_ref[:, _head_cols(h)], kb[:, _head_cols(h)])
        p = jnp.exp(s - jnp.max(s, axis=1, keepdims=True))
        l = jnp.sum(p, axis=1, keepdims=True)
        o_ref[:, _head_cols(h)] = (_dot(p.astype(BF16), vb[:, _head_cols(h)]) * (1.0 / l)).astype(o_ref.dtype)


def _mem_attention(q, kv, *, heads, q_col_block, tq):
    b, s, _ = q.shape
    n_mem = kv.shape[1]
    w = heads * HEAD_DIM
    return pl.pallas_call(
        functools.partial(_mem_attn_kernel, heads=heads),
        grid=(b, s // tq),
        in_specs=[pl.BlockSpec((None, tq, HEAD_TILE), lambda bi, qi: (bi, qi, q_col_block)),
                  pl.BlockSpec((None, n_mem, 2 * w), lambda bi, qi: (bi, 0, 0))],
        out_specs=pl.BlockSpec((None, tq, w), lambda bi, qi: (bi, qi, 0)),
        out_shape=jax.ShapeDtypeStruct((b, s, w), BF16),
        scratch_shapes=[pltpu.VMEM((n_mem, w), BF16), pltpu.VMEM((n_mem, w), BF16)],
        compiler_params=_params(("arbitrary", "arbitrary")),
        name="attn_mem",
    )(q, kv)


def _merge_kernel(om_ref, of_ref, oc_ref, gate_ref, x_ref, wm_ref, wf_ref, wc_ref, wo_ref, o_ref):
    d = x_ref.shape[1]
    merged = gate_ref[:, 0:d].astype(F32) * _dot(om_ref[...], wm_ref[...])
    merged = merged + gate_ref[:, d:2 * d].astype(F32) * _dot(of_ref[...], wf_ref[...])
    merged = merged + gate_ref[:, 2 * d:3 * d].astype(F32) * _dot(oc_ref[...], wc_ref[...])
    o_ref[...] = x_ref[...] + _dot(merged.astype(BF16), wo_ref[...])


def _resident(shape):
    return pl.BlockSpec(shape, lambda i: (0,) * len(shape), pipeline_mode=pl.Buffered(1))


def _merge_out(om, of, oc, gates, x, wm, wf, wc, wo, *, tm):
    m, d = x.shape
    rows = lambda w: pl.BlockSpec((tm, w), lambda i: (i, 0))
    return pl.pallas_call(
        _merge_kernel,
        grid=(m // tm,),
        in_specs=[rows(om.shape[1]), rows(of.shape[1]), rows(oc.shape[1]), rows(gates.shape[1]), rows(d),
                  _resident(wm.shape), _resident(wf.shape), _resident(wc.shape), _resident(wo.shape)],
        out_specs=rows(d),
        out_shape=jax.ShapeDtypeStruct((m, d), F32),
        compiler_params=_params(("arbitrary",)),
        name="merge_out",
    )(om, of, oc, gates, x, wm, wf, wc, wo)


def _ffn_up_kernel(*refs, tiles_per_seq, per_row_state):
    if per_row_state:
        x_ref, g_ref, wa_ref, wb_ref, wc_ref, bc_ref, p0_ref, p1_ref, g_out, a_out, h_scr = refs
    else:
        x_ref, g_ref, wa_ref, wb_ref, wc_ref, bc_ref, g_out, tail_out, h_scr, carry_scr = refs
    i = pl.program_id(0)
    j = pl.program_id(1)

    @pl.when(j == 0)
    def _():
        h_scr[...] = _rms(x_ref[...], g_ref[...]).astype(BF16)

    w = wc_ref[...]
    tm = x_ref.shape[0]
    rc = min(ROW_CHUNK, tm)
    if not per_row_state:
        @pl.when(i % tiles_per_seq == 0)
        def _():
            carry_scr[j] = jnp.zeros(carry_scr.shape[1:], F32)
        prev = carry_scr[j]
    for r0 in range(0, tm, rc):
        rows = slice(r0, r0 + rc)
        h = h_scr[rows, :]
        a = _dot(h, wa_ref[...])
        b = _dot(h, wb_ref[...])
        if per_row_state:
            a1 = p1_ref[rows, :]
            a2 = p0_ref[rows, :]
            a_out[rows, :] = a
        else:
            row = lax.broadcasted_iota(jnp.int32, a.shape, 0)
            a1 = jnp.where(row == 0, prev[7:8, :], pltpu.roll(a, 1, 0))
            a2 = jnp.where(row == 0, prev[6:7, :], jnp.where(row == 1, prev[7:8, :], pltpu.roll(a, 2, 0)))
            prev = a[rc - SUBLANES:, :]
        a_conv = bc_ref[...] + a2 * w[0:1, :] + a1 * w[1:2, :] + a * w[2:3, :]
        g_out[rows, :] = (a_conv * jax.nn.sigmoid(a_conv) * b).astype(g_out.dtype)
    if not per_row_state:
        carry_scr[j] = prev
        tail_out[...] = prev


def _ffn_up(x, g, wa, wb, wconv, bconv, *, tm, tn, seq_len, prev=None):
    m, d = x.shape
    f = wa.shape[1]
    nj = f // tn
    per_row_state = prev is not None
    tiles_per_seq = max(seq_len // tm, 1)
    kern = functools.partial(_ffn_up_kernel, tiles_per_seq=tiles_per_seq, per_row_state=per_row_state)
    col = lambda r: pl.BlockSpec((r, tn), lambda i, j: (0, j))
    in_specs = [pl.BlockSpec((tm, d), lambda i, j: (i, 0), pipeline_mode=pl.Buffered(1)),
                pl.BlockSpec((1, d), lambda i, j: (0, 0)), col(d), col(d), col(SUBLANES), col(1)]
    args = [x, g, wa, wb, wconv, bconv]
    tile = pl.BlockSpec((tm, tn), lambda i, j: (i, j))
    scratch = [pltpu.VMEM((tm, d), BF16)]
    if per_row_state:
        in_specs += [tile, tile]
        args += [prev[0], prev[1]]
        out_specs = [tile, tile]
        out_shape = [jax.ShapeDtypeStruct((m, f), BF16), jax.ShapeDtypeStruct((m, f), F32)]
    else:
        out_specs = [tile, pl.BlockSpec((None, SUBLANES, tn), lambda i, j: (i, 0, j))]
        out_shape = [jax.ShapeDtypeStruct((m, f), BF16), jax.ShapeDtypeStruct((m // tm, SUBLANES, f), F32)]
        scratch += [pltpu.VMEM((nj, SUBLANES, tn), F32)]
    return pl.pallas_call(
        kern, grid=(m // tm, nj), in_specs=in_specs, out_specs=out_specs, out_shape=out_shape,
        scratch_shapes=scratch, compiler_params=_params(("arbitrary", "arbitrary")), name="ffn_up",
    )(*args)


def _ffn_down_kernel(*refs, final):
    if final:
        g_ref, w_ref, x_ref, gf_ref, o_ref, y_ref = refs
    else:
        g_ref, w_ref, x_ref, o_ref = refs
    xo = x_ref[...] + _dot(g_ref[...], w_ref[...])
    o_ref[...] = xo
    if final:
        y_ref[...] = _rms(xo, gf_ref[...])


def _ffn_down(gact, w, x, g_final, *, tm):
    m, d = x.shape
    f = gact.shape[1]
    final = g_final is not None
    rows = lambda wd: pl.BlockSpec((tm, wd), lambda i: (i, 0))
    in_specs = [rows(f), _resident(w.shape), rows(d)]
    args = [gact, w, x]
    out_specs = [rows(d)]
    out_shape = [jax.ShapeDtypeStruct((m, d), F32)]
    if final:
        in_specs += [pl.BlockSpec((1, d), lambda i: (0, 0))]
        args += [g_final]
        out_specs += [rows(d)]
        out_shape += [jax.ShapeDtypeStruct((m, d), F32)]
    res = pl.pallas_call(
        functools.partial(_ffn_down_kernel, final=final),
        grid=(m // tm,), in_specs=in_specs, out_specs=out_specs, out_shape=out_shape,
        compiler_params=_params(("arbitrary",)), name="ffn_down",
    )(*args)
    return (res[0], res[1]) if final else (res[0], None)


def _head_mask(rows, width):
    r = lax.broadcasted_iota(jnp.int32, (rows, width), 0)
    c = lax.broadcasted_iota(jnp.int32, (rows, width), 1)
    return (c // HEAD_DIM) == r


def _page_heads(page_ref, kv, heads):
    return jnp.concatenate(
        [page_ref[pl.ds(2 * h + kv, LANES, stride=2 * heads), :] for h in range(heads)], axis=1)


def _fox_dec_kernel(pt_ref, q_ref, kvn_ref, lfn_ref, lf_ref, *refs, pages_per_step, heads, n_pages):
    page_refs = refs[:pages_per_step]
    o_ref, m_scr, l_scr, acc_scr, run_scr = refs[pages_per_step:]
    b = pl.program_id(0)
    c = pl.program_id(1)
    w = heads * HEAD_DIM
    hm = _head_mask(SUBLANES, w)
    qbd = jnp.where(hm, jnp.broadcast_to(q_ref[...], (SUBLANES, w)), 0.0)

    @pl.when(c == 0)
    def _():
        kvn = kvn_ref[...]
        m_scr[...] = jnp.sum(qbd * kvn[:, :w], axis=1, keepdims=True)
        l_scr[...] = jnp.ones_like(l_scr)
        acc_scr[...] = jnp.broadcast_to(kvn[:, w:], (SUBLANES, w))
        run_scr[...] = lfn_ref[...]

    r = lax.broadcasted_iota(jnp.int32, (LANES, LANES), 0)
    cc = lax.broadcasted_iota(jnp.int32, (LANES, LANES), 1)
    upper = (r > cc).astype(BF16)
    qb = qbd.astype(BF16)
    run = run_scr[...]
    pad = jnp.zeros((SUBLANES - heads, LANES), F32)
    ss = []
    for u in range(pages_per_step):
        pid = pt_ref[b * n_pages + (n_pages - 1 - (c * pages_per_step + u))]
        lf = jnp.concatenate([lf_ref[h, pl.ds(pid, 1), :] for h in range(heads)] + [pad], axis=0)
        s = _dot_nt(qb, _page_heads(page_refs[u], 0, heads).astype(BF16))
        suffix = sum(_dot(part, upper) for part in _split3(lf))
        ss.append(s + run + suffix)
        run = run + jnp.sum(lf, axis=1, keepdims=True)
    s = jnp.concatenate(ss, axis=1)
    m_old = m_scr[...]
    m_new = jnp.maximum(m_old, jnp.max(s, axis=1, keepdims=True))
    alpha = jnp.exp(m_old - m_new)
    p = jnp.exp(s - m_new).astype(BF16)
    l_scr[...] = alpha * l_scr[...] + jnp.sum(p.astype(F32), axis=1, keepdims=True)
    pv = sum(_dot(p[:, u * LANES:(u + 1) * LANES], _page_heads(page_refs[u], 1, heads).astype(BF16))
             for u in range(pages_per_step))
    acc_scr[...] = alpha * acc_scr[...] + pv
    m_scr[...] = m_new
    run_scr[...] = run

    @pl.when(c == pl.num_programs(1) - 1)
    def _():
        o = jnp.where(hm, acc_scr[...] * (1.0 / l_scr[...]), 0.0)
        o_ref[...] = jnp.sum(o, axis=0, keepdims=True).astype(o_ref.dtype)


def _fox_decode(q, kv_new, lf_new, cache_pages, cache_lf, page_table, layer, *, pages_per_step):
    db, n_pages = page_table.shape
    heads = FOX_HEADS
    w = heads * HEAD_DIM
    n_phys = cache_lf.shape[2]
    steps = n_pages // pages_per_step

    def page_map(u):
        def f(b, c, pt):
            return (layer * n_phys + pt[b * n_pages + (n_pages - 1 - (c * pages_per_step + u))], 0, 0)
        return f

    in_specs = [pl.BlockSpec((None, 1, w), lambda b, c, pt: (b, 0, 0)),
                pl.BlockSpec((None, 1, 2 * w), lambda b, c, pt: (b, 0, 0)),
                pl.BlockSpec((None, SUBLANES, 1), lambda b, c, pt: (b, 0, 0)),
                pl.BlockSpec((None, heads, n_phys, LANES), lambda b, c, pt: (layer, 0, 0, 0))]
    in_specs += [pl.BlockSpec((None, 2 * w, LANES), page_map(u)) for u in range(pages_per_step)]
    kern = functools.partial(_fox_dec_kernel, pages_per_step=pages_per_step, heads=heads, n_pages=n_pages)
    return pl.pallas_call(
        kern,
        grid_spec=pltpu.PrefetchScalarGridSpec(
            num_scalar_prefetch=1, grid=(db, steps), in_specs=in_specs,
            out_specs=pl.BlockSpec((None, 1, w), lambda b, c, pt: (b, 0, 0)),
            scratch_shapes=[pltpu.VMEM((SUBLANES, 1), F32), pltpu.VMEM((SUBLANES, 1), F32),
                            pltpu.VMEM((SUBLANES, w), F32), pltpu.VMEM((SUBLANES, 1), F32)]),
        out_shape=jax.ShapeDtypeStruct((db, 1, w), BF16),
        compiler_params=_params(("arbitrary", "arbitrary")),
        name="fox_decode",
    )(page_table.reshape(-1), q, kv_new, lf_new, cache_lf, *([cache_pages] * pages_per_step))


def _moba_sel_kernel(pt_ref, q_ref, *refs, pages_per_step, heads, n_blk):
    del pt_ref
    k_refs = refs[:pages_per_step]
    sel_ref, km_scr = refs[pages_per_step:]
    c = pl.program_id(1)
    w = heads * HEAD_DIM
    pages_per_blk = MOBA_BLOCK // LANES
    blks = pages_per_step // pages_per_blk

    @pl.when(c == 0)
    def _():
        km_scr[...] = jnp.zeros_like(km_scr)

    rows = []
    for bb in range(blks):
        per_head = []
        for h in range(heads):
            tot = jnp.zeros((1, HEAD_DIM), F32)
            for pp in range(pages_per_blk):
                keys = k_refs[bb * pages_per_blk + pp][pl.ds(2 * h, LANES, stride=2 * heads), :]
                tot = tot + jnp.sum(keys, axis=0, keepdims=True)
            per_head.append(tot)
        rows.append(jnp.concatenate(per_head, axis=1) * (1.0 / MOBA_BLOCK))
    km_scr[pl.ds(pl.multiple_of(c * blks, blks), blks), :] = jnp.concatenate(rows, axis=0)

    @pl.when(c == pl.num_programs(1) - 1)
    def _():
        hm = _head_mask(SUBLANES, w)
        qbd = jnp.where(hm, jnp.broadcast_to(q_ref[...], (SUBLANES, w)), 0.0)
        km = km_scr[...]
        bs = jnp.zeros((SUBLANES, LANES), F32)
        for qp in _split3(qbd):
            for kp in _split3(km):
                bs = bs + _dot_nt(qp, kp)
        lane = lax.broadcasted_iota(jnp.int32, (SUBLANES, LANES), 1)
        bs = jnp.where(lane < n_blk, bs, -jnp.inf)
        out = jnp.zeros((SUBLANES, LANES), jnp.int32)
        for t in range(MOBA_TOPK):
            mx = jnp.max(bs, axis=1, keepdims=True)
            idx = jnp.min(jnp.where(bs == mx, lane, LANES), axis=1, keepdims=True)
            out = jnp.where(lane == t, idx, out)
            bs = jnp.where(lane == idx, -jnp.inf, bs)
        sel_ref[...] = out


def _moba_select(q, cache_pages, page_table, layer_page0, *, pages_per_step):
    db, n_pages = page_table.shape
    heads = MOBA_HEADS
    w = heads * HEAD_DIM
    n_blk = n_pages * LANES // MOBA_BLOCK
    assert MOBA_TOPK <= n_blk <= LANES and pages_per_step % (SUBLANES * MOBA_BLOCK // LANES) == 0
    steps = n_pages // pages_per_step

    def page_map(u):
        return lambda b, c, pt: (layer_page0 + pt[b * n_pages + c * pages_per_step + u], 0, 0)

    in_specs = [pl.BlockSpec((None, 1, w), lambda b, c, pt: (b, 0, 0))]
    in_specs += [pl.BlockSpec((None, 2 * w, LANES), page_map(u)) for u in range(pages_per_step)]
    kern = functools.partial(_moba_sel_kernel, pages_per_step=pages_per_step, heads=heads, n_blk=n_blk)
    return pl.pallas_call(
        kern,
        grid_spec=pltpu.PrefetchScalarGridSpec(
            num_scalar_prefetch=1, grid=(db, steps), in_specs=in_specs,
            out_specs=pl.BlockSpec((None, SUBLANES, LANES), lambda b, c, pt: (b, 0, 0)),
            scratch_shapes=[pltpu.VMEM((LANES, w), F32)]),
        out_shape=jax.ShapeDtypeStruct((db, SUBLANES, LANES), jnp.int32),
        compiler_params=_params(("arbitrary", "arbitrary")),
        name="moba_select",
    )(page_table.reshape(-1), q, *([cache_pages] * pages_per_step))


def _moba_dec_kernel(sel_ref, pt_ref, q_ref, kn_ref, vn_ref, *refs, n_pages_sel):
    del sel_ref, pt_ref
    page_refs = refs[:n_pages_sel]
    o_ref = refs[n_pages_sel]
    q = q_ref[...]
    qb = jnp.broadcast_to(q, (SUBLANES, HEAD_DIM)).astype(BF16)
    s_self = jnp.sum(q * kn_ref[...], axis=1, keepdims=True)
    s = jnp.concatenate([_dot_nt(qb, page_refs[u][:, 0, :].astype(BF16)) for u in range(n_pages_sel)], axis=1)
    m = jnp.maximum(s_self, jnp.max(s, axis=1, keepdims=True))
    p_self = jnp.exp(s_self - m)
    p = jnp.exp(s - m).astype(BF16)
    l = p_self + jnp.sum(p.astype(F32), axis=1, keepdims=True)
    acc = p_self * vn_ref[...]
    for u in range(n_pages_sel):
        acc = acc + _dot(p[:, u * LANES:(u + 1) * LANES], page_refs[u][:, 1, :].astype(BF16))
    o_ref[...] = (acc * (1.0 / l))[0:1, :].astype(o_ref.dtype)


def _moba_decode(q, kv_new, sel, cache_rows, page_table, layer_page0):
    db, n_pages = page_table.shape
    heads = MOBA_HEADS
    pages_per_blk = MOBA_BLOCK // LANES
    n_pages_sel = MOBA_TOPK * pages_per_blk

    def page_map(u):
        jsel, pg = divmod(u, pages_per_blk)

        def f(b, h, sel_s, pt):
            blk = sel_s[(b * heads + h) * MOBA_TOPK + jsel]
            return (layer_page0 + pt[b * n_pages + blk * pages_per_blk + pg], h, 0, 0)
        return f

    one = lambda col0: pl.BlockSpec((None, 1, HEAD_DIM), lambda b, h, sel_s, pt: (b, 0, col0 + h))
    in_specs = [one(0), one(0), one(heads)]
    in_specs += [pl.BlockSpec((LANES, None, 2, HEAD_DIM), page_map(u)) for u in range(n_pages_sel)]
    kern = functools.partial(_moba_dec_kernel, n_pages_sel=n_pages_sel)
    return pl.pallas_call(
        kern,
        grid_spec=pltpu.PrefetchScalarGridSpec(
            num_scalar_prefetch=2, grid=(db, heads), in_specs=in_specs,
            out_specs=pl.BlockSpec((None, 1, HEAD_DIM), lambda b, h, sel_s, pt: (b, 0, h))),
        out_shape=jax.ShapeDtypeStruct((db, 1, heads * HEAD_DIM), BF16),
        compiler_params=_params(("arbitrary", "arbitrary")),
        name="moba_decode",
    )(sel.reshape(-1), page_table.reshape(-1), q, kv_new, kv_new, *([cache_rows] * n_pages_sel))


def _mem_dec_kernel(q_ref, kv_ref, o_ref, *, heads, n_mem):
    outs = []
    for h in range(heads):
        qb = jnp.broadcast_to(q_ref[:, _head_cols(h)], (SUBLANES, HEAD_DIM)).astype(BF16)
        k = kv_ref[pl.ds(h, n_mem, stride=2 * heads), :].astype(BF16)
        v = kv_ref[pl.ds(heads + h, n_mem, stride=2 * heads), :].astype(BF16)
        s = _dot_nt(qb, k)
        p = jnp.exp(s - jnp.max(s, axis=1, keepdims=True))
        l = jnp.sum(p, axis=1, keepdims=True)
        outs.append((_dot(p.astype(BF16), v) * (1.0 / l))[0:1, :])
    o_ref[...] = jnp.concatenate(outs, axis=1).astype(o_ref.dtype)


def _mem_decode(q, mem_rows, layer, *, q_col_block):
    db = q.shape[0]
    heads = MEM_HEADS
    n_mem = mem_rows.shape[1] // (2 * heads)
    return pl.pallas_call(
        functools.partial(_mem_dec_kernel, heads=heads, n_mem=n_mem),
        grid=(db,),
        in_specs=[pl.BlockSpec((None, 1, HEAD_TILE), lambda b: (b, 0, q_col_block)),
                  pl.BlockSpec((None, mem_rows.shape[1], HEAD_DIM), lambda b: (layer * db + b, 0, 0))],
        out_specs=pl.BlockSpec((None, 1, heads * HEAD_DIM), lambda b: (b, 0, 0)),
        out_shape=jax.ShapeDtypeStruct((db, 1, heads * HEAD_DIM), BF16),
        compiler_params=_params(("arbitrary",)),
        name="mem_decode",
    )(q, mem_rows)


def _rope_tables(pos):
    half = HEAD_DIM // 2
    inv_freq = ROPE_THETA ** (-jnp.arange(half, dtype=F32) / half)
    ang = pos.astype(F32)[:, None] * inv_freq[None, :]
    cos, sin = jnp.cos(ang), jnp.sin(ang)
    return jnp.concatenate([cos, cos], axis=-1), jnp.concatenate([-sin, sin], axis=-1)


def _pad_cols(a, n):
    return jnp.pad(a, ((0, 0), (0, n - a.shape[1])))


def _prep_w_in(w_in, d):
    mw, fw, cw = MOBA_HEADS * HEAD_DIM, FOX_HEADS * HEAD_DIM, MEM_HEADS * HEAD_DIM
    o_fl = 3 * mw + 3 * fw
    o_qc = o_fl + FOX_HEADS
    o_gl = o_qc + cw
    layers, k, n = w_in.shape
    assert n == o_gl + 3 * d and cw % W_IN_COLS == 0 and (3 * d) % W_IN_COLS == 0 and o_fl % W_IN_COLS == 0
    w_t = w_in.reshape(layers, k // LANES, LANES, n).transpose(3, 1, 0, 2)
    starts = list(range(0, o_fl, W_IN_COLS)) + list(range(o_qc, o_gl, W_IN_COLS))
    starts += [o_gl] * ((HEAD_TILE - cw) // W_IN_COLS)
    starts += list(range(o_gl, n, W_IN_COLS))
    w_main = _w_in_layout(w_t, starts, W_IN_COLS)
    w_fl = _w_in_layout(w_t, [o_fl], LANES)
    return w_main, w_fl


def _prep_layer(l, g_attn, b_f, w_br_moba, w_br_fox, w_br_mem, w_out, g_mem, w_mem_kv,
                g_ffn, w_up, w_conv, b_conv, w_down, ff_tile):
    f = w_down.shape[1]
    fp = -(-f // ff_tile) * ff_tile
    wu = w_up[l]
    return dict(
        g_attn=g_attn[l][None], b_f=_pad_cols(b_f[l][None], LANES),
        wm=w_br_moba[l].astype(BF16), wf=w_br_fox[l].astype(BF16), wc=w_br_mem[l].astype(BF16),
        wo=w_out[l].astype(BF16), g_mem=g_mem[l][None], w_mem=w_mem_kv[l].astype(BF16),
        g_ffn=g_ffn[l][None], wa=_pad_cols(wu[:, :f], fp).astype(BF16), wb=_pad_cols(wu[:, f:], fp).astype(BF16),
        w_conv=jnp.pad(w_conv[l], ((0, SUBLANES - CONV_W), (0, fp - f))), b_conv=_pad_cols(b_conv[l][None], fp),
        w_down=jnp.pad(w_down[l], ((0, fp - f), (0, 0))).astype(BF16), f=f)


def kernel(x_prompt, x_sample, cache_moba_kv, cache_fox_kv, cache_fox_logf, cache_mem_kv, state_ffn_conv,
           page_table, mem_prompt, g_attn, w_in, b_f, w_br_moba, w_br_fox, w_br_mem, w_out, g_mem,
           w_mem_kv, g_ffn, w_up, w_conv, b_conv, w_down, g_final):
    bsz, seq, d = x_prompt.shape
    db = x_sample.shape[0]
    depth, n_phys, page = cache_moba_kv.shape[:3]
    n_pages = page_table.shape[1]
    past_len = n_pages * page
    n_mem = mem_prompt.shape[1]
    mw, fw, cw = MOBA_HEADS * HEAD_DIM, FOX_HEADS * HEAD_DIM, MEM_HEADS * HEAD_DIM
    assert page == LANES and db == SUBLANES and x_sample.shape[1] == 1
    ff_tile = 512
    tm = max(t for t in (1024, 512, 256) if seq % t == 0)
    tq = MOBA_BLOCK

    cos_p, sin_p = _rope_tables(jnp.arange(seq))
    cos_s, sin_s = _rope_tables(jnp.full((db,), past_len))
    gf = g_final[None]

    moba_t = jnp.transpose(cache_moba_kv, (0, 1, 2, 4, 3, 5))
    moba_pages = moba_t.reshape(depth * n_phys, page * 2 * MOBA_HEADS, HEAD_DIM)
    moba_rows = moba_t.reshape(depth * n_phys * page, MOBA_HEADS, 2, HEAD_DIM)
    fox_pages = jnp.transpose(cache_fox_kv, (0, 1, 2, 4, 3, 5)).reshape(depth * n_phys, page * 2 * FOX_HEADS, HEAD_DIM)
    lf_cache = jnp.transpose(cache_fox_logf, (0, 3, 1, 2))
    mem_rows = cache_mem_kv.reshape(depth * db, n_mem * 2 * MEM_HEADS, HEAD_DIM)

    w_main_all, w_fl_all = _prep_w_in(w_in, d)
    xp = x_prompt.reshape(bsz * seq, d)
    xs = x_sample.reshape(db, d)
    outs = {k: [] for k in ("p_moba", "p_fox", "p_logf", "p_mem", "p_conv", "s_moba", "s_fox", "s_logf", "s_conv")}
    yp = ys = None
    for l in range(depth):
        w = _prep_layer(l, g_attn, b_f, w_br_moba, w_br_fox, w_br_mem, w_out, g_mem, w_mem_kv,
                        g_ffn, w_up, w_conv, b_conv, w_down, ff_tile)
        w["w_main"], w["w_fl"] = w_main_all[l], w_fl_all[l]
        f = w["f"]
        g_last = gf if l == depth - 1 else None

        mkv = _norm_mm(mem_prompt.reshape(bsz * n_mem, d), w["g_mem"], w["w_mem"], tm=min(512, bsz * n_mem))
        q_all, moba_kv, fox_kv, gates, logf, c = _in_proj(
            xp, w["g_attn"], w["w_main"], w["w_fl"], w["b_f"], cos_p, sin_p, tm=tm, seq_len=seq, q_dtype=BF16)
        q3 = q_all.reshape(bsz, seq, 3 * HEAD_TILE)
        c3 = c.reshape(bsz, seq, LANES)
        o_m = _causal_attention(q3, moba_kv.reshape(bsz, seq, 2 * mw), mode="moba", heads=MOBA_HEADS, q_col0=0)
        o_f = _causal_attention(q3, fox_kv.reshape(bsz, seq, 2 * fw), mode="fox", heads=FOX_HEADS,
                                q_col0=MOBA_HEADS, c=c3)
        o_c = _mem_attention(q3, mkv.reshape(bsz, n_mem, 2 * cw), heads=MEM_HEADS, q_col_block=2, tq=tq)
        xp = _merge_out(o_m.reshape(bsz * seq, mw), o_f.reshape(bsz * seq, fw), o_c.reshape(bsz * seq, cw),
                        gates, xp, w["wm"], w["wf"], w["wc"], w["wo"], tm=256)
        gact, tails = _ffn_up(xp, w["g_ffn"], w["wa"], w["wb"], w["w_conv"], w["b_conv"],
                              tm=tm, tn=ff_tile, seq_len=seq)
        xp, yp = _ffn_down(gact, w["w_down"], xp, g_last, tm=256)
        outs["p_moba"].append(moba_kv.reshape(bsz, seq, 2, MOBA_HEADS, HEAD_DIM))
        outs["p_fox"].append(fox_kv.reshape(bsz, seq, 2, FOX_HEADS, HEAD_DIM))
        outs["p_logf"].append(logf[:, :FOX_HEADS].reshape(bsz, seq, FOX_HEADS))
        outs["p_mem"].append(mkv.reshape(bsz, n_mem, 2, MEM_HEADS, HEAD_DIM))
        tiles_per_seq = seq // tm
        outs["p_conv"].append(tails[tiles_per_seq - 1::tiles_per_seq, SUBLANES - (CONV_W - 1):, :f])

        q_s, moba_s, fox_s, gates_s, logf_s, _ = _in_proj(
            xs, w["g_attn"], w["w_main"], w["w_fl"], w["b_f"], cos_s, sin_s, tm=db, seq_len=1, q_dtype=F32)
        q_s3 = q_s.reshape(db, 1, 3 * HEAD_TILE)
        sel = _moba_select(q_s3, moba_pages, page_table, l * n_phys, pages_per_step=16)
        o_ms = _moba_decode(q_s3, moba_s.reshape(db, 1, 2 * mw), sel[:, :MOBA_HEADS, :MOBA_TOPK],
                            moba_rows, page_table, l * n_phys)
        o_fs = _fox_decode(q_s3[:, :, mw:mw + fw], fox_s.reshape(db, 1, 2 * fw),
                           logf_s[:, :SUBLANES].reshape(db, SUBLANES, 1), fox_pages, lf_cache, page_table,
                           l, pages_per_step=8)
        o_cs = _mem_decode(q_s3, mem_rows, l, q_col_block=2)
        xs = _merge_out(o_ms.reshape(db, mw), o_fs.reshape(db, fw), o_cs.reshape(db, cw), gates_s, xs,
                        w["wm"], w["wf"], w["wc"], w["wo"], tm=db)
        fp = w["wa"].shape[1]
        prev = jnp.pad(state_ffn_conv[l], ((0, 0), (0, 0), (0, fp - f)))
        gact_s, a_s = _ffn_up(xs, w["g_ffn"], w["wa"], w["wb"], w["w_conv"], w["b_conv"],
                              tm=db, tn=ff_tile, seq_len=1, prev=(prev[:, 0], prev[:, 1]))
        xs, ys = _ffn_down(gact_s, w["w_down"], xs, g_last, tm=db)
        outs["s_moba"].append(moba_s.reshape(db, 1, 2, MOBA_HEADS, HEAD_DIM))
        outs["s_fox"].append(fox_s.reshape(db, 1, 2, FOX_HEADS, HEAD_DIM))
        outs["s_logf"].append(logf_s[:, :FOX_HEADS].reshape(db, 1, FOX_HEADS))
        outs["s_conv"].append(jnp.stack([state_ffn_conv[l][:, 1], a_s[:, :f]], axis=1))

    st = lambda k: jnp.stack(outs[k])
    return (yp.reshape(bsz, seq, d), ys.reshape(db, 1, d),
            st("p_moba"), st("p_fox"), st("p_logf"), st("p_mem"), st("p_conv"),
            st("s_moba"), st("s_fox"), st("s_logf"), st("s_conv"))
```

```python
import functools

import jax
import jax.numpy as jnp
from jax import lax
from jax.experimental import pallas as pl
from jax.experimental.pallas import tpu as pltpu

F32 = jnp.float32
BF16 = jnp.bfloat16

HEAD_DIM = 128
MOBA_HEADS = 6
FOX_HEADS = 6
MEM_HEADS = 4
MOBA_BLOCK = 256
MOBA_TOPK = 3
CONV_W = 3
ROPE_THETA = 10000.0
NORM_EPS = 1e-6
NEG = -1e30

LANES = 128
SUBLANES = 8
HEAD_TILE = MOBA_HEADS * HEAD_DIM
VMEM_LIMIT = 56 * 1024 * 1024
ROW_CHUNK = 256
SCALE = HEAD_DIM ** -0.5

_NT = (((1,), (1,)), ((), ()))


def _params(sem):
    return pltpu.CompilerParams(dimension_semantics=sem, vmem_limit_bytes=VMEM_LIMIT)


def _rms(x, g):
    return x * lax.rsqrt(jnp.mean(x * x, axis=-1, keepdims=True) + NORM_EPS) * g


def _log_sigmoid(x):
    return jnp.minimum(x, 0.0) - jnp.log1p(jnp.exp(-jnp.abs(x)))


def _split3(v):
    hi = v.astype(BF16)
    r1 = v - hi.astype(F32)
    mid = r1.astype(BF16)
    lo = (r1 - mid.astype(F32)).astype(BF16)
    return hi, mid, lo


def _dot(a, b):
    return jnp.dot(a, b, preferred_element_type=F32)


def _dot_nt(a, b):
    return lax.dot_general(a, b, _NT, preferred_element_type=F32)


def _cumsum_rows(v, carry, blk=256):
    tm = v.shape[0]
    blk = min(blk, tm)
    r = lax.broadcasted_iota(jnp.int32, (blk, blk), 0)
    c = lax.broadcasted_iota(jnp.int32, (blk, blk), 1)
    tri = (c <= r).astype(BF16)
    outs = []
    for s in range(0, tm, blk):
        hi, mid, lo = _split3(v[s:s + blk])
        cs = _dot(tri, hi) + _dot(tri, mid) + _dot(tri, lo) + carry
        outs.append(cs)
        carry = cs[blk - 1:blk, :]
    return jnp.concatenate(outs, axis=0), carry


def _rope(acc, cos, sin_signed):
    outs = []
    for hh in range(acc.shape[1] // HEAD_DIM):
        xh = acc[:, hh * HEAD_DIM:(hh + 1) * HEAD_DIM]
        outs.append(xh * cos + pltpu.roll(xh, HEAD_DIM // 2, 1) * sin_signed)
    return jnp.concatenate(outs, axis=1)


def _head_cols(h):
    return slice(h * HEAD_DIM, (h + 1) * HEAD_DIM)


W_IN_COLS = 256


def _w_in_kernel(starts_ref, x_ref, o_ref):
    del starts_ref
    cols, kt, layers, _ = x_ref.shape
    for l in range(layers):
        for t in range(kt):
            o_ref[l, t * LANES:(t + 1) * LANES, :] = x_ref[:, t, l, :].T.astype(o_ref.dtype)


def _w_in_layout(w_t, starts, cols):
    _, kt, layers, _ = w_t.shape
    starts = jnp.asarray(starts, jnp.int32)
    return pl.pallas_call(
        _w_in_kernel,
        grid_spec=pltpu.PrefetchScalarGridSpec(
            num_scalar_prefetch=1, grid=(starts.shape[0],),
            in_specs=[pl.BlockSpec((pl.Element(cols), pl.Element(kt), pl.Element(layers), pl.Element(LANES)),
                                   lambda j, st: (st[j], 0, 0, 0))],
            out_specs=pl.BlockSpec((layers, kt * LANES, cols), lambda j, st: (0, 0, j))),
        out_shape=jax.ShapeDtypeStruct((layers, kt * LANES, starts.shape[0] * cols), BF16),
        compiler_params=_params(("arbitrary",)),
        name="w_in_layout",
    )(starts, w_t)


def _cast_kernel(x_ref, o_ref):
    o_ref[...] = x_ref[...].astype(o_ref.dtype)


def _cast_bf16(w, rows):
    layers, r, c = w.shape
    rows = min(rows, r)
    assert r % rows == 0
    spec = pl.BlockSpec((None, rows, c), lambda l, i: (l, i, 0))
    return pl.pallas_call(
        _cast_kernel, grid=(layers, r // rows), in_specs=[spec], out_specs=spec,
        out_shape=jax.ShapeDtypeStruct(w.shape, BF16), compiler_params=_params(("arbitrary", "arbitrary")),
        name="cast_bf16")(w)


def _cast_up_kernel(x_ref, a_ref, b_ref, *, f):
    for o_ref, c0 in ((a_ref, 0), (b_ref, f)):
        o_ref[:, :f] = x_ref[:, c0:c0 + f].astype(o_ref.dtype)
        o_ref[:, f:] = jnp.zeros((o_ref.shape[0], o_ref.shape[1] - f), o_ref.dtype)


def _cast_w_up(w_up, fp, rows):
    layers, d, f2 = w_up.shape
    out = pl.BlockSpec((None, rows, fp), lambda l, i: (l, i, 0))
    return pl.pallas_call(
        functools.partial(_cast_up_kernel, f=f2 // 2), grid=(layers, d // rows),
        in_specs=[pl.BlockSpec((None, rows, f2), lambda l, i: (l, i, 0))], out_specs=[out, out],
        out_shape=[jax.ShapeDtypeStruct((layers, d, fp), BF16)] * 2,
        compiler_params=_params(("arbitrary", "arbitrary")), name="cast_w_up")(w_up)


def _cast_down_kernel(x_ref, o_ref, *, n_real):
    x = x_ref[...].astype(o_ref.dtype)
    o_ref[...] = jnp.where(pl.program_id(1) < n_real, x, jnp.zeros_like(x))


def _cast_w_down(w_down, fp, rows):
    layers, f, d = w_down.shape
    assert f % rows == 0 and fp % rows == 0
    n_real = f // rows
    return pl.pallas_call(
        functools.partial(_cast_down_kernel, n_real=n_real), grid=(layers, fp // rows),
        in_specs=[pl.BlockSpec((None, rows, d), lambda l, i: (l, jnp.minimum(i, n_real - 1), 0))],
        out_specs=pl.BlockSpec((None, rows, d), lambda l, i: (l, i, 0)),
        out_shape=jax.ShapeDtypeStruct((layers, fp, d), BF16),
        compiler_params=_params(("arbitrary", "arbitrary")), name="cast_w_down")(w_down)


def _in_proj_kernel(x_ref, g_ref, w_ref, wfl_ref, bf_ref, cos_ref, sin_ref,
                    q_ref, mkv_ref, fkv_ref, gate_ref, logf_ref, c_ref,
                    h_scr, carry_scr, *, tiles_per_seq, cumsum):
    i = pl.program_id(0)
    j = pl.program_id(1)

    @pl.when(j == 0)
    def _():
        h = _rms(x_ref[...], g_ref[...]).astype(BF16)
        h_scr[...] = h
        logf = _log_sigmoid(_dot(h, wfl_ref[...]) + bf_ref[...])
        logf_ref[...] = logf
        if cumsum:
            @pl.when(i % tiles_per_seq == 0)
            def _():
                carry_scr[...] = jnp.zeros_like(carry_scr)
            c, carry = _cumsum_rows(logf, carry_scr[...])
            c_ref[...] = c
            carry_scr[...] = carry
        else:
            c_ref[...] = logf

    tm = x_ref.shape[0]
    rc = min(ROW_CHUNK, tm)

    def tile(out_ref, epilogue):
        for r0 in range(0, tm, rc):
            rows = slice(r0, r0 + rc)
            acc = _dot(h_scr[rows, :], w_ref[...])
            out_ref[rows, :] = epilogue(acc, rows).astype(out_ref.dtype)

    rope = lambda acc, rows: _rope(acc, cos_ref[rows, :], sin_ref[rows, :])

    @pl.when(j == 0)
    def _():
        tile(q_ref, lambda acc, rows: rope(acc, rows) * SCALE)

    @pl.when(j == 1)
    def _():
        tile(mkv_ref, rope)

    @pl.when(j == 2)
    def _():
        tile(mkv_ref, lambda acc, rows: acc)

    @pl.when(jnp.logical_or(j == 3, j == 6))
    def _():
        tile(q_ref, lambda acc, rows: acc * SCALE)

    @pl.when(jnp.logical_or(j == 4, j == 5))
    def _():
        tile(fkv_ref, lambda acc, rows: acc)

    @pl.when(j >= 7)
    def _():
        tile(gate_ref, lambda acc, rows: jax.nn.sigmoid(acc))


def _in_proj(x, g, w_main, w_fl, b_f, cos, sin, *, layer, tm, seq_len, q_dtype):
    m, d = x.shape
    n_tiles = w_main.shape[2] // HEAD_TILE
    n_gate = n_tiles - 7
    table_tiles = cos.shape[0] // tm
    cumsum = seq_len > 1
    tiles_per_seq = max(seq_len // tm, 1)
    kern = functools.partial(_in_proj_kernel, tiles_per_seq=tiles_per_seq, cumsum=cumsum)
    row = lambda i, j: (i, 0)
    return pl.pallas_call(
        kern,
        grid=(m // tm, n_tiles),
        in_specs=[
            pl.BlockSpec((tm, d), row, pipeline_mode=pl.Buffered(1)),
            pl.BlockSpec((1, d), lambda i, j: (0, 0)),
            pl.BlockSpec((None, d, HEAD_TILE), lambda i, j: (layer, 0, j)),
            pl.BlockSpec((None, d, LANES), lambda i, j: (layer, 0, 0)),
            pl.BlockSpec((1, LANES), lambda i, j: (0, 0)),
            pl.BlockSpec((tm, LANES), lambda i, j: (i % table_tiles, 0)),
            pl.BlockSpec((tm, LANES), lambda i, j: (i % table_tiles, 0)),
        ],
        out_specs=[
            pl.BlockSpec((tm, HEAD_TILE), lambda i, j: (i, jnp.minimum(j // 3, 2))),
            pl.BlockSpec((tm, HEAD_TILE), lambda i, j: (i, jnp.clip(j - 1, 0, 1))),
            pl.BlockSpec((tm, HEAD_TILE), lambda i, j: (i, jnp.clip(j - 4, 0, 1))),
            pl.BlockSpec((tm, HEAD_TILE), lambda i, j: (i, jnp.clip(j - 7, 0, n_gate - 1))),
            pl.BlockSpec((tm, LANES), row),
            pl.BlockSpec((tm, LANES), row),
        ],
        out_shape=[
            jax.ShapeDtypeStruct((m, 3 * HEAD_TILE), q_dtype),
            jax.ShapeDtypeStruct((m, 2 * HEAD_TILE), F32),
            jax.ShapeDtypeStruct((m, 2 * HEAD_TILE), F32),
            jax.ShapeDtypeStruct((m, n_gate * HEAD_TILE), BF16),
            jax.ShapeDtypeStruct((m, LANES), F32),
            jax.ShapeDtypeStruct((m, LANES), F32),
        ],
        scratch_shapes=[pltpu.VMEM((tm, d), BF16), pltpu.VMEM((1, LANES), F32)],
        compiler_params=_params(("arbitrary", "arbitrary")),
        name="in_proj",
    )(x, g, w_main, w_fl, b_f, cos, sin)


def _norm_mm_kernel(x_ref, g_ref, w_ref, o_ref):
    h = _rms(x_ref[...], g_ref[...]).astype(BF16)
    o_ref[...] = _dot(h, w_ref[...])


def _norm_mm(x, g, w, *, layer, tm):
    m, d = x.shape
    n = w.shape[2]
    return pl.pallas_call(
        _norm_mm_kernel,
        grid=(m // tm,),
        in_specs=[pl.BlockSpec((tm, d), lambda i: (i, 0)),
                  pl.BlockSpec((1, d), lambda i: (0, 0)),
                  pl.BlockSpec((None, d, n), lambda i: (layer, 0, 0))],
        out_specs=pl.BlockSpec((tm, n), lambda i: (i, 0)),
        out_shape=jax.ShapeDtypeStruct((m, n), F32),
        compiler_params=_params(("arbitrary",)),
        name="mem_kv_proj",
    )(x, g, w)


MASK_BIAS = 1e30


def _causal_attn_kernel(*refs, mode, blk, n_blk):
    q_ref, k_ref, v_ref = refs[:3]
    c_ref = refs[3] if mode == "fox" else None
    o_ref, kv_out_ref = refs[-2:]
    h = pl.program_id(1)
    s_len = q_ref.shape[0]
    q = q_ref[...]
    kf = k_ref[...]
    vf = v_ref[...]
    kv_out_ref[:, 0, :] = kf
    kv_out_ref[:, 1, :] = vf
    vb = vf.astype(BF16)
    lane = lax.broadcasted_iota(jnp.int32, (s_len, LANES), 1)

    if mode == "fox":
        c_col = jnp.sum(jnp.where(lane == h, c_ref[...], 0.0), axis=1, keepdims=True)
        hi, mid, lo = [part.astype(F32) for part in _split3(c_col)]
        aug_q = jnp.where(lane == 0, hi, jnp.where(lane == 1, mid, jnp.where(lane == 2, lo,
                          jnp.where(lane < 6, 1.0, 0.0))))
        aug_k = jnp.where(lane < 3, 1.0, jnp.where(lane == 3, -hi, jnp.where(lane == 4, -mid,
                          jnp.where(lane == 5, -lo, 0.0))))
    else:
        nb8 = -(-n_blk // SUBLANES) * SUBLANES
        km = [jnp.sum(kf[n * blk:(n + 1) * blk], axis=0, keepdims=True) * (1.0 / blk) for n in range(n_blk)]
        km = jnp.concatenate(km + [jnp.zeros((1, HEAD_DIM), F32)] * (nb8 - n_blk), axis=0)
        bs = sum(_dot_nt(part, q) for part in _split3(km))
        blk_id = lax.broadcasted_iota(jnp.int32, (nb8, s_len), 0)
        own = lax.broadcasted_iota(jnp.int32, (nb8, s_len), 1) // blk
        cnt = jnp.zeros((nb8, s_len), F32)
        for mth in range(n_blk - 1):
            sm = bs[mth:mth + 1, :]
            beats = jnp.logical_or(sm > bs, jnp.logical_and(sm == bs, mth < blk_id))
            cnt = cnt + jnp.where(jnp.logical_and(beats, mth < own), 1.0, 0.0)
        sel = jnp.where(jnp.logical_and(blk_id < own, cnt < MOBA_TOPK), 1.0, 0.0)
        sel = jnp.concatenate([sel, jnp.zeros((LANES - nb8, s_len), F32)], axis=0).astype(BF16)
        eye = (lax.broadcasted_iota(jnp.int32, (blk, blk), 0)
               == lax.broadcasted_iota(jnp.int32, (blk, blk), 1)).astype(BF16)
        selc = jnp.concatenate([_dot_nt(eye, sel[:, t * blk:(t + 1) * blk]) for t in range(n_blk)], axis=0)
        row_blk = lax.broadcasted_iota(jnp.int32, (s_len, LANES), 0) // blk
        aug_q = jnp.where(lane == row_blk, 0.0, (selc - 1.0) * MASK_BIAS)
        aug_k = jnp.where(lane == row_blk, 1.0, 0.0)

    q_aug = jnp.concatenate([q, aug_q.astype(BF16)], axis=1)
    k_aug = jnp.concatenate([kf.astype(BF16), aug_k.astype(BF16)], axis=1)
    row = lax.broadcasted_iota(jnp.int32, (blk, blk), 0)
    col = lax.broadcasted_iota(jnp.int32, (blk, blk), 1)
    for t in range(n_blk):
        n = (t + 1) * blk
        s = _dot_nt(q_aug[t * blk:(t + 1) * blk], k_aug[:n])
        diag = jnp.where(col <= row, s[:, n - blk:], NEG)
        s = diag if t == 0 else jnp.concatenate([s[:, :n - blk], diag], axis=1)
        p = jnp.exp(s - jnp.max(s, axis=1, keepdims=True))
        l = jnp.sum(p, axis=1, keepdims=True)
        o_ref[t * blk:(t + 1) * blk, :] = (_dot(p.astype(BF16), vb[:n]) * (1.0 / l)).astype(o_ref.dtype)


def _causal_attention(q, kv, *, mode, heads, q_col0, layer, kv_out, c=None):
    b, s, _ = q.shape
    blk = MOBA_BLOCK
    n_blk = s // blk
    assert s % blk == 0 and n_blk <= LANES and kv.shape[1] == s
    seq_head = lambda col0: pl.BlockSpec((None, s, HEAD_DIM), lambda bi, h: (bi, 0, col0 + h))
    in_specs = [seq_head(q_col0), seq_head(0), seq_head(heads)]
    args = [q, kv, kv]
    if mode == "fox":
        in_specs += [pl.BlockSpec((None, s, LANES), lambda bi, h: (bi, 0, 0))]
        args += [c]
    aliases = {len(args): 1}
    in_specs += [pl.BlockSpec(memory_space=pl.ANY)]
    args += [kv_out]
    return pl.pallas_call(
        functools.partial(_causal_attn_kernel, mode=mode, blk=blk, n_blk=n_blk),
        grid=(b, heads),
        in_specs=in_specs,
        out_specs=[seq_head(0),
                   pl.BlockSpec((None, None, s, None, 2, HEAD_DIM), lambda bi, h: (layer, bi, 0, h, 0, 0))],
        out_shape=[jax.ShapeDtypeStruct((b, s, heads * HEAD_DIM), BF16),
                   jax.ShapeDtypeStruct(kv_out.shape, kv_out.dtype)],
        input_output_aliases=aliases,
        compiler_params=_params(("arbitrary", "arbitrary")),
        name="attn_" + mode,
    )(*args)


def _mem_attn_kernel(q_ref, kv_ref, o_ref, kb, vb, *, heads):
    w = heads * HEAD_DIM

    @pl.when(pl.program_id(1) == 0)
    def _():
        kb[...] = kv_ref[:, :w].astype(BF16)
        vb[...] = kv_ref[:, w:].astype(BF16)

    for h in range(heads):
        s = _dot_nt(q_ref[:, _head_cols(h)], kb[:, _head_cols(h)])
        p = jnp.exp(s - jnp.max(s, axis=1, keepdims=True))
        l = jnp.sum(p, axis=1, keepdims=True)
        o_ref[:, _head_cols(h)] = (_dot(p.astype(BF16), vb[:, _head_cols(h)]) * (1.0 / l)).astype(o_ref.dtype)


def _mem_attention(q, kv, *, heads, q_col_block, tq):
    b, s, _ = q.shape
    n_mem = kv.shape[1]
    w = heads * HEAD_DIM
    return pl.pallas_call(
        functools.partial(_mem_attn_kernel, heads=heads),
        grid=(b, s // tq),
        in_specs=[pl.BlockSpec((None, tq, HEAD_TILE), lambda bi, qi: (bi, qi, q_col_block)),
                  pl.BlockSpec((None, n_mem, 2 * w), lambda bi, qi: (bi, 0, 0))],
        out_specs=pl.BlockSpec((None, tq, w), lambda bi, qi: (bi, qi, 0)),
        out_shape=jax.ShapeDtypeStruct((b, s, w), BF16),
        scratch_shapes=[pltpu.VMEM((n_mem, w), BF16), pltpu.VMEM((n_mem, w), BF16)],
        compiler_params=_params(("arbitrary", "arbitrary")),
        name="attn_mem",
    )(q, kv)


def _merge_kernel(om_ref, of_ref, oc_ref, gate_ref, x_ref, wm_ref, wf_ref, wc_ref, wo_ref, o_ref):
    d = x_ref.shape[1]
    merged = gate_ref[:, 0:d].astype(F32) * _dot(om_ref[...], wm_ref[...])
    merged = merged + gate_ref[:, d:2 * d].astype(F32) * _dot(of_ref[...], wf_ref[...])
    merged = merged + gate_ref[:, 2 * d:3 * d].astype(F32) * _dot(oc_ref[...], wc_ref[...])
    o_ref[...] = x_ref[...] + _dot(merged.astype(BF16), wo_ref[...])


def _resident(w, layer):
    return pl.BlockSpec((None,) + w.shape[1:], lambda i: (layer, 0, 0), pipeline_mode=pl.Buffered(1))


def _merge_out(om, of, oc, gates, x, wm, wf, wc, wo, *, layer, tm):
    m, d = x.shape
    rows = lambda w: pl.BlockSpec((tm, w), lambda i: (i, 0))
    return pl.pallas_call(
        _merge_kernel,
        grid=(m // tm,),
        in_specs=[rows(om.shape[1]), rows(of.shape[1]), rows(oc.shape[1]), rows(gates.shape[1]), rows(d),
                  _resident(wm, layer), _resident(wf, layer), _resident(wc, layer), _resident(wo, layer)],
        out_specs=rows(d),
        out_shape=jax.ShapeDtypeStruct((m, d), F32),
        compiler_params=_params(("arbitrary",)),
        name="merge_out",
    )(om, of, oc, gates, x, wm, wf, wc, wo)


def _ffn_up_kernel(*refs, tiles_per_seq, per_row_state):
    if per_row_state:
        x_ref, g_ref, wa_ref, wb_ref, wc_ref, bc_ref, p0_ref, p1_ref, g_out, a_out, h_scr = refs
    else:
        x_ref, g_ref, wa_ref, wb_ref, wc_ref, bc_ref, g_out, tail_out, h_scr, carry_scr = refs
    i = pl.program_id(0)
    j = pl.program_id(1)

    @pl.when(j == 0)
    def _():
        h_scr[...] = _rms(x_ref[...], g_ref[...]).astype(BF16)

    w = wc_ref[...]
    tm = x_ref.shape[0]
    rc = min(ROW_CHUNK, tm)
    if not per_row_state:
        @pl.when(i % tiles_per_seq == 0)
        def _():
            carry_scr[j] = jnp.zeros(carry_scr.shape[1:], F32)
        prev = carry_scr[j]
    for r0 in range(0, tm, rc):
        rows = slice(r0, r0 + rc)
        h = h_scr[rows, :]
        a = _dot(h, wa_ref[...])
        b = _dot(h, wb_ref[...])
        if per_row_state:
            a1 = p1_ref[rows, :]
            a2 = p0_ref[rows, :]
            a_out[rows, :] = a
        else:
            row = lax.broadcasted_iota(jnp.int32, a.shape, 0)
            a1 = jnp.where(row == 0, prev[7:8, :], pltpu.roll(a, 1, 0))
            a2 = jnp.where(row == 0, prev[6:7, :], jnp.where(row == 1, prev[7:8, :], pltpu.roll(a, 2, 0)))
            prev = a[rc - SUBLANES:, :]
        a_conv = bc_ref[...] + a2 * w[0:1, :] + a1 * w[1:2, :] + a * w[2:3, :]
        g_out[rows, :] = (a_conv * jax.nn.sigmoid(a_conv) * b).astype(g_out.dtype)
    if not per_row_state:
        carry_scr[j] = prev
        tail_out[...] = prev


def _ffn_up(x, g, wa, wb, wconv, bconv, *, layer, tm, tn, seq_len, prev=None):
    m, d = x.shape
    f = wa.shape[2]
    nj = f // tn
    per_row_state = prev is not None
    tiles_per_seq = max(seq_len // tm, 1)
    kern = functools.partial(_ffn_up_kernel, tiles_per_seq=tiles_per_seq, per_row_state=per_row_state)
    col = lambda r: pl.BlockSpec((r, tn), lambda i, j: (0, j))
    wcol = pl.BlockSpec((None, d, tn), lambda i, j: (layer, 0, j))
    in_specs = [pl.BlockSpec((tm, d), lambda i, j: (i, 0), pipeline_mode=pl.Buffered(1)),
                pl.BlockSpec((1, d), lambda i, j: (0, 0)), wcol, wcol, col(SUBLANES), col(1)]
    args = [x, g, wa, wb, wconv, bconv]
    tile = pl.BlockSpec((tm, tn), lambda i, j: (i, j))
    scratch = [pltpu.VMEM((tm, d), BF16)]
    if per_row_state:
        in_specs += [tile, tile]
        args += [prev[0], prev[1]]
        out_specs = [tile, tile]
        out_shape = [jax.ShapeDtypeStruct((m, f), BF16), jax.ShapeDtypeStruct((m, f), F32)]
    else:
        out_specs = [tile, pl.BlockSpec((None, SUBLANES, tn), lambda i, j: (i, 0, j))]
        out_shape = [jax.ShapeDtypeStruct((m, f), BF16), jax.ShapeDtypeStruct((m // tm, SUBLANES, f), F32)]
        scratch += [pltpu.VMEM((nj, SUBLANES, tn), F32)]
    return pl.pallas_call(
        kern, grid=(m // tm, nj), in_specs=in_specs, out_specs=out_specs, out_shape=out_shape,
        scratch_shapes=scratch, compiler_params=_params(("arbitrary", "arbitrary")), name="ffn_up",
    )(*args)


def _ffn_down_kernel(*refs, final):
    if final:
        g_ref, w_ref, x_ref, gf_ref, o_ref, y_ref = refs
    else:
        g_ref, w_ref, x_ref, o_ref = refs
    xo = x_ref[...] + _dot(g_ref[...], w_ref[...])
    o_ref[...] = xo
    if final:
        y_ref[...] = _rms(xo, gf_ref[...])


def _ffn_down(gact, w, x, g_final, *, layer, tm):
    m, d = x.shape
    f = gact.shape[1]
    final = g_final is not None
    rows = lambda wd: pl.BlockSpec((tm, wd), lambda i: (i, 0))
    in_specs = [rows(f), _resident(w, layer), rows(d)]
    args = [gact, w, x]
    out_specs = [rows(d)]
    out_shape = [jax.ShapeDtypeStruct((m, d), F32)]
    if final:
        in_specs += [pl.BlockSpec((1, d), lambda i: (0, 0))]
        args += [g_final]
        out_specs += [rows(d)]
        out_shape += [jax.ShapeDtypeStruct((m, d), F32)]
    res = pl.pallas_call(
        functools.partial(_ffn_down_kernel, final=final),
        grid=(m // tm,), in_specs=in_specs, out_specs=out_specs, out_shape=out_shape,
        compiler_params=_params(("arbitrary",)), name="ffn_down",
    )(*args)
    return (res[0], res[1]) if final else (res[0], None)


def _head_mask(rows, width):
    r = lax.broadcasted_iota(jnp.int32, (rows, width), 0)
    c = lax.broadcasted_iota(jnp.int32, (rows, width), 1)
    return (c // HEAD_DIM) == r


def _page_heads(page_ref, kv, heads):
    return jnp.concatenate(
        [page_ref[pl.ds(2 * h + kv, LANES, stride=2 * heads), :] for h in range(heads)], axis=1)


def _fox_dec_kernel(pt_ref, q_ref, kvn_ref, lfn_ref, lf_ref, *refs, pages_per_step, heads, n_pages):
    page_refs = refs[:pages_per_step]
    o_ref, m_scr, l_scr, acc_scr, run_scr = refs[pages_per_step:]
    b = pl.program_id(0)
    c = pl.program_id(1)
    w = heads * HEAD_DIM
    hm = _head_mask(SUBLANES, w)
    qbd = jnp.where(hm, jnp.broadcast_to(q_ref[...], (SUBLANES, w)), 0.0)

    @pl.when(c == 0)
    def _():
        kvn = kvn_ref[...]
        m_scr[...] = jnp.sum(qbd * kvn[:, :w], axis=1, keepdims=True)
        l_scr[...] = jnp.ones_like(l_scr)
        acc_scr[...] = jnp.broadcast_to(kvn[:, w:], (SUBLANES, w))
        run_scr[...] = lfn_ref[...]

    r = lax.broadcasted_iota(jnp.int32, (LANES, LANES), 0)
    cc = lax.broadcasted_iota(jnp.int32, (LANES, LANES), 1)
    upper = (r > cc).astype(BF16)
    qb = qbd.astype(BF16)
    run = run_scr[...]
    pad = jnp.zeros((SUBLANES - heads, LANES), F32)
    ss = []
    for u in range(pages_per_step):
        pid = pt_ref[b * n_pages + (n_pages - 1 - (c * pages_per_step + u))]
        lf = jnp.concatenate([lf_ref[h, pl.ds(pid, 1), :] for h in range(heads)] + [pad], axis=0)
        s = _dot_nt(qb, _page_heads(page_refs[u], 0, heads).astype(BF16))
        suffix = sum(_dot(part, upper) for part in _split3(lf))
        ss.append(s + run + suffix)
        run = run + jnp.sum(lf, axis=1, keepdims=True)
    s = jnp.concatenate(ss, axis=1)
    m_old = m_scr[...]
    m_new = jnp.maximum(m_old, jnp.max(s, axis=1, keepdims=True))
    alpha = jnp.exp(m_old - m_new)
    p = jnp.exp(s - m_new).astype(BF16)
    l_scr[...] = alpha * l_scr[...] + jnp.sum(p.astype(F32), axis=1, keepdims=True)
    pv = sum(_dot(p[:, u * LANES:(u + 1) * LANES], _page_heads(page_refs[u], 1, heads).astype(BF16))
             for u in range(pages_per_step))
    acc_scr[...] = alpha * acc_scr[...] + pv
    m_scr[...] = m_new
    run_scr[...] = run

    @pl.when(c == pl.num_programs(1) - 1)
    def _():
        o = jnp.where(hm, acc_scr[...] * (1.0 / l_scr[...]), 0.0)
        o_ref[...] = jnp.sum(o, axis=0, keepdims=True).astype(o_ref.dtype)


def _fox_decode(q, kv_new, lf_new, cache_pages, cache_lf, page_table, layer, *, pages_per_step):
    db, n_pages = page_table.shape
    heads = FOX_HEADS
    w = heads * HEAD_DIM
    n_phys = cache_lf.shape[2]
    steps = n_pages // pages_per_step

    def page_map(u):
        def f(b, c, pt):
            return (layer * n_phys + pt[b * n_pages + (n_pages - 1 - (c * pages_per_step + u))], 0, 0)
        return f

    in_specs = [pl.BlockSpec((None, 1, w), lambda b, c, pt: (b, 0, 0)),
                pl.BlockSpec((None, 1, 2 * w), lambda b, c, pt: (b, 0, 0)),
                pl.BlockSpec((None, SUBLANES, 1), lambda b, c, pt: (b, 0, 0)),
                pl.BlockSpec((None, heads, n_phys, LANES), lambda b, c, pt: (layer, 0, 0, 0))]
    in_specs += [pl.BlockSpec((None, 2 * w, LANES), page_map(u)) for u in range(pages_per_step)]
    kern = functools.partial(_fox_dec_kernel, pages_per_step=pages_per_step, heads=heads, n_pages=n_pages)
    return pl.pallas_call(
        kern,
        grid_spec=pltpu.PrefetchScalarGridSpec(
            num_scalar_prefetch=1, grid=(db, steps), in_specs=in_specs,
            out_specs=pl.BlockSpec((None, 1, w), lambda b, c, pt: (b, 0, 0)),
            scratch_shapes=[pltpu.VMEM((SUBLANES, 1), F32), pltpu.VMEM((SUBLANES, 1), F32),
                            pltpu.VMEM((SUBLANES, w), F32), pltpu.VMEM((SUBLANES, 1), F32)]),
        out_shape=jax.ShapeDtypeStruct((db, 1, w), BF16),
        compiler_params=_params(("arbitrary", "arbitrary")),
        name="fox_decode",
    )(page_table.reshape(-1), q, kv_new, lf_new, cache_lf, *([cache_pages] * pages_per_step))


def _moba_sel_kernel(pt_ref, q_ref, *refs, pages_per_step, heads, n_blk):
    del pt_ref
    k_refs = refs[:pages_per_step]
    sel_ref, km_scr = refs[pages_per_step:]
    c = pl.program_id(1)
    w = heads * HEAD_DIM
    pages_per_blk = MOBA_BLOCK // LANES
    blks = pages_per_step // pages_per_blk

    @pl.when(c == 0)
    def _():
        km_scr[...] = jnp.zeros_like(km_scr)

    rows = []
    for bb in range(blks):
        per_head = []
        for h in range(heads):
            tot = jnp.zeros((1, HEAD_DIM), F32)
            for pp in range(pages_per_blk):
                keys = k_refs[bb * pages_per_blk + pp][pl.ds(2 * h, LANES, stride=2 * heads), :]
                tot = tot + jnp.sum(keys, axis=0, keepdims=True)
            per_head.append(tot)
        rows.append(jnp.concatenate(per_head, axis=1) * (1.0 / MOBA_BLOCK))
    km_scr[pl.ds(pl.multiple_of(c * blks, blks), blks), :] = jnp.concatenate(rows, axis=0)

    @pl.when(c == pl.num_programs(1) - 1)
    def _():
        hm = _head_mask(SUBLANES, w)
        qbd = jnp.where(hm, jnp.broadcast_to(q_ref[...], (SUBLANES, w)), 0.0)
        km = km_scr[...]
        bs = jnp.zeros((SUBLANES, LANES), F32)
        for qp in _split3(qbd):
            for kp in _split3(km):
                bs = bs + _dot_nt(qp, kp)
        lane = lax.broadcasted_iota(jnp.int32, (SUBLANES, LANES), 1)
        bs = jnp.where(lane < n_blk, bs, -jnp.inf)
        out = jnp.zeros((SUBLANES, LANES), jnp.int32)
        for t in range(MOBA_TOPK):
            mx = jnp.max(bs, axis=1, keepdims=True)
            idx = jnp.min(jnp.where(bs == mx, lane, LANES), axis=1, keepdims=True)
            out = jnp.where(lane == t, idx, out)
            bs = jnp.where(lane == idx, -jnp.inf, bs)
        sel_ref[...] = out


def _moba_select(q, cache_pages, page_table, layer_page0, *, pages_per_step):
    db, n_pages = page_table.shape
    heads = MOBA_HEADS
    w = heads * HEAD_DIM
    n_blk = n_pages * LANES // MOBA_BLOCK
    assert MOBA_TOPK <= n_blk <= LANES and pages_per_step % (SUBLANES * MOBA_BLOCK // LANES) == 0
    steps = n_pages // pages_per_step

    def page_map(u):
        return lambda b, c, pt: (layer_page0 + pt[b * n_pages + c * pages_per_step + u], 0, 0)

    in_specs = [pl.BlockSpec((None, 1, w), lambda b, c, pt: (b, 0, 0))]
    in_specs += [pl.BlockSpec((None, 2 * w, LANES), page_map(u)) for u in range(pages_per_step)]
    kern = functools.partial(_moba_sel_kernel, pages_per_step=pages_per_step, heads=heads, n_blk=n_blk)
    return pl.pallas_call(
        kern,
        grid_spec=pltpu.PrefetchScalarGridSpec(
            num_scalar_prefetch=1, grid=(db, steps), in_specs=in_specs,
            out_specs=pl.BlockSpec((None, SUBLANES, LANES), lambda b, c, pt: (b, 0, 0)),
            scratch_shapes=[pltpu.VMEM((LANES, w), F32)]),
        out_shape=jax.ShapeDtypeStruct((db, SUBLANES, LANES), jnp.int32),
        compiler_params=_params(("arbitrary", "arbitrary")),
        name="moba_select",
    )(page_table.reshape(-1), q, *([cache_pages] * pages_per_step))


def _moba_dec_kernel(sel_ref, pt_ref, q_ref, kn_ref, vn_ref, *refs, n_pages_sel):
    del sel_ref, pt_ref
    page_refs = refs[:n_pages_sel]
    o_ref = refs[n_pages_sel]
    q = q_ref[...]
    qb = jnp.broadcast_to(q, (SUBLANES, HEAD_DIM)).astype(BF16)
    s_self = jnp.sum(q * kn_ref[...], axis=1, keepdims=True)
    s = jnp.concatenate([_dot_nt(qb, page_refs[u][:, 0, :].astype(BF16)) for u in range(n_pages_sel)], axis=1)
    m = jnp.maximum(s_self, jnp.max(s, axis=1, keepdims=True))
    p_self = jnp.exp(s_self - m)
    p = jnp.exp(s - m).astype(BF16)
    l = p_self + jnp.sum(p.astype(F32), axis=1, keepdims=True)
    acc = p_self * vn_ref[...]
    for u in range(n_pages_sel):
        acc = acc + _dot(p[:, u * LANES:(u + 1) * LANES], page_refs[u][:, 1, :].astype(BF16))
    o_ref[...] = (acc * (1.0 / l))[0:1, :].astype(o_ref.dtype)


def _moba_decode(q, kv_new, sel, cache_rows, page_table, layer_page0):
    db, n_pages = page_table.shape
    heads = MOBA_HEADS
    pages_per_blk = MOBA_BLOCK // LANES
    n_pages_sel = MOBA_TOPK * pages_per_blk

    def page_map(u):
        jsel, pg = divmod(u, pages_per_blk)

        def f(b, h, sel_s, pt):
            blk = sel_s[(b * heads + h) * MOBA_TOPK + jsel]
            return (layer_page0 + pt[b * n_pages + blk * pages_per_blk + pg], h, 0, 0)
        return f

    one = lambda col0: pl.BlockSpec((None, 1, HEAD_DIM), lambda b, h, sel_s, pt: (b, 0, col0 + h))
    in_specs = [one(0), one(0), one(heads)]
    in_specs += [pl.BlockSpec((LANES, None, 2, HEAD_DIM), page_map(u)) for u in range(n_pages_sel)]
    kern = functools.partial(_moba_dec_kernel, n_pages_sel=n_pages_sel)
    return pl.pallas_call(
        kern,
        grid_spec=pltpu.PrefetchScalarGridSpec(
            num_scalar_prefetch=2, grid=(db, heads), in_specs=in_specs,
            out_specs=pl.BlockSpec((None, 1, HEAD_DIM), lambda b, h, sel_s, pt: (b, 0, h))),
        out_shape=jax.ShapeDtypeStruct((db, 1, heads * HEAD_DIM), BF16),
        compiler_params=_params(("arbitrary", "arbitrary")),
        name="moba_decode",
    )(sel.reshape(-1), page_table.reshape(-1), q, kv_new, kv_new, *([cache_rows] * n_pages_sel))


def _mem_dec_kernel(q_ref, kv_ref, o_ref, *, heads, n_mem):
    outs = []
    for h in range(heads):
        qb = jnp.broadcast_to(q_ref[:, _head_cols(h)], (SUBLANES, HEAD_DIM)).astype(BF16)
        k = kv_ref[pl.ds(h, n_mem, stride=2 * heads), :].astype(BF16)
        v = kv_ref[pl.ds(heads + h, n_mem, stride=2 * heads), :].astype(BF16)
        s = _dot_nt(qb, k)
        p = jnp.exp(s - jnp.max(s, axis=1, keepdims=True))
        l = jnp.sum(p, axis=1, keepdims=True)
        outs.append((_dot(p.astype(BF16), v) * (1.0 / l))[0:1, :])
    o_ref[...] = jnp.concatenate(outs, axis=1).astype(o_ref.dtype)


def _mem_decode(q, mem_rows, layer, *, q_col_block):
    db = q.shape[0]
    heads = MEM_HEADS
    n_mem = mem_rows.shape[1] // (2 * heads)
    return pl.pallas_call(
        functools.partial(_mem_dec_kernel, heads=heads, n_mem=n_mem),
        grid=(db,),
        in_specs=[pl.BlockSpec((None, 1, HEAD_TILE), lambda b: (b, 0, q_col_block)),
                  pl.BlockSpec((None, mem_rows.shape[1], HEAD_DIM), lambda b: (layer * db + b, 0, 0))],
        out_specs=pl.BlockSpec((None, 1, heads * HEAD_DIM), lambda b: (b, 0, 0)),
        out_shape=jax.ShapeDtypeStruct((db, 1, heads * HEAD_DIM), BF16),
        compiler_params=_params(("arbitrary",)),
        name="mem_decode",
    )(q, mem_rows)


def _rope_tables(pos):
    half = HEAD_DIM // 2
    inv_freq = ROPE_THETA ** (-jnp.arange(half, dtype=F32) / half)
    ang = pos.astype(F32)[:, None] * inv_freq[None, :]
    cos, sin = jnp.cos(ang), jnp.sin(ang)
    return jnp.concatenate([cos, cos], axis=-1), jnp.concatenate([-sin, sin], axis=-1)


def _pad_cols(a, n):
    return jnp.pad(a, ((0, 0), (0, n - a.shape[1])))


def _prep_w_in(w_in, d):
    mw, fw, cw = MOBA_HEADS * HEAD_DIM, FOX_HEADS * HEAD_DIM, MEM_HEADS * HEAD_DIM
    o_fl = 3 * mw + 3 * fw
    o_qc = o_fl + FOX_HEADS
    o_gl = o_qc + cw
    layers, k, n = w_in.shape
    assert n == o_gl + 3 * d and cw % W_IN_COLS == 0 and (3 * d) % W_IN_COLS == 0 and o_fl % W_IN_COLS == 0
    w_t = w_in.reshape(layers, k // LANES, LANES, n).transpose(3, 1, 0, 2)
    starts = list(range(0, o_fl, W_IN_COLS)) + list(range(o_qc, o_gl, W_IN_COLS))
    starts += [o_gl] * ((HEAD_TILE - cw) // W_IN_COLS)
    starts += list(range(o_gl, n, W_IN_COLS))
    w_main = _w_in_layout(w_t, starts, W_IN_COLS)
    w_fl = _w_in_layout(w_t, [o_fl], LANES)
    return w_main, w_fl


def _prep_layer(l, g_attn, b_f, g_mem, g_ffn, w_conv, b_conv, fp):
    f = w_conv.shape[2]
    return dict(
        g_attn=g_attn[l][None], b_f=_pad_cols(b_f[l][None], LANES), g_mem=g_mem[l][None], g_ffn=g_ffn[l][None],
        w_conv=jnp.pad(w_conv[l], ((0, SUBLANES - CONV_W), (0, fp - f))), b_conv=_pad_cols(b_conv[l][None], fp))


def kernel(x_prompt, x_sample, cache_moba_kv, cache_fox_kv, cache_fox_logf, cache_mem_kv, state_ffn_conv,
           page_table, mem_prompt, g_attn, w_in, b_f, w_br_moba, w_br_fox, w_br_mem, w_out, g_mem,
           w_mem_kv, g_ffn, w_up, w_conv, b_conv, w_down, g_final):
    bsz, seq, d = x_prompt.shape
    db = x_sample.shape[0]
    depth, n_phys, page = cache_moba_kv.shape[:3]
    n_pages = page_table.shape[1]
    past_len = n_pages * page
    n_mem = mem_prompt.shape[1]
    mw, fw, cw = MOBA_HEADS * HEAD_DIM, FOX_HEADS * HEAD_DIM, MEM_HEADS * HEAD_DIM
    assert page == LANES and db == SUBLANES and x_sample.shape[1] == 1
    ff_tile = 512
    tm = max(t for t in (1024, 512, 256) if seq % t == 0)
    tq = MOBA_BLOCK

    cos_p, sin_p = _rope_tables(jnp.arange(seq))
    cos_s, sin_s = _rope_tables(jnp.full((db,), past_len))
    gf = g_final[None]

    moba_t = jnp.transpose(cache_moba_kv, (0, 1, 2, 4, 3, 5))
    moba_pages = moba_t.reshape(depth * n_phys, page * 2 * MOBA_HEADS, HEAD_DIM)
    moba_rows = moba_t.reshape(depth * n_phys * page, MOBA_HEADS, 2, HEAD_DIM)
    fox_pages = jnp.transpose(cache_fox_kv, (0, 1, 2, 4, 3, 5)).reshape(depth * n_phys, page * 2 * FOX_HEADS, HEAD_DIM)
    lf_cache = jnp.transpose(cache_fox_logf, (0, 3, 1, 2))
    mem_rows = cache_mem_kv.reshape(depth * db, n_mem * 2 * MEM_HEADS, HEAD_DIM)

    f = w_down.shape[1]
    fp = -(-f // ff_tile) * ff_tile
    w_main, w_fl = _prep_w_in(w_in, d)
    wm, wf, wc = _cast_bf16(w_br_moba, mw), _cast_bf16(w_br_fox, fw), _cast_bf16(w_br_mem, cw)
    wo, w_mem = _cast_bf16(w_out, 512), _cast_bf16(w_mem_kv, 512)
    wa, wb = _cast_w_up(w_up, fp, 256)
    wdn = _cast_w_down(w_down, fp, LANES)
    moba_out = jnp.zeros((depth, bsz, seq, MOBA_HEADS, 2, HEAD_DIM), F32)
    fox_out = jnp.zeros((depth, bsz, seq, FOX_HEADS, 2, HEAD_DIM), F32)
    xp = x_prompt.reshape(bsz * seq, d)
    xs = x_sample.reshape(db, d)
    outs = {k: [] for k in ("p_logf", "p_mem", "p_conv", "s_moba", "s_fox", "s_logf", "s_conv")}
    yp = ys = None
    for l in range(depth):
        w = _prep_layer(l, g_attn, b_f, g_mem, g_ffn, w_conv, b_conv, fp)
        g_last = gf if l == depth - 1 else None

        mkv = _norm_mm(mem_prompt.reshape(bsz * n_mem, d), w["g_mem"], w_mem, layer=l, tm=min(512, bsz * n_mem))
        q_all, moba_kv, fox_kv, gates, logf, c = _in_proj(
            xp, w["g_attn"], w_main, w_fl, w["b_f"], cos_p, sin_p, layer=l, tm=tm, seq_len=seq, q_dtype=BF16)
        q3 = q_all.reshape(bsz, seq, 3 * HEAD_TILE)
        c3 = c.reshape(bsz, seq, LANES)
        o_m, moba_out = _causal_attention(q3, moba_kv.reshape(bsz, seq, 2 * mw), mode="moba", heads=MOBA_HEADS,
                                          q_col0=0, layer=l, kv_out=moba_out)
        o_f, fox_out = _causal_attention(q3, fox_kv.reshape(bsz, seq, 2 * fw), mode="fox", heads=FOX_HEADS,
                                         q_col0=MOBA_HEADS, layer=l, kv_out=fox_out, c=c3)
        o_c = _mem_attention(q3, mkv.reshape(bsz, n_mem, 2 * cw), heads=MEM_HEADS, q_col_block=2, tq=tq)
        xp = _merge_out(o_m.reshape(bsz * seq, mw), o_f.reshape(bsz * seq, fw), o_c.reshape(bsz * seq, cw),
                        gates, xp, wm, wf, wc, wo, layer=l, tm=256)
        gact, tails = _ffn_up(xp, w["g_ffn"], wa, wb, w["w_conv"], w["b_conv"],
                              layer=l, tm=tm, tn=ff_tile, seq_len=seq)
        xp, yp = _ffn_down(gact, wdn, xp, g_last, layer=l, tm=256)
        outs["p_logf"].append(logf[:, :FOX_HEADS].reshape(bsz, seq, FOX_HEADS))
        outs["p_mem"].append(mkv.reshape(bsz, n_mem, 2, MEM_HEADS, HEAD_DIM))
        tiles_per_seq = seq // tm
        outs["p_conv"].append(tails[tiles_per_seq - 1::tiles_per_seq, SUBLANES - (CONV_W - 1):, :f])

        q_s, moba_s, fox_s, gates_s, logf_s, _ = _in_proj(
            xs, w["g_attn"], w_main, w_fl, w["b_f"], cos_s, sin_s, layer=l, tm=db, seq_len=1, q_dtype=F32)
        q_s3 = q_s.reshape(db, 1, 3 * HEAD_TILE)
        sel = _moba_select(q_s3, moba_pages, page_table, l * n_phys, pages_per_step=16)
        o_ms = _moba_decode(q_s3, moba_s.reshape(db, 1, 2 * mw), sel[:, :MOBA_HEADS, :MOBA_TOPK],
                            moba_rows, page_table, l * n_phys)
        o_fs = _fox_decode(q_s3[:, :, mw:mw + fw], fox_s.reshape(db, 1, 2 * fw),
                           logf_s[:, :SUBLANES].reshape(db, SUBLANES, 1), fox_pages, lf_cache, page_table,
                           l, pages_per_step=8)
        o_cs = _mem_decode(q_s3, mem_rows, l, q_col_block=2)
        xs = _merge_out(o_ms.reshape(db, mw), o_fs.reshape(db, fw), o_cs.reshape(db, cw), gates_s, xs,
                        wm, wf, wc, wo, layer=l, tm=db)
        prev = jnp.pad(state_ffn_conv[l], ((0, 0), (0, 0), (0, fp - f)))
        gact_s, a_s = _ffn_up(xs, w["g_ffn"], wa, wb, w["w_conv"], w["b_conv"],
                              layer=l, tm=db, tn=ff_tile, seq_len=1, prev=(prev[:, 0], prev[:, 1]))
        xs, ys = _ffn_down(gact_s, wdn, xs, g_last, layer=l, tm=db)
        outs["s_moba"].append(moba_s.reshape(db, 1, 2, MOBA_HEADS, HEAD_DIM))
        outs["s_fox"].append(fox_s.reshape(db, 1, 2, FOX_HEADS, HEAD_DIM))
        outs["s_logf"].append(logf_s[:, :FOX_HEADS].reshape(db, 1, FOX_HEADS))
        outs["s_conv"].append(jnp.stack([state_ffn_conv[l][:, 1], a_s[:, :f]], axis=1))

    st = lambda k: jnp.stack(outs[k])
    kv_order = (0, 1, 2, 4, 3, 5)
    return (yp.reshape(bsz, seq, d), ys.reshape(db, 1, d),
            jnp.transpose(moba_out, kv_order), jnp.transpose(fox_out, kv_order),
            st("p_logf"), st("p_mem"), st("p_conv"),
            st("s_moba"), st("s_fox"), st("s_logf"), st("s_conv"))
```

```python
import functools

import jax
import jax.numpy as jnp
from jax import lax
from jax.experimental import pallas as pl
from jax.experimental.pallas import tpu as pltpu

F32 = jnp.float32
BF16 = jnp.bfloat16

HEAD_DIM = 128
MOBA_HEADS = 6
FOX_HEADS = 6
MEM_HEADS = 4
MOBA_BLOCK = 256
MOBA_TOPK = 3
CONV_W = 3
ROPE_THETA = 10000.0
NORM_EPS = 1e-6
NEG = -1e30

LANES = 128
SUBLANES = 8
HEAD_TILE = MOBA_HEADS * HEAD_DIM
VMEM_LIMIT = 56 * 1024 * 1024
ROW_CHUNK = 256
SCALE = HEAD_DIM ** -0.5

_NT = (((1,), (1,)), ((), ()))


def _params(sem):
    return pltpu.CompilerParams(dimension_semantics=sem, vmem_limit_bytes=VMEM_LIMIT)


def _rms(x, g):
    return x * lax.rsqrt(jnp.mean(x * x, axis=-1, keepdims=True) + NORM_EPS) * g


def _log_sigmoid(x):
    return jnp.minimum(x, 0.0) - jnp.log1p(jnp.exp(-jnp.abs(x)))


def _split3(v):
    hi = v.astype(BF16)
    r1 = v - hi.astype(F32)
    mid = r1.astype(BF16)
    lo = (r1 - mid.astype(F32)).astype(BF16)
    return hi, mid, lo


def _dot(a, b):
    return jnp.dot(a, b, preferred_element_type=F32)


def _dot_nt(a, b):
    return lax.dot_general(a, b, _NT, preferred_element_type=F32)


def _cumsum_rows(v, carry, blk=256):
    tm = v.shape[0]
    blk = min(blk, tm)
    r = lax.broadcasted_iota(jnp.int32, (blk, blk), 0)
    c = lax.broadcasted_iota(jnp.int32, (blk, blk), 1)
    tri = (c <= r).astype(BF16)
    outs = []
    for s in range(0, tm, blk):
        hi, mid, lo = _split3(v[s:s + blk])
        cs = _dot(tri, hi) + _dot(tri, mid) + _dot(tri, lo) + carry
        outs.append(cs)
        carry = cs[blk - 1:blk, :]
    return jnp.concatenate(outs, axis=0), carry


def _rope(acc, cos, sin_signed):
    outs = []
    for hh in range(acc.shape[1] // HEAD_DIM):
        xh = acc[:, hh * HEAD_DIM:(hh + 1) * HEAD_DIM]
        outs.append(xh * cos + pltpu.roll(xh, HEAD_DIM // 2, 1) * sin_signed)
    return jnp.concatenate(outs, axis=1)


def _head_cols(h):
    return slice(h * HEAD_DIM, (h + 1) * HEAD_DIM)


W_IN_COLS = 256


def _w_in_kernel(starts_ref, x_ref, o_ref):
    del starts_ref
    cols, kt, layers, _ = x_ref.shape
    for l in range(layers):
        for t in range(kt):
            o_ref[l, t * LANES:(t + 1) * LANES, :] = x_ref[:, t, l, :].T.astype(o_ref.dtype)


def _w_in_layout(w_t, starts, cols):
    _, kt, layers, _ = w_t.shape
    starts = jnp.asarray(starts, jnp.int32)
    return pl.pallas_call(
        _w_in_kernel,
        grid_spec=pltpu.PrefetchScalarGridSpec(
            num_scalar_prefetch=1, grid=(starts.shape[0],),
            in_specs=[pl.BlockSpec((pl.Element(cols), pl.Element(kt), pl.Element(layers), pl.Element(LANES)),
                                   lambda j, st: (st[j], 0, 0, 0))],
            out_specs=pl.BlockSpec((layers, kt * LANES, cols), lambda j, st: (0, 0, j))),
        out_shape=jax.ShapeDtypeStruct((layers, kt * LANES, starts.shape[0] * cols), BF16),
        compiler_params=_params(("arbitrary",)),
        name="w_in_layout",
    )(starts, w_t)


def _cast_kernel(x_ref, o_ref):
    o_ref[...] = x_ref[...].astype(o_ref.dtype)


def _cast_bf16(w, rows):
    layers, r, c = w.shape
    rows = min(rows, r)
    assert r % rows == 0
    spec = pl.BlockSpec((None, rows, c), lambda l, i: (l, i, 0))
    return pl.pallas_call(
        _cast_kernel, grid=(layers, r // rows), in_specs=[spec], out_specs=spec,
        out_shape=jax.ShapeDtypeStruct(w.shape, BF16), compiler_params=_params(("arbitrary", "arbitrary")),
        name="cast_bf16")(w)


def _cast_up_kernel(x_ref, a_ref, b_ref, *, f):
    for o_ref, c0 in ((a_ref, 0), (b_ref, f)):
        o_ref[:, :f] = x_ref[:, c0:c0 + f].astype(o_ref.dtype)
        o_ref[:, f:] = jnp.zeros((o_ref.shape[0], o_ref.shape[1] - f), o_ref.dtype)


def _cast_w_up(w_up, fp, rows):
    layers, d, f2 = w_up.shape
    out = pl.BlockSpec((None, rows, fp), lambda l, i: (l, i, 0))
    return pl.pallas_call(
        functools.partial(_cast_up_kernel, f=f2 // 2), grid=(layers, d // rows),
        in_specs=[pl.BlockSpec((None, rows, f2), lambda l, i: (l, i, 0))], out_specs=[out, out],
        out_shape=[jax.ShapeDtypeStruct((layers, d, fp), BF16)] * 2,
        compiler_params=_params(("arbitrary", "arbitrary")), name="cast_w_up")(w_up)


def _in_proj_kernel(x_ref, g_ref, w_ref, wfl_ref, bf_ref, cos_ref, sin_ref,
                    q_ref, mkv_ref, fkv_ref, gate_ref, logf_ref, c_ref,
                    h_scr, carry_scr, *, tiles_per_seq, cumsum):
    i = pl.program_id(0)
    j = pl.program_id(1)

    @pl.when(j == 0)
    def _():
        h = _rms(x_ref[...], g_ref[...]).astype(BF16)
        h_scr[...] = h
        logf = _log_sigmoid(_dot(h, wfl_ref[...]) + bf_ref[...])
        logf_ref[...] = logf
        if cumsum:
            @pl.when(i % tiles_per_seq == 0)
            def _():
                carry_scr[...] = jnp.zeros_like(carry_scr)
            c, carry = _cumsum_rows(logf, carry_scr[...])
            c_ref[...] = c
            carry_scr[...] = carry
        else:
            c_ref[...] = logf

    tm = x_ref.shape[0]
    rc = min(ROW_CHUNK, tm)

    def tile(out_ref, epilogue):
        for r0 in range(0, tm, rc):
            rows = slice(r0, r0 + rc)
            acc = _dot(h_scr[rows, :], w_ref[...])
            out_ref[rows, :] = epilogue(acc, rows).astype(out_ref.dtype)

    rope = lambda acc, rows: _rope(acc, cos_ref[rows, :], sin_ref[rows, :])

    @pl.when(j == 0)
    def _():
        tile(q_ref, lambda acc, rows: rope(acc, rows) * SCALE)

    @pl.when(j == 1)
    def _():
        tile(mkv_ref, rope)

    @pl.when(j == 2)
    def _():
        tile(mkv_ref, lambda acc, rows: acc)

    @pl.when(jnp.logical_or(j == 3, j == 6))
    def _():
        tile(q_ref, lambda acc, rows: acc * SCALE)

    @pl.when(jnp.logical_or(j == 4, j == 5))
    def _():
        tile(fkv_ref, lambda acc, rows: acc)

    @pl.when(j >= 7)
    def _():
        tile(gate_ref, lambda acc, rows: jax.nn.sigmoid(acc))


def _in_proj(x, g, w_main, w_fl, b_f, cos, sin, *, layer, tm, seq_len, q_dtype):
    m, d = x.shape
    n_tiles = w_main.shape[2] // HEAD_TILE
    n_gate = n_tiles - 7
    table_tiles = cos.shape[0] // tm
    cumsum = seq_len > 1
    tiles_per_seq = max(seq_len // tm, 1)
    kern = functools.partial(_in_proj_kernel, tiles_per_seq=tiles_per_seq, cumsum=cumsum)
    row = lambda i, j: (i, 0)
    return pl.pallas_call(
        kern,
        grid=(m // tm, n_tiles),
        in_specs=[
            pl.BlockSpec((tm, d), row, pipeline_mode=pl.Buffered(1)),
            pl.BlockSpec((1, d), lambda i, j: (0, 0)),
            pl.BlockSpec((None, d, HEAD_TILE), lambda i, j: (layer, 0, j)),
            pl.BlockSpec((None, d, LANES), lambda i, j: (layer, 0, 0)),
            pl.BlockSpec((1, LANES), lambda i, j: (0, 0)),
            pl.BlockSpec((tm, LANES), lambda i, j: (i % table_tiles, 0)),
            pl.BlockSpec((tm, LANES), lambda i, j: (i % table_tiles, 0)),
        ],
        out_specs=[
            pl.BlockSpec((tm, HEAD_TILE), lambda i, j: (i, jnp.minimum(j // 3, 2))),
            pl.BlockSpec((tm, HEAD_TILE), lambda i, j: (i, jnp.clip(j - 1, 0, 1))),
            pl.BlockSpec((tm, HEAD_TILE), lambda i, j: (i, jnp.clip(j - 4, 0, 1))),
            pl.BlockSpec((tm, HEAD_TILE), lambda i, j: (i, jnp.clip(j - 7, 0, n_gate - 1))),
            pl.BlockSpec((tm, LANES), row),
            pl.BlockSpec((tm, LANES), row),
        ],
        out_shape=[
            jax.ShapeDtypeStruct((m, 3 * HEAD_TILE), q_dtype),
            jax.ShapeDtypeStruct((m, 2 * HEAD_TILE), F32),
            jax.ShapeDtypeStruct((m, 2 * HEAD_TILE), F32),
            jax.ShapeDtypeStruct((m, n_gate * HEAD_TILE), BF16),
            jax.ShapeDtypeStruct((m, LANES), F32),
            jax.ShapeDtypeStruct((m, LANES), F32),
        ],
        scratch_shapes=[pltpu.VMEM((tm, d), BF16), pltpu.VMEM((1, LANES), F32)],
        compiler_params=_params(("arbitrary", "arbitrary")),
        name="in_proj",
    )(x, g, w_main, w_fl, b_f, cos, sin)


def _norm_mm_kernel(x_ref, g_ref, w_ref, o_ref):
    h = _rms(x_ref[...], g_ref[...]).astype(BF16)
    o_ref[...] = _dot(h, w_ref[...])


def _norm_mm(x, g, w, *, layer, tm):
    m, d = x.shape
    n = w.shape[2]
    return pl.pallas_call(
        _norm_mm_kernel,
        grid=(m // tm,),
        in_specs=[pl.BlockSpec((tm, d), lambda i: (i, 0)),
                  pl.BlockSpec((1, d), lambda i: (0, 0)),
                  pl.BlockSpec((None, d, n), lambda i: (layer, 0, 0))],
        out_specs=pl.BlockSpec((tm, n), lambda i: (i, 0)),
        out_shape=jax.ShapeDtypeStruct((m, n), F32),
        compiler_params=_params(("arbitrary",)),
        name="mem_kv_proj",
    )(x, g, w)


MASK_BIAS = 1e30


def _causal_attn_kernel(*refs, mode, blk, n_blk):
    q_ref, k_ref, v_ref = refs[:3]
    c_ref = refs[3] if mode == "fox" else None
    o_ref, kv_out_ref = refs[-2:]
    h = pl.program_id(1)
    s_len = q_ref.shape[0]
    q = q_ref[...]
    kf = k_ref[...]
    vf = v_ref[...]
    kv_out_ref[:, 0, :] = kf
    kv_out_ref[:, 1, :] = vf
    vb = vf.astype(BF16)
    lane = lax.broadcasted_iota(jnp.int32, (s_len, LANES), 1)

    if mode == "fox":
        c_col = jnp.sum(jnp.where(lane == h, c_ref[...], 0.0), axis=1, keepdims=True)
        hi, mid, lo = [part.astype(F32) for part in _split3(c_col)]
        aug_q = jnp.where(lane == 0, hi, jnp.where(lane == 1, mid, jnp.where(lane == 2, lo,
                          jnp.where(lane < 6, 1.0, 0.0))))
        aug_k = jnp.where(lane < 3, 1.0, jnp.where(lane == 3, -hi, jnp.where(lane == 4, -mid,
                          jnp.where(lane == 5, -lo, 0.0))))
    else:
        nb8 = -(-n_blk // SUBLANES) * SUBLANES
        km = [jnp.sum(kf[n * blk:(n + 1) * blk], axis=0, keepdims=True) * (1.0 / blk) for n in range(n_blk)]
        km = jnp.concatenate(km + [jnp.zeros((1, HEAD_DIM), F32)] * (nb8 - n_blk), axis=0)
        bs = sum(_dot_nt(part, q) for part in _split3(km))
        blk_id = lax.broadcasted_iota(jnp.int32, (nb8, s_len), 0)
        own = lax.broadcasted_iota(jnp.int32, (nb8, s_len), 1) // blk
        cnt = jnp.zeros((nb8, s_len), F32)
        for mth in range(n_blk - 1):
            sm = bs[mth:mth + 1, :]
            beats = jnp.logical_or(sm > bs, jnp.logical_and(sm == bs, mth < blk_id))
            cnt = cnt + jnp.where(jnp.logical_and(beats, mth < own), 1.0, 0.0)
        sel = jnp.where(jnp.logical_and(blk_id < own, cnt < MOBA_TOPK), 1.0, 0.0)
        sel = jnp.concatenate([sel, jnp.zeros((LANES - nb8, s_len), F32)], axis=0).astype(BF16)
        eye = (lax.broadcasted_iota(jnp.int32, (blk, blk), 0)
               == lax.broadcasted_iota(jnp.int32, (blk, blk), 1)).astype(BF16)
        selc = jnp.concatenate([_dot_nt(eye, sel[:, t * blk:(t + 1) * blk]) for t in range(n_blk)], axis=0)
        row_blk = lax.broadcasted_iota(jnp.int32, (s_len, LANES), 0) // blk
        aug_q = jnp.where(lane == row_blk, 0.0, (selc - 1.0) * MASK_BIAS)
        aug_k = jnp.where(lane == row_blk, 1.0, 0.0)

    q_aug = jnp.concatenate([q, aug_q.astype(BF16)], axis=1)
    k_aug = jnp.concatenate([kf.astype(BF16), aug_k.astype(BF16)], axis=1)
    row = lax.broadcasted_iota(jnp.int32, (blk, blk), 0)
    col = lax.broadcasted_iota(jnp.int32, (blk, blk), 1)
    for t in range(n_blk):
        n = (t + 1) * blk
        s = _dot_nt(q_aug[t * blk:(t + 1) * blk], k_aug[:n])
        diag = jnp.where(col <= row, s[:, n - blk:], NEG)
        s = diag if t == 0 else jnp.concatenate([s[:, :n - blk], diag], axis=1)
        p = jnp.exp(s - jnp.max(s, axis=1, keepdims=True))
        l = jnp.sum(p, axis=1, keepdims=True)
        o_ref[t * blk:(t + 1) * blk, :] = (_dot(p.astype(BF16), vb[:n]) * (1.0 / l)).astype(o_ref.dtype)


def _causal_attention(q, kv, *, mode, heads, q_col0, layer, kv_out, c=None):
    b, s, _ = q.shape
    blk = MOBA_BLOCK
    n_blk = s // blk
    assert s % blk == 0 and n_blk <= LANES and kv.shape[1] == s
    seq_head = lambda col0: pl.BlockSpec((None, s, HEAD_DIM), lambda bi, h: (bi, 0, col0 + h))
    in_specs = [seq_head(q_col0), seq_head(0), seq_head(heads)]
    args = [q, kv, kv]
    if mode == "fox":
        in_specs += [pl.BlockSpec((None, s, LANES), lambda bi, h: (bi, 0, 0))]
        args += [c]
    aliases = {len(args): 1}
    in_specs += [pl.BlockSpec(memory_space=pl.ANY)]
    args += [kv_out]
    return pl.pallas_call(
        functools.partial(_causal_attn_kernel, mode=mode, blk=blk, n_blk=n_blk),
        grid=(b, heads),
        in_specs=in_specs,
        out_specs=[seq_head(0),
                   pl.BlockSpec((None, None, s, None, 2, HEAD_DIM), lambda bi, h: (layer, bi, 0, h, 0, 0))],
        out_shape=[jax.ShapeDtypeStruct((b, s, heads * HEAD_DIM), BF16),
                   jax.ShapeDtypeStruct(kv_out.shape, kv_out.dtype)],
        input_output_aliases=aliases,
        compiler_params=_params(("arbitrary", "arbitrary")),
        name="attn_" + mode,
    )(*args)


def _mem_attn_kernel(q_ref, kv_ref, o_ref, kb, vb, *, heads):
    w = heads * HEAD_DIM

    @pl.when(pl.program_id(1) == 0)
    def _():
        kb[...] = kv_ref[:, :w].astype(BF16)
        vb[...] = kv_ref[:, w:].astype(BF16)

    for h in range(heads):
        s = _dot_nt(q_ref[:, _head_cols(h)], kb[:, _head_cols(h)])
        p = jnp.exp(s - jnp.max(s, axis=1, keepdims=True))
        l = jnp.sum(p, axis=1, keepdims=True)
        o_ref[:, _head_cols(h)] = (_dot(p.astype(BF16), vb[:, _head_cols(h)]) * (1.0 / l)).astype(o_ref.dtype)


def _mem_attention(q, kv, *, heads, q_col_block, tq):
    b, s, _ = q.shape
    n_mem = kv.shape[1]
    w = heads * HEAD_DIM
    return pl.pallas_call(
        functools.partial(_mem_attn_kernel, heads=heads),
        grid=(b, s // tq),
        in_specs=[pl.BlockSpec((None, tq, HEAD_TILE), lambda bi, qi: (bi, qi, q_col_block)),
                  pl.BlockSpec((None, n_mem, 2 * w), lambda bi, qi: (bi, 0, 0))],
        out_specs=pl.BlockSpec((None, tq, w), lambda bi, qi: (bi, qi, 0)),
        out_shape=jax.ShapeDtypeStruct((b, s, w), BF16),
        scratch_shapes=[pltpu.VMEM((n_mem, w), BF16), pltpu.VMEM((n_mem, w), BF16)],
        compiler_params=_params(("arbitrary", "arbitrary")),
        name="attn_mem",
    )(q, kv)


def _merge_kernel(om_ref, of_ref, oc_ref, gate_ref, x_ref, wm_ref, wf_ref, wc_ref, wo_ref, o_ref):
    d = x_ref.shape[1]
    merged = gate_ref[:, 0:d].astype(F32) * _dot(om_ref[...], wm_ref[...])
    merged = merged + gate_ref[:, d:2 * d].astype(F32) * _dot(of_ref[...], wf_ref[...])
    merged = merged + gate_ref[:, 2 * d:3 * d].astype(F32) * _dot(oc_ref[...], wc_ref[...])
    o_ref[...] = x_ref[...] + _dot(merged.astype(BF16), wo_ref[...])


def _resident(w, layer):
    return pl.BlockSpec((None,) + w.shape[1:], lambda i: (layer, 0, 0), pipeline_mode=pl.Buffered(1))


def _merge_out(om, of, oc, gates, x, wm, wf, wc, wo, *, layer, tm):
    m, d = x.shape
    rows = lambda w: pl.BlockSpec((tm, w), lambda i: (i, 0))
    return pl.pallas_call(
        _merge_kernel,
        grid=(m // tm,),
        in_specs=[rows(om.shape[1]), rows(of.shape[1]), rows(oc.shape[1]), rows(gates.shape[1]), rows(d),
                  _resident(wm, layer), _resident(wf, layer), _resident(wc, layer), _resident(wo, layer)],
        out_specs=rows(d),
        out_shape=jax.ShapeDtypeStruct((m, d), F32),
        compiler_params=_params(("arbitrary",)),
        name="merge_out",
    )(om, of, oc, gates, x, wm, wf, wc, wo)


def _block_means(page_refs, km_out, t, *, chunks_per_seq, n_seq, heads):
    pages_per_blk = MOBA_BLOCK // LANES
    blks = len(page_refs) // pages_per_blk
    c = t % chunks_per_seq

    @pl.when(t < n_seq * chunks_per_seq)
    def _():
        @pl.when(c == 0)
        def _():
            km_out[...] = jnp.zeros_like(km_out)
        for bb in range(blks):
            per_head = []
            for h in range(heads):
                tot = jnp.zeros((1, HEAD_DIM), F32)
                for pp in range(pages_per_blk):
                    keys = page_refs[bb * pages_per_blk + pp][pl.ds(2 * h, LANES, stride=2 * heads), :]
                    tot = tot + jnp.sum(keys, axis=0, keepdims=True)
                per_head.append(tot)
            km_out[pl.ds(c * blks + bb, 1), :] = jnp.concatenate(per_head, axis=1) * (1.0 / MOBA_BLOCK)


def _ffn_up_kernel(*refs, tiles_per_seq, per_row_state, kmean=None):
    if per_row_state:
        x_ref, g_ref, wa_ref, wb_ref, wc_ref, bc_ref, p0_ref, p1_ref, g_out, a_out, h_scr = refs
    elif kmean is None:
        x_ref, g_ref, wa_ref, wb_ref, wc_ref, bc_ref, g_out, tail_out, h_scr, carry_scr = refs
    else:
        n_pg = kmean["pages_per_step"]
        x_ref, g_ref, wa_ref, wb_ref, wc_ref, bc_ref = refs[1:7]
        page_refs = refs[7:7 + n_pg]
        g_out, tail_out, km_out, h_scr, carry_scr = refs[7 + n_pg:]
    i = pl.program_id(0)
    j = pl.program_id(1)
    if kmean is not None:
        _block_means(page_refs, km_out, i * pl.num_programs(1) + j, chunks_per_seq=kmean["chunks_per_seq"],
                     n_seq=kmean["n_seq"], heads=kmean["heads"])

    @pl.when(j == 0)
    def _():
        h_scr[...] = _rms(x_ref[...], g_ref[...]).astype(BF16)

    w = wc_ref[...]
    tm = x_ref.shape[0]
    rc = min(ROW_CHUNK, tm)
    if not per_row_state:
        @pl.when(i % tiles_per_seq == 0)
        def _():
            carry_scr[j] = jnp.zeros(carry_scr.shape[1:], F32)
        prev = carry_scr[j]
    for r0 in range(0, tm, rc):
        rows = slice(r0, r0 + rc)
        h = h_scr[rows, :]
        a = _dot(h, wa_ref[...])
        b = _dot(h, wb_ref[...])
        if per_row_state:
            a1 = p1_ref[rows, :]
            a2 = p0_ref[rows, :]
            a_out[rows, :] = a
        else:
            row = lax.broadcasted_iota(jnp.int32, a.shape, 0)
            a1 = jnp.where(row == 0, prev[7:8, :], pltpu.roll(a, 1, 0))
            a2 = jnp.where(row == 0, prev[6:7, :], jnp.where(row == 1, prev[7:8, :], pltpu.roll(a, 2, 0)))
            prev = a[rc - SUBLANES:, :]
        a_conv = bc_ref[...] + a2 * w[0:1, :] + a1 * w[1:2, :] + a * w[2:3, :]
        g_out[rows, :] = (a_conv * jax.nn.sigmoid(a_conv) * b).astype(g_out.dtype)
    if not per_row_state:
        carry_scr[j] = prev
        tail_out[...] = prev


def _ffn_up(x, g, wa, wb, wconv, bconv, *, layer, tm, tn, seq_len, prev=None, kmean_src=None):
    m, d = x.shape
    f = wa.shape[2]
    nj = f // tn
    per_row_state = prev is not None
    tiles_per_seq = max(seq_len // tm, 1)
    col = lambda r: pl.BlockSpec((r, tn), lambda i, j, *_: (0, j))
    wcol = pl.BlockSpec((None, d, tn), lambda i, j, *_: (layer, 0, j))
    in_specs = [pl.BlockSpec((tm, d), lambda i, j, *_: (i, 0), pipeline_mode=pl.Buffered(1)),
                pl.BlockSpec((1, d), lambda i, j, *_: (0, 0)), wcol, wcol, col(SUBLANES), col(1)]
    args = [x, g, wa, wb, wconv, bconv]
    tile = pl.BlockSpec((tm, tn), lambda i, j, *_: (i, j))
    scratch = [pltpu.VMEM((tm, d), BF16)]
    kmean = None
    prefetch = []
    if per_row_state:
        in_specs += [tile, tile]
        args += [prev[0], prev[1]]
        out_specs = [tile, tile]
        out_shape = [jax.ShapeDtypeStruct((m, f), BF16), jax.ShapeDtypeStruct((m, f), F32)]
    else:
        out_specs = [tile, pl.BlockSpec((None, SUBLANES, tn), lambda i, j, *_: (i, 0, j))]
        out_shape = [jax.ShapeDtypeStruct((m, f), BF16), jax.ShapeDtypeStruct((m // tm, SUBLANES, f), F32)]
        scratch += [pltpu.VMEM((nj, SUBLANES, tn), F32)]
        if kmean_src is not None:
            cache_pages, page_table, layer_page0 = kmean_src
            n_seq, n_pages = page_table.shape
            pages_per_blk = MOBA_BLOCK // LANES
            steps = (m // tm) * nj
            chunks_per_seq = steps // n_seq
            assert chunks_per_seq >= 1
            pages_per_step = -(-n_pages // (chunks_per_seq * pages_per_blk)) * pages_per_blk
            chunks_per_seq = -(-n_pages // pages_per_step)
            assert chunks_per_seq * pages_per_step // pages_per_blk <= LANES
            kmean = dict(pages_per_step=pages_per_step, chunks_per_seq=chunks_per_seq, n_seq=n_seq, heads=MOBA_HEADS)
            seq_of = lambda i, j: jnp.minimum((i * nj + j) // chunks_per_seq, n_seq - 1)

            def page_map(u):
                def fn(i, j, pt):
                    pg = jnp.minimum(((i * nj + j) % chunks_per_seq) * pages_per_step + u, n_pages - 1)
                    return (layer_page0 + pt[seq_of(i, j) * n_pages + pg], 0, 0)
                return fn

            in_specs += [pl.BlockSpec((None,) + cache_pages.shape[1:], page_map(u)) for u in range(pages_per_step)]
            args += [cache_pages] * pages_per_step
            prefetch = [page_table.reshape(-1)]
            w_heads = MOBA_HEADS * HEAD_DIM
            out_specs += [pl.BlockSpec((None, LANES, w_heads), lambda i, j, pt: (seq_of(i, j), 0, 0))]
            out_shape += [jax.ShapeDtypeStruct((n_seq, LANES, w_heads), F32)]
    kern = functools.partial(_ffn_up_kernel, tiles_per_seq=tiles_per_seq, per_row_state=per_row_state, kmean=kmean)
    return pl.pallas_call(
        kern,
        grid_spec=pltpu.PrefetchScalarGridSpec(
            num_scalar_prefetch=len(prefetch), grid=(m // tm, nj), in_specs=in_specs, out_specs=out_specs,
            scratch_shapes=scratch),
        out_shape=out_shape, compiler_params=_params(("arbitrary", "arbitrary")), name="ffn_up",
    )(*prefetch, *args)


def _ffn_down_kernel(*refs, final):
    if final:
        g_ref, w_ref, x_ref, gf_ref, o_ref, y_ref = refs
    else:
        g_ref, w_ref, x_ref, o_ref = refs
    xo = x_ref[...] + _dot(g_ref[...], w_ref[...])
    o_ref[...] = xo
    if final:
        y_ref[...] = _rms(xo, gf_ref[...])


def _ffn_down(gact, w, x, g_final, *, layer, tm):
    m, d = x.shape
    f = w.shape[1]
    final = g_final is not None
    rows = lambda wd: pl.BlockSpec((tm, wd), lambda i: (i, 0))
    in_specs = [rows(f), _resident(w, layer), rows(d)]
    args = [gact, w, x]
    out_specs = [rows(d)]
    out_shape = [jax.ShapeDtypeStruct((m, d), F32)]
    if final:
        in_specs += [pl.BlockSpec((1, d), lambda i: (0, 0))]
        args += [g_final]
        out_specs += [rows(d)]
        out_shape += [jax.ShapeDtypeStruct((m, d), F32)]
    res = pl.pallas_call(
        functools.partial(_ffn_down_kernel, final=final),
        grid=(m // tm,), in_specs=in_specs, out_specs=out_specs, out_shape=out_shape,
        compiler_params=_params(("arbitrary",)), name="ffn_down",
    )(*args)
    return (res[0], res[1]) if final else (res[0], None)


def _head_mask(rows, width):
    r = lax.broadcasted_iota(jnp.int32, (rows, width), 0)
    c = lax.broadcasted_iota(jnp.int32, (rows, width), 1)
    return (c // HEAD_DIM) == r


def _page_heads(page_ref, kv, heads):
    return jnp.concatenate(
        [page_ref[pl.ds(2 * h + kv, LANES, stride=2 * heads), :] for h in range(heads)], axis=1)


def _fox_dec_kernel(pt_ref, q_ref, kvn_ref, lfn_ref, lf_ref, *refs, pages_per_step, heads, n_pages):
    page_refs = refs[:pages_per_step]
    o_ref, m_scr, l_scr, acc_scr, run_scr = refs[pages_per_step:]
    b = pl.program_id(0)
    c = pl.program_id(1)
    w = heads * HEAD_DIM
    hm = _head_mask(SUBLANES, w)
    qbd = jnp.where(hm, jnp.broadcast_to(q_ref[...], (SUBLANES, w)), 0.0)

    @pl.when(c == 0)
    def _():
        kvn = kvn_ref[...]
        m_scr[...] = jnp.sum(qbd * kvn[:, :w], axis=1, keepdims=True)
        l_scr[...] = jnp.ones_like(l_scr)
        acc_scr[...] = jnp.broadcast_to(kvn[:, w:], (SUBLANES, w))
        run_scr[...] = lfn_ref[...]

    r = lax.broadcasted_iota(jnp.int32, (LANES, LANES), 0)
    cc = lax.broadcasted_iota(jnp.int32, (LANES, LANES), 1)
    upper = (r > cc).astype(BF16)
    qb = qbd.astype(BF16)
    run = run_scr[...]
    pad = jnp.zeros((SUBLANES - heads, LANES), F32)
    ss = []
    for u in range(pages_per_step):
        pid = pt_ref[b * n_pages + (n_pages - 1 - (c * pages_per_step + u))]
        lf = jnp.concatenate([lf_ref[h, pl.ds(pid, 1), :] for h in range(heads)] + [pad], axis=0)
        s = _dot_nt(qb, _page_heads(page_refs[u], 0, heads).astype(BF16))
        suffix = sum(_dot(part, upper) for part in _split3(lf))
        ss.append(s + run + suffix)
        run = run + jnp.sum(lf, axis=1, keepdims=True)
    s = jnp.concatenate(ss, axis=1)
    m_old = m_scr[...]
    m_new = jnp.maximum(m_old, jnp.max(s, axis=1, keepdims=True))
    alpha = jnp.exp(m_old - m_new)
    p = jnp.exp(s - m_new).astype(BF16)
    l_scr[...] = alpha * l_scr[...] + jnp.sum(p.astype(F32), axis=1, keepdims=True)
    pv = sum(_dot(p[:, u * LANES:(u + 1) * LANES], _page_heads(page_refs[u], 1, heads).astype(BF16))
             for u in range(pages_per_step))
    acc_scr[...] = alpha * acc_scr[...] + pv
    m_scr[...] = m_new
    run_scr[...] = run

    @pl.when(c == pl.num_programs(1) - 1)
    def _():
        o = jnp.where(hm, acc_scr[...] * (1.0 / l_scr[...]), 0.0)
        o_ref[...] = jnp.sum(o, axis=0, keepdims=True).astype(o_ref.dtype)


def _fox_decode(q, kv_new, lf_new, cache_pages, cache_lf, page_table, layer, *, pages_per_step):
    db, n_pages = page_table.shape
    heads = FOX_HEADS
    w = heads * HEAD_DIM
    n_phys = cache_lf.shape[2]
    steps = n_pages // pages_per_step

    def page_map(u):
        def f(b, c, pt):
            return (layer * n_phys + pt[b * n_pages + (n_pages - 1 - (c * pages_per_step + u))], 0, 0)
        return f

    in_specs = [pl.BlockSpec((None, 1, w), lambda b, c, pt: (b, 0, 0)),
                pl.BlockSpec((None, 1, 2 * w), lambda b, c, pt: (b, 0, 0)),
                pl.BlockSpec((None, SUBLANES, 1), lambda b, c, pt: (b, 0, 0)),
                pl.BlockSpec((None, heads, n_phys, LANES), lambda b, c, pt: (layer, 0, 0, 0))]
    in_specs += [pl.BlockSpec((None, 2 * w, LANES), page_map(u)) for u in range(pages_per_step)]
    kern = functools.partial(_fox_dec_kernel, pages_per_step=pages_per_step, heads=heads, n_pages=n_pages)
    return pl.pallas_call(
        kern,
        grid_spec=pltpu.PrefetchScalarGridSpec(
            num_scalar_prefetch=1, grid=(db, steps), in_specs=in_specs,
            out_specs=pl.BlockSpec((None, 1, w), lambda b, c, pt: (b, 0, 0)),
            scratch_shapes=[pltpu.VMEM((SUBLANES, 1), F32), pltpu.VMEM((SUBLANES, 1), F32),
                            pltpu.VMEM((SUBLANES, w), F32), pltpu.VMEM((SUBLANES, 1), F32)]),
        out_shape=jax.ShapeDtypeStruct((db, 1, w), BF16),
        compiler_params=_params(("arbitrary", "arbitrary")),
        name="fox_decode",
    )(page_table.reshape(-1), q, kv_new, lf_new, cache_lf, *([cache_pages] * pages_per_step))


def _moba_topk_kernel(q_ref, km_ref, sel_ref, *, heads, n_blk):
    w = heads * HEAD_DIM
    hm = _head_mask(SUBLANES, w)
    qbd = jnp.where(hm, jnp.broadcast_to(q_ref[...], (SUBLANES, w)), 0.0)
    km = km_ref[...]
    bs = jnp.zeros((SUBLANES, LANES), F32)
    for qp in _split3(qbd):
        for kp in _split3(km):
            bs = bs + _dot_nt(qp, kp)
    lane = lax.broadcasted_iota(jnp.int32, (SUBLANES, LANES), 1)
    bs = jnp.where(lane < n_blk, bs, -jnp.inf)
    out = jnp.zeros((SUBLANES, LANES), jnp.int32)
    for t in range(MOBA_TOPK):
        mx = jnp.max(bs, axis=1, keepdims=True)
        idx = jnp.min(jnp.where(bs == mx, lane, LANES), axis=1, keepdims=True)
        out = jnp.where(lane == t, idx, out)
        bs = jnp.where(lane == idx, -jnp.inf, bs)
    sel_ref[...] = out


def _moba_topk(q, km, n_blk):
    db = q.shape[0]
    heads = MOBA_HEADS
    w = heads * HEAD_DIM
    assert MOBA_TOPK <= n_blk <= LANES
    return pl.pallas_call(
        functools.partial(_moba_topk_kernel, heads=heads, n_blk=n_blk),
        grid=(db,),
        in_specs=[pl.BlockSpec((None, 1, w), lambda b: (b, 0, 0)),
                  pl.BlockSpec((None, LANES, w), lambda b: (b, 0, 0))],
        out_specs=pl.BlockSpec((None, SUBLANES, LANES), lambda b: (b, 0, 0)),
        out_shape=jax.ShapeDtypeStruct((db, SUBLANES, LANES), jnp.int32),
        compiler_params=_params(("arbitrary",)),
        name="moba_topk",
    )(q, km)


def _moba_dec_kernel(sel_ref, pt_ref, q_ref, kn_ref, vn_ref, *refs, n_pages_sel):
    del sel_ref, pt_ref
    page_refs = refs[:n_pages_sel]
    o_ref = refs[n_pages_sel]
    q = q_ref[...]
    qb = jnp.broadcast_to(q, (SUBLANES, HEAD_DIM)).astype(BF16)
    s_self = jnp.sum(q * kn_ref[...], axis=1, keepdims=True)
    s = jnp.concatenate([_dot_nt(qb, page_refs[u][:, 0, :].astype(BF16)) for u in range(n_pages_sel)], axis=1)
    m = jnp.maximum(s_self, jnp.max(s, axis=1, keepdims=True))
    p_self = jnp.exp(s_self - m)
    p = jnp.exp(s - m).astype(BF16)
    l = p_self + jnp.sum(p.astype(F32), axis=1, keepdims=True)
    acc = p_self * vn_ref[...]
    for u in range(n_pages_sel):
        acc = acc + _dot(p[:, u * LANES:(u + 1) * LANES], page_refs[u][:, 1, :].astype(BF16))
    o_ref[...] = (acc * (1.0 / l))[0:1, :].astype(o_ref.dtype)


def _moba_decode(q, kv_new, sel, cache_rows, page_table, layer_page0):
    db, n_pages = page_table.shape
    heads = MOBA_HEADS
    pages_per_blk = MOBA_BLOCK // LANES
    n_pages_sel = MOBA_TOPK * pages_per_blk

    def page_map(u):
        jsel, pg = divmod(u, pages_per_blk)

        def f(b, h, sel_s, pt):
            blk = sel_s[(b * heads + h) * MOBA_TOPK + jsel]
            return (layer_page0 + pt[b * n_pages + blk * pages_per_blk + pg], h, 0, 0)
        return f

    one = lambda col0: pl.BlockSpec((None, 1, HEAD_DIM), lambda b, h, sel_s, pt: (b, 0, col0 + h))
    in_specs = [one(0), one(0), one(heads)]
    in_specs += [pl.BlockSpec((LANES, None, 2, HEAD_DIM), page_map(u)) for u in range(n_pages_sel)]
    kern = functools.partial(_moba_dec_kernel, n_pages_sel=n_pages_sel)
    return pl.pallas_call(
        kern,
        grid_spec=pltpu.PrefetchScalarGridSpec(
            num_scalar_prefetch=2, grid=(db, heads), in_specs=in_specs,
            out_specs=pl.BlockSpec((None, 1, HEAD_DIM), lambda b, h, sel_s, pt: (b, 0, h))),
        out_shape=jax.ShapeDtypeStruct((db, 1, heads * HEAD_DIM), BF16),
        compiler_params=_params(("arbitrary", "arbitrary")),
        name="moba_decode",
    )(sel.reshape(-1), page_table.reshape(-1), q, kv_new, kv_new, *([cache_rows] * n_pages_sel))


def _mem_dec_kernel(q_ref, kv_ref, o_ref, *, heads, n_mem):
    outs = []
    for h in range(heads):
        qb = jnp.broadcast_to(q_ref[:, _head_cols(h)], (SUBLANES, HEAD_DIM)).astype(BF16)
        k = kv_ref[pl.ds(h, n_mem, stride=2 * heads), :].astype(BF16)
        v = kv_ref[pl.ds(heads + h, n_mem, stride=2 * heads), :].astype(BF16)
        s = _dot_nt(qb, k)
        p = jnp.exp(s - jnp.max(s, axis=1, keepdims=True))
        l = jnp.sum(p, axis=1, keepdims=True)
        outs.append((_dot(p.astype(BF16), v) * (1.0 / l))[0:1, :])
    o_ref[...] = jnp.concatenate(outs, axis=1).astype(o_ref.dtype)


def _mem_decode(q, mem_rows, layer, *, q_col_block):
    db = q.shape[0]
    heads = MEM_HEADS
    n_mem = mem_rows.shape[1] // (2 * heads)
    return pl.pallas_call(
        functools.partial(_mem_dec_kernel, heads=heads, n_mem=n_mem),
        grid=(db,),
        in_specs=[pl.BlockSpec((None, 1, HEAD_TILE), lambda b: (b, 0, q_col_block)),
                  pl.BlockSpec((None, mem_rows.shape[1], HEAD_DIM), lambda b: (layer * db + b, 0, 0))],
        out_specs=pl.BlockSpec((None, 1, heads * HEAD_DIM), lambda b: (b, 0, 0)),
        out_shape=jax.ShapeDtypeStruct((db, 1, heads * HEAD_DIM), BF16),
        compiler_params=_params(("arbitrary",)),
        name="mem_decode",
    )(q, mem_rows)


def _rope_tables(pos):
    half = HEAD_DIM // 2
    inv_freq = ROPE_THETA ** (-jnp.arange(half, dtype=F32) / half)
    ang = pos.astype(F32)[:, None] * inv_freq[None, :]
    cos, sin = jnp.cos(ang), jnp.sin(ang)
    return jnp.concatenate([cos, cos], axis=-1), jnp.concatenate([-sin, sin], axis=-1)


def _pad_cols(a, n):
    return jnp.pad(a, ((0, 0), (0, n - a.shape[1])))


def _prep_w_in(w_in, d):
    mw, fw, cw = MOBA_HEADS * HEAD_DIM, FOX_HEADS * HEAD_DIM, MEM_HEADS * HEAD_DIM
    o_fl = 3 * mw + 3 * fw
    o_qc = o_fl + FOX_HEADS
    o_gl = o_qc + cw
    layers, k, n = w_in.shape
    assert n == o_gl + 3 * d and cw % W_IN_COLS == 0 and (3 * d) % W_IN_COLS == 0 and o_fl % W_IN_COLS == 0
    w_t = w_in.reshape(layers, k // LANES, LANES, n).transpose(3, 1, 0, 2)
    starts = list(range(0, o_fl, W_IN_COLS)) + list(range(o_qc, o_gl, W_IN_COLS))
    starts += [o_gl] * ((HEAD_TILE - cw) // W_IN_COLS)
    starts += list(range(o_gl, n, W_IN_COLS))
    w_main = _w_in_layout(w_t, starts, W_IN_COLS)
    w_fl = _w_in_layout(w_t, [o_fl], LANES)
    return w_main, w_fl


def _prep_layer(l, g_attn, b_f, g_mem, g_ffn, w_conv, b_conv, fp):
    f = w_conv.shape[2]
    return dict(
        g_attn=g_attn[l][None], b_f=_pad_cols(b_f[l][None], LANES), g_mem=g_mem[l][None], g_ffn=g_ffn[l][None],
        w_conv=jnp.pad(w_conv[l], ((0, SUBLANES - CONV_W), (0, fp - f))), b_conv=_pad_cols(b_conv[l][None], fp))


def kernel(x_prompt, x_sample, cache_moba_kv, cache_fox_kv, cache_fox_logf, cache_mem_kv, state_ffn_conv,
           page_table, mem_prompt, g_attn, w_in, b_f, w_br_moba, w_br_fox, w_br_mem, w_out, g_mem,
           w_mem_kv, g_ffn, w_up, w_conv, b_conv, w_down, g_final):
    bsz, seq, d = x_prompt.shape
    db = x_sample.shape[0]
    depth, n_phys, page = cache_moba_kv.shape[:3]
    n_pages = page_table.shape[1]
    past_len = n_pages * page
    n_mem = mem_prompt.shape[1]
    mw, fw, cw = MOBA_HEADS * HEAD_DIM, FOX_HEADS * HEAD_DIM, MEM_HEADS * HEAD_DIM
    assert page == LANES and db == SUBLANES and x_sample.shape[1] == 1
    ff_tile = 512
    tm = max(t for t in (1024, 512, 256) if seq % t == 0)
    tq = MOBA_BLOCK

    cos_p, sin_p = _rope_tables(jnp.arange(seq))
    cos_s, sin_s = _rope_tables(jnp.full((db,), past_len))
    gf = g_final[None]

    moba_t = jnp.transpose(cache_moba_kv, (0, 1, 2, 4, 3, 5))
    moba_pages = moba_t.reshape(depth * n_phys, page * 2 * MOBA_HEADS, HEAD_DIM)
    moba_rows = moba_t.reshape(depth * n_phys * page, MOBA_HEADS, 2, HEAD_DIM)
    fox_pages = jnp.transpose(cache_fox_kv, (0, 1, 2, 4, 3, 5)).reshape(depth * n_phys, page * 2 * FOX_HEADS, HEAD_DIM)
    lf_cache = jnp.transpose(cache_fox_logf, (0, 3, 1, 2))
    mem_rows = cache_mem_kv.reshape(depth * db, n_mem * 2 * MEM_HEADS, HEAD_DIM)

    f = w_down.shape[1]
    fp = -(-f // ff_tile) * ff_tile
    w_main, w_fl = _prep_w_in(w_in, d)
    wm, wf, wc = _cast_bf16(w_br_moba, mw), _cast_bf16(w_br_fox, fw), _cast_bf16(w_br_mem, cw)
    wo, w_mem = _cast_bf16(w_out, 512), _cast_bf16(w_mem_kv, 512)
    wa, wb = _cast_w_up(w_up, fp, 256)
    wdn = _cast_bf16(w_down, f // SUBLANES)
    moba_out = jnp.zeros((depth, bsz, seq, MOBA_HEADS, 2, HEAD_DIM), F32)
    fox_out = jnp.zeros((depth, bsz, seq, FOX_HEADS, 2, HEAD_DIM), F32)
    xp = x_prompt.reshape(bsz * seq, d)
    xs = x_sample.reshape(db, d)
    outs = {k: [] for k in ("p_logf", "p_mem", "p_conv", "s_moba", "s_fox", "s_logf", "s_conv")}
    yp = ys = None
    for l in range(depth):
        w = _prep_layer(l, g_attn, b_f, g_mem, g_ffn, w_conv, b_conv, fp)
        g_last = gf if l == depth - 1 else None

        mkv = _norm_mm(mem_prompt.reshape(bsz * n_mem, d), w["g_mem"], w_mem, layer=l, tm=min(512, bsz * n_mem))
        q_all, moba_kv, fox_kv, gates, logf, c = _in_proj(
            xp, w["g_attn"], w_main, w_fl, w["b_f"], cos_p, sin_p, layer=l, tm=tm, seq_len=seq, q_dtype=BF16)
        q3 = q_all.reshape(bsz, seq, 3 * HEAD_TILE)
        c3 = c.reshape(bsz, seq, LANES)
        o_m, moba_out = _causal_attention(q3, moba_kv.reshape(bsz, seq, 2 * mw), mode="moba", heads=MOBA_HEADS,
                                          q_col0=0, layer=l, kv_out=moba_out)
        o_f, fox_out = _causal_attention(q3, fox_kv.reshape(bsz, seq, 2 * fw), mode="fox", heads=FOX_HEADS,
                                         q_col0=MOBA_HEADS, layer=l, kv_out=fox_out, c=c3)
        o_c = _mem_attention(q3, mkv.reshape(bsz, n_mem, 2 * cw), heads=MEM_HEADS, q_col_block=2, tq=tq)
        xp = _merge_out(o_m.reshape(bsz * seq, mw), o_f.reshape(bsz * seq, fw), o_c.reshape(bsz * seq, cw),
                        gates, xp, wm, wf, wc, wo, layer=l, tm=256)
        gact, tails, km = _ffn_up(xp, w["g_ffn"], wa, wb, w["w_conv"], w["b_conv"], layer=l, tm=tm, tn=ff_tile,
                                  seq_len=seq, kmean_src=(moba_pages, page_table, l * n_phys))
        xp, yp = _ffn_down(gact, wdn, xp, g_last, layer=l, tm=256)
        outs["p_logf"].append(logf[:, :FOX_HEADS].reshape(bsz, seq, FOX_HEADS))
        outs["p_mem"].append(mkv.reshape(bsz, n_mem, 2, MEM_HEADS, HEAD_DIM))
        tiles_per_seq = seq // tm
        outs["p_conv"].append(tails[tiles_per_seq - 1::tiles_per_seq, SUBLANES - (CONV_W - 1):, :f])

        q_s, moba_s, fox_s, gates_s, logf_s, _ = _in_proj(
            xs, w["g_attn"], w_main, w_fl, w["b_f"], cos_s, sin_s, layer=l, tm=db, seq_len=1, q_dtype=F32)
        q_s3 = q_s.reshape(db, 1, 3 * HEAD_TILE)
        sel = _moba_topk(q_s3, km, past_len // MOBA_BLOCK)
        o_ms = _moba_decode(q_s3, moba_s.reshape(db, 1, 2 * mw), sel[:, :MOBA_HEADS, :MOBA_TOPK],
                            moba_rows, page_table, l * n_phys)
        o_fs = _fox_decode(q_s3[:, :, mw:mw + fw], fox_s.reshape(db, 1, 2 * fw),
                           logf_s[:, :SUBLANES].reshape(db, SUBLANES, 1), fox_pages, lf_cache, page_table,
                           l, pages_per_step=16)
        o_cs = _mem_decode(q_s3, mem_rows, l, q_col_block=2)
        xs = _merge_out(o_ms.reshape(db, mw), o_fs.reshape(db, fw), o_cs.reshape(db, cw), gates_s, xs,
                        wm, wf, wc, wo, layer=l, tm=db)
        prev = jnp.pad(state_ffn_conv[l], ((0, 0), (0, 0), (0, fp - f)))
        gact_s, a_s = _ffn_up(xs, w["g_ffn"], wa, wb, w["w_conv"], w["b_conv"],
                              layer=l, tm=db, tn=ff_tile, seq_len=1, prev=(prev[:, 0], prev[:, 1]))
        xs, ys = _ffn_down(gact_s, wdn, xs, g_last, layer=l, tm=db)
        outs["s_moba"].append(moba_s.reshape(db, 1, 2, MOBA_HEADS, HEAD_DIM))
        outs["s_fox"].append(fox_s.reshape(db, 1, 2, FOX_HEADS, HEAD_DIM))
        outs["s_logf"].append(logf_s[:, :FOX_HEADS].reshape(db, 1, FOX_HEADS))
        outs["s_conv"].append(jnp.stack([state_ffn_conv[l][:, 1], a_s[:, :f]], axis=1))

    st = lambda k: jnp.stack(outs[k])
    kv_order = (0, 1, 2, 4, 3, 5)
    return (yp.reshape(bsz, seq, d), ys.reshape(db, 1, d),
            jnp.transpose(moba_out, kv_order), jnp.transpose(fox_out, kv_order),
            st("p_logf"), st("p_mem"), st("p_conv"),
            st("s_moba"), st("s_fox"), st("s_logf"), st("s_conv"))
```

```python
import functools

import jax
import jax.numpy as jnp
from jax import lax
from jax.experimental import pallas as pl
from jax.experimental.pallas import tpu as pltpu

F32 = jnp.float32
BF16 = jnp.bfloat16

HEAD_DIM = 128
MOBA_HEADS = 6
FOX_HEADS = 6
MEM_HEADS = 4
MOBA_BLOCK = 256
MOBA_TOPK = 3
CONV_W = 3
ROPE_THETA = 10000.0
NORM_EPS = 1e-6
NEG = -1e30

LANES = 128
SUBLANES = 8
HEAD_TILE = MOBA_HEADS * HEAD_DIM
VMEM_LIMIT = 56 * 1024 * 1024
ROW_CHUNK = 256
SCALE = HEAD_DIM ** -0.5
LOG2E = 1.4426950408889634

_NT = (((1,), (1,)), ((), ()))


def _params(sem):
    return pltpu.CompilerParams(dimension_semantics=sem, vmem_limit_bytes=VMEM_LIMIT)


def _rms(x, g):
    return x * lax.rsqrt(jnp.mean(x * x, axis=-1, keepdims=True) + NORM_EPS) * g


def _log_sigmoid(x):
    return jnp.minimum(x, 0.0) - jnp.log1p(jnp.exp(-jnp.abs(x)))


def _split3(v):
    hi = v.astype(BF16)
    r1 = v - hi.astype(F32)
    mid = r1.astype(BF16)
    lo = (r1 - mid.astype(F32)).astype(BF16)
    return hi, mid, lo


def _dot(a, b):
    return jnp.dot(a, b, preferred_element_type=F32)


def _dot_nt(a, b):
    return lax.dot_general(a, b, _NT, preferred_element_type=F32)


def _cumsum_rows(v, carry, blk=256):
    tm = v.shape[0]
    blk = min(blk, tm)
    r = lax.broadcasted_iota(jnp.int32, (blk, blk), 0)
    c = lax.broadcasted_iota(jnp.int32, (blk, blk), 1)
    tri = (c <= r).astype(BF16)
    outs = []
    for s in range(0, tm, blk):
        hi, mid, lo = _split3(v[s:s + blk])
        cs = _dot(tri, hi) + _dot(tri, mid) + _dot(tri, lo) + carry
        outs.append(cs)
        carry = cs[blk - 1:blk, :]
    return jnp.concatenate(outs, axis=0), carry


def _rope(acc, cos, sin_signed):
    outs = []
    for hh in range(acc.shape[1] // HEAD_DIM):
        xh = acc[:, hh * HEAD_DIM:(hh + 1) * HEAD_DIM]
        outs.append(xh * cos + pltpu.roll(xh, HEAD_DIM // 2, 1) * sin_signed)
    return jnp.concatenate(outs, axis=1)


def _head_cols(h):
    return slice(h * HEAD_DIM, (h + 1) * HEAD_DIM)


W_IN_COLS = 256


def _w_in_kernel(starts_ref, x_ref, o_ref):
    del starts_ref
    cols, kt, layers, _ = x_ref.shape
    for l in range(layers):
        for t in range(kt):
            o_ref[l, t * LANES:(t + 1) * LANES, :] = x_ref[:, t, l, :].T.astype(o_ref.dtype)


def _w_in_layout(w_t, starts, cols):
    _, kt, layers, _ = w_t.shape
    starts = jnp.asarray(starts, jnp.int32)
    return pl.pallas_call(
        _w_in_kernel,
        grid_spec=pltpu.PrefetchScalarGridSpec(
            num_scalar_prefetch=1, grid=(starts.shape[0],),
            in_specs=[pl.BlockSpec((pl.Element(cols), pl.Element(kt), pl.Element(layers), pl.Element(LANES)),
                                   lambda j, st: (st[j], 0, 0, 0))],
            out_specs=pl.BlockSpec((layers, kt * LANES, cols), lambda j, st: (0, 0, j))),
        out_shape=jax.ShapeDtypeStruct((layers, kt * LANES, starts.shape[0] * cols), BF16),
        compiler_params=_params(("arbitrary",)),
        name="w_in_layout",
    )(starts, w_t)


def _cast_kernel(x_ref, o_ref):
    o_ref[...] = x_ref[...].astype(o_ref.dtype)


def _cast_bf16(w, rows):
    layers, r, c = w.shape
    rows = min(rows, r)
    assert r % rows == 0
    spec = pl.BlockSpec((None, rows, c), lambda l, i: (l, i, 0))
    return pl.pallas_call(
        _cast_kernel, grid=(layers, r // rows), in_specs=[spec], out_specs=spec,
        out_shape=jax.ShapeDtypeStruct(w.shape, BF16), compiler_params=_params(("arbitrary", "arbitrary")),
        name="cast_bf16")(w)


def _cast_up_kernel(x_ref, a_ref, b_ref, *, f):
    for o_ref, c0 in ((a_ref, 0), (b_ref, f)):
        o_ref[:, :f] = x_ref[:, c0:c0 + f].astype(o_ref.dtype)
        o_ref[:, f:] = jnp.zeros((o_ref.shape[0], o_ref.shape[1] - f), o_ref.dtype)


def _cast_w_up(w_up, fp, rows):
    layers, d, f2 = w_up.shape
    out = pl.BlockSpec((None, rows, fp), lambda l, i: (l, i, 0))
    return pl.pallas_call(
        functools.partial(_cast_up_kernel, f=f2 // 2), grid=(layers, d // rows),
        in_specs=[pl.BlockSpec((None, rows, f2), lambda l, i: (l, i, 0))], out_specs=[out, out],
        out_shape=[jax.ShapeDtypeStruct((layers, d, fp), BF16)] * 2,
        compiler_params=_params(("arbitrary", "arbitrary")), name="cast_w_up")(w_up)


def _in_proj_kernel(x_ref, g_ref, w_ref, wfl_ref, bf_ref, cos_ref, sin_ref,
                    q_ref, mkv_ref, fkv_ref, gate_ref, logf_ref, c_ref,
                    h_scr, carry_scr, *, tiles_per_seq, cumsum, q_scale):
    i = pl.program_id(0)
    j = pl.program_id(1)

    @pl.when(j == 0)
    def _():
        h = _rms(x_ref[...], g_ref[...]).astype(BF16)
        h_scr[...] = h
        logf = _log_sigmoid(_dot(h, wfl_ref[...]) + bf_ref[...])
        logf_ref[...] = logf
        if cumsum:
            @pl.when(i % tiles_per_seq == 0)
            def _():
                carry_scr[...] = jnp.zeros_like(carry_scr)
            c, carry = _cumsum_rows(logf, carry_scr[...])
            c_ref[...] = c
            carry_scr[...] = carry
        else:
            c_ref[...] = logf

    tm = x_ref.shape[0]
    rc = min(ROW_CHUNK, tm)

    def tile(out_ref, epilogue):
        for r0 in range(0, tm, rc):
            rows = slice(r0, r0 + rc)
            acc = _dot(h_scr[rows, :], w_ref[...])
            out_ref[rows, :] = epilogue(acc, rows).astype(out_ref.dtype)

    rope = lambda acc, rows: _rope(acc, cos_ref[rows, :], sin_ref[rows, :])

    @pl.when(j == 0)
    def _():
        tile(q_ref, lambda acc, rows: rope(acc, rows) * q_scale)

    @pl.when(j == 1)
    def _():
        tile(mkv_ref, rope)

    @pl.when(j == 2)
    def _():
        tile(mkv_ref, lambda acc, rows: acc)

    @pl.when(jnp.logical_or(j == 3, j == 6))
    def _():
        tile(q_ref, lambda acc, rows: acc * q_scale)

    @pl.when(jnp.logical_or(j == 4, j == 5))
    def _():
        tile(fkv_ref, lambda acc, rows: acc)

    @pl.when(j >= 7)
    def _():
        tile(gate_ref, lambda acc, rows: jax.nn.sigmoid(acc))


def _in_proj(x, g, w_main, w_fl, b_f, cos, sin, *, layer, tm, seq_len, q_dtype, q_scale):
    m, d = x.shape
    n_tiles = w_main.shape[2] // HEAD_TILE
    n_gate = n_tiles - 7
    table_tiles = cos.shape[0] // tm
    cumsum = seq_len > 1
    tiles_per_seq = max(seq_len // tm, 1)
    kern = functools.partial(_in_proj_kernel, tiles_per_seq=tiles_per_seq, cumsum=cumsum, q_scale=q_scale)
    row = lambda i, j: (i, 0)
    return pl.pallas_call(
        kern,
        grid=(m // tm, n_tiles),
        in_specs=[
            pl.BlockSpec((tm, d), row),
            pl.BlockSpec((1, d), lambda i, j: (0, 0)),
            pl.BlockSpec((None, d, HEAD_TILE), lambda i, j: (layer, 0, j)),
            pl.BlockSpec((None, d, LANES), lambda i, j: (layer, 0, 0)),
            pl.BlockSpec((1, LANES), lambda i, j: (0, 0)),
            pl.BlockSpec((tm, LANES), lambda i, j: (i % table_tiles, 0)),
            pl.BlockSpec((tm, LANES), lambda i, j: (i % table_tiles, 0)),
        ],
        out_specs=[
            pl.BlockSpec((tm, HEAD_TILE), lambda i, j: (i, jnp.minimum(j // 3, 2))),
            pl.BlockSpec((tm, HEAD_TILE), lambda i, j: (i, jnp.clip(j - 1, 0, 1))),
            pl.BlockSpec((tm, HEAD_TILE), lambda i, j: (i, jnp.clip(j - 4, 0, 1))),
            pl.BlockSpec((tm, HEAD_TILE), lambda i, j: (i, jnp.clip(j - 7, 0, n_gate - 1))),
            pl.BlockSpec((tm, LANES), row),
            pl.BlockSpec((tm, LANES), row),
        ],
        out_shape=[
            jax.ShapeDtypeStruct((m, 3 * HEAD_TILE), q_dtype),
            jax.ShapeDtypeStruct((m, 2 * HEAD_TILE), F32),
            jax.ShapeDtypeStruct((m, 2 * HEAD_TILE), F32),
            jax.ShapeDtypeStruct((m, n_gate * HEAD_TILE), BF16),
            jax.ShapeDtypeStruct((m, LANES), F32),
            jax.ShapeDtypeStruct((m, LANES), F32),
        ],
        scratch_shapes=[pltpu.VMEM((tm, d), BF16), pltpu.VMEM((1, LANES), F32)],
        compiler_params=_params(("arbitrary", "arbitrary")),
        name="in_proj",
    )(x, g, w_main, w_fl, b_f, cos, sin)


def _norm_mm_kernel(x_ref, g_ref, w_ref, o_ref):
    h = _rms(x_ref[...], g_ref[...]).astype(BF16)
    o_ref[...] = _dot(h, w_ref[...])


def _norm_mm(x, g, w, *, layer, tm):
    m, d = x.shape
    n = w.shape[2]
    return pl.pallas_call(
        _norm_mm_kernel,
        grid=(m // tm,),
        in_specs=[pl.BlockSpec((tm, d), lambda i: (i, 0)),
                  pl.BlockSpec((1, d), lambda i: (0, 0)),
                  pl.BlockSpec((None, d, n), lambda i: (layer, 0, 0))],
        out_specs=pl.BlockSpec((tm, n), lambda i: (i, 0)),
        out_shape=jax.ShapeDtypeStruct((m, n), F32),
        compiler_params=_params(("arbitrary",)),
        name="mem_kv_proj",
    )(x, g, w)


MASK_BIAS = 1e30


def _causal_attn_kernel(*refs, mode, blk, n_blk):
    q_ref, k_ref, v_ref = refs[:3]
    c_ref = refs[3] if mode == "fox" else None
    o_ref, kv_out_ref = refs[-2:]
    h = pl.program_id(1)
    s_len = q_ref.shape[0]
    q = q_ref[...]
    kf = k_ref[...]
    vf = v_ref[...]
    kv_out_ref[:, 0, :] = kf
    kv_out_ref[:, 1, :] = vf
    vb = jnp.concatenate([vf.astype(BF16), jnp.ones((s_len, HEAD_DIM), BF16)], axis=1)
    lane = lax.broadcasted_iota(jnp.int32, (s_len, LANES), 1)

    if mode == "fox":
        c_col = jnp.sum(jnp.where(lane == h, c_ref[...], 0.0), axis=1, keepdims=True) * LOG2E
        hi, mid, lo = [part.astype(F32) for part in _split3(c_col)]
        aug_q = jnp.where(lane == 0, hi, jnp.where(lane == 1, mid, jnp.where(lane == 2, lo,
                          jnp.where(lane < 6, 1.0, 0.0))))
        aug_k = jnp.where(lane < 3, 1.0, jnp.where(lane == 3, -hi, jnp.where(lane == 4, -mid,
                          jnp.where(lane == 5, -lo, 0.0))))
    else:
        nb8 = -(-n_blk // SUBLANES) * SUBLANES
        km = [jnp.sum(kf[n * blk:(n + 1) * blk], axis=0, keepdims=True) * (1.0 / blk) for n in range(n_blk)]
        km = jnp.concatenate(km + [jnp.zeros((1, HEAD_DIM), F32)] * (nb8 - n_blk), axis=0)
        bs = sum(_dot_nt(part, q) for part in _split3(km))
        blk_id = lax.broadcasted_iota(jnp.int32, (nb8, s_len), 0)
        own = lax.broadcasted_iota(jnp.int32, (nb8, s_len), 1) // blk
        cnt = jnp.zeros((nb8, s_len), F32)
        for mth in range(n_blk - 1):
            sm = bs[mth:mth + 1, :]
            beats = jnp.logical_or(sm > bs, jnp.logical_and(sm == bs, mth < blk_id))
            cnt = cnt + jnp.where(jnp.logical_and(beats, mth < own), 1.0, 0.0)
        sel = jnp.where(jnp.logical_and(blk_id < own, cnt < MOBA_TOPK), 1.0, 0.0)
        sel = jnp.concatenate([sel, jnp.zeros((LANES - nb8, s_len), F32)], axis=0).astype(BF16)
        eye = (lax.broadcasted_iota(jnp.int32, (blk, blk), 0)
               == lax.broadcasted_iota(jnp.int32, (blk, blk), 1)).astype(BF16)
        selc = jnp.concatenate([_dot_nt(eye, sel[:, t * blk:(t + 1) * blk]) for t in range(n_blk)], axis=0)
        row_blk = lax.broadcasted_iota(jnp.int32, (s_len, LANES), 0) // blk
        aug_q = jnp.where(lane == row_blk, 0.0, (selc - 1.0) * MASK_BIAS)
        aug_k = jnp.where(lane == row_blk, 1.0, 0.0)

    q_aug = jnp.concatenate([q, aug_q.astype(BF16)], axis=1)
    k_aug = jnp.concatenate([kf.astype(BF16), aug_k.astype(BF16)], axis=1)
    tq = 2 * blk if s_len % (2 * blk) == 0 else blk
    row = lax.broadcasted_iota(jnp.int32, (tq, tq), 0)
    col = lax.broadcasted_iota(jnp.int32, (tq, tq), 1)
    for t in range(s_len // tq):
        n = (t + 1) * tq
        s = _dot_nt(q_aug[t * tq:(t + 1) * tq], k_aug[:n])
        diag = jnp.where(col <= row, s[:, n - tq:], NEG)
        s = diag if t == 0 else jnp.concatenate([s[:, :n - tq], diag], axis=1)
        p = jnp.exp2((s - jnp.max(s, axis=1, keepdims=True)).astype(BF16))
        ol = _dot(p, vb[:n])
        o_ref[t * tq:(t + 1) * tq, :] = (ol[:, :HEAD_DIM] * (1.0 / ol[:, HEAD_DIM:])).astype(o_ref.dtype)


def _causal_attention(q, kv, *, mode, heads, q_col0, layer, kv_out, c=None):
    b, s, _ = q.shape
    blk = MOBA_BLOCK
    n_blk = s // blk
    assert s % blk == 0 and n_blk <= LANES and kv.shape[1] == s
    seq_head = lambda col0: pl.BlockSpec((None, s, HEAD_DIM), lambda bi, h: (bi, 0, col0 + h))
    in_specs = [seq_head(q_col0), seq_head(0), seq_head(heads)]
    args = [q, kv, kv]
    if mode == "fox":
        in_specs += [pl.BlockSpec((None, s, LANES), lambda bi, h: (bi, 0, 0))]
        args += [c]
    aliases = {len(args): 1}
    in_specs += [pl.BlockSpec(memory_space=pl.ANY)]
    args += [kv_out]
    return pl.pallas_call(
        functools.partial(_causal_attn_kernel, mode=mode, blk=blk, n_blk=n_blk),
        grid=(b, heads),
        in_specs=in_specs,
        out_specs=[seq_head(0),
                   pl.BlockSpec((None, None, s, None, 2, HEAD_DIM), lambda bi, h: (layer, bi, 0, h, 0, 0))],
        out_shape=[jax.ShapeDtypeStruct((b, s, heads * HEAD_DIM), BF16),
                   jax.ShapeDtypeStruct(kv_out.shape, kv_out.dtype)],
        input_output_aliases=aliases,
        compiler_params=_params(("arbitrary", "arbitrary")),
        name="attn_" + mode,
    )(*args)


def _mem_attn_kernel(q_ref, kv_ref, o_ref, kb, vb, *, heads):
    w = heads * HEAD_DIM

    @pl.when(pl.program_id(1) == 0)
    def _():
        kb[...] = kv_ref[:, :w].astype(BF16)
        vb[...] = kv_ref[:, w:].astype(BF16)

    for h in range(heads):
        s = _dot_nt(q_ref[:, _head_cols(h)], kb[:, _head_cols(h)])
        p = jnp.exp2(s - jnp.max(s, axis=1, keepdims=True))
        l = jnp.sum(p, axis=1, keepdims=True)
        o_ref[:, _head_cols(h)] = (_dot(p.astype(BF16), vb[:, _head_cols(h)]) * (1.0 / l)).astype(o_ref.dtype)


def _mem_attention(q, kv, *, heads, q_col_block, tq):
    b, s, _ = q.shape
    n_mem = kv.shape[1]
    w = heads * HEAD_DIM
    return pl.pallas_call(
        functools.partial(_mem_attn_kernel, heads=heads),
        grid=(b, s // tq),
        in_specs=[pl.BlockSpec((None, tq, HEAD_TILE), lambda bi, qi: (bi, qi, q_col_block)),
                  pl.BlockSpec((None, n_mem, 2 * w), lambda bi, qi: (bi, 0, 0))],
        out_specs=pl.BlockSpec((None, tq, w), lambda bi, qi: (bi, qi, 0)),
        out_shape=jax.ShapeDtypeStruct((b, s, w), BF16),
        scratch_shapes=[pltpu.VMEM((n_mem, w), BF16), pltpu.VMEM((n_mem, w), BF16)],
        compiler_params=_params(("arbitrary", "arbitrary")),
        name="attn_mem",
    )(q, kv)


def _merge_kernel(om_ref, of_ref, oc_ref, gate_ref, x_ref, wm_ref, wf_ref, wc_ref, wo_ref, o_ref):
    d = x_ref.shape[1]
    merged = gate_ref[:, 0:d].astype(F32) * _dot(om_ref[...], wm_ref[...])
    merged = merged + gate_ref[:, d:2 * d].astype(F32) * _dot(of_ref[...], wf_ref[...])
    merged = merged + gate_ref[:, 2 * d:3 * d].astype(F32) * _dot(oc_ref[...], wc_ref[...])
    o_ref[...] = x_ref[...] + _dot(merged.astype(BF16), wo_ref[...])


def _resident(w, layer):
    return pl.BlockSpec((None,) + w.shape[1:], lambda i: (layer, 0, 0), pipeline_mode=pl.Buffered(1))


def _merge_out(om, of, oc, gates, x, wm, wf, wc, wo, *, layer, tm):
    m, d = x.shape
    rows = lambda w: pl.BlockSpec((tm, w), lambda i: (i, 0))
    return pl.pallas_call(
        _merge_kernel,
        grid=(m // tm,),
        in_specs=[rows(om.shape[1]), rows(of.shape[1]), rows(oc.shape[1]), rows(gates.shape[1]), rows(d),
                  _resident(wm, layer), _resident(wf, layer), _resident(wc, layer), _resident(wo, layer)],
        out_specs=rows(d),
        out_shape=jax.ShapeDtypeStruct((m, d), F32),
        compiler_params=_params(("arbitrary",)),
        name="merge_out",
    )(om, of, oc, gates, x, wm, wf, wc, wo)


def _block_means(page_refs, km_out, t, *, chunks_per_seq, n_seq, heads, every_step):
    pages_per_blk = MOBA_BLOCK // LANES
    blks = len(page_refs) // pages_per_blk
    c = t % chunks_per_seq

    def body():
        used = chunks_per_seq * blks
        if used < km_out.shape[0]:
            km_out[used:, :] = jnp.zeros((km_out.shape[0] - used, km_out.shape[1]), F32)
        for bb in range(blks):
            per_head = []
            for h in range(heads):
                tot = jnp.zeros((1, HEAD_DIM), F32)
                for pp in range(pages_per_blk):
                    keys = page_refs[bb * pages_per_blk + pp][pl.ds(2 * h, LANES, stride=2 * heads), :]
                    tot = tot + jnp.sum(keys, axis=0, keepdims=True)
                per_head.append(tot)
            km_out[pl.ds(c * blks + bb, 1), :] = jnp.concatenate(per_head, axis=1) * (1.0 / MOBA_BLOCK)

    if every_step:
        body()
    else:
        pl.when(t < n_seq * chunks_per_seq)(body)


def _ffn_up_kernel(*refs, tiles_per_seq, per_row_state, kmean=None):
    if per_row_state:
        x_ref, g_ref, wa_ref, wb_ref, wc_ref, bc_ref, p0_ref, p1_ref, g_out, a_out, h_scr = refs
    elif kmean is None:
        x_ref, g_ref, wa_ref, wb_ref, wc_ref, bc_ref, g_out, tail_out, h_scr, carry_scr = refs
    else:
        n_pg = kmean["pages_per_step"]
        x_ref, g_ref, wa_ref, wb_ref, wc_ref, bc_ref = refs[1:7]
        page_refs = refs[7:7 + n_pg]
        g_out, tail_out, km_out, h_scr, carry_scr = refs[7 + n_pg:]
    i = pl.program_id(0)
    j = pl.program_id(1)

    @pl.when(j == 0)
    def _():
        h_scr[...] = _rms(x_ref[...], g_ref[...]).astype(BF16)

    w = wc_ref[...]
    tm = x_ref.shape[0]
    rc = min(ROW_CHUNK, tm)
    if not per_row_state:
        @pl.when(i % tiles_per_seq == 0)
        def _():
            carry_scr[j] = jnp.zeros(carry_scr.shape[1:], F32)
        prev = carry_scr[j]
    if kmean is not None:
        _block_means(page_refs, km_out, i * pl.num_programs(1) + j, chunks_per_seq=kmean["chunks_per_seq"],
                     n_seq=kmean["n_seq"], heads=kmean["heads"], every_step=kmean["every_step"])
    for r0 in range(0, tm, rc):
        rows = slice(r0, r0 + rc)
        h = h_scr[rows, :]
        a = _dot(h, wa_ref[...])
        b = _dot(h, wb_ref[...])
        if per_row_state:
            a1 = p1_ref[rows, :]
            a2 = p0_ref[rows, :]
            a_out[rows, :] = a
        else:
            row = lax.broadcasted_iota(jnp.int32, a.shape, 0)
            a1 = jnp.where(row == 0, prev[7:8, :], pltpu.roll(a, 1, 0))
            a2 = jnp.where(row == 0, prev[6:7, :], jnp.where(row == 1, prev[7:8, :], pltpu.roll(a, 2, 0)))
            prev = a[rc - SUBLANES:, :]
        a_conv = bc_ref[...] + a2 * w[0:1, :] + a1 * w[1:2, :] + a * w[2:3, :]
        g_out[rows, :] = (a_conv * jax.nn.sigmoid(a_conv) * b).astype(g_out.dtype)
    if not per_row_state:
        carry_scr[j] = prev
        tail_out[...] = prev


def _ffn_up(x, g, wa, wb, wconv, bconv, *, layer, tm, tn, seq_len, prev=None, kmean_src=None):
    m, d = x.shape
    f = wa.shape[2]
    nj = f // tn
    per_row_state = prev is not None
    tiles_per_seq = max(seq_len // tm, 1)
    col = lambda r: pl.BlockSpec((r, tn), lambda i, j, *_: (0, j))
    wcol = pl.BlockSpec((None, d, tn), lambda i, j, *_: (layer, 0, j))
    in_specs = [pl.BlockSpec((tm, d), lambda i, j, *_: (i, 0)),
                pl.BlockSpec((1, d), lambda i, j, *_: (0, 0)), wcol, wcol, col(SUBLANES), col(1)]
    args = [x, g, wa, wb, wconv, bconv]
    tile = pl.BlockSpec((tm, tn), lambda i, j, *_: (i, j))
    scratch = [pltpu.VMEM((tm, d), BF16)]
    kmean = None
    prefetch = []
    if per_row_state:
        in_specs += [tile, tile]
        args += [prev[0], prev[1]]
        out_specs = [tile, tile]
        out_shape = [jax.ShapeDtypeStruct((m, f), BF16), jax.ShapeDtypeStruct((m, f), F32)]
    else:
        out_specs = [tile, pl.BlockSpec((None, SUBLANES, tn), lambda i, j, *_: (i, 0, j))]
        out_shape = [jax.ShapeDtypeStruct((m, f), BF16), jax.ShapeDtypeStruct((m // tm, SUBLANES, f), F32)]
        scratch += [pltpu.VMEM((nj, SUBLANES, tn), F32)]
        if kmean_src is not None:
            cache_pages, page_table, layer_page0 = kmean_src
            n_seq, n_pages = page_table.shape
            pages_per_blk = MOBA_BLOCK // LANES
            steps = (m // tm) * nj
            chunks_per_seq = steps // n_seq
            assert chunks_per_seq >= 1
            pages_per_step = -(-n_pages // (chunks_per_seq * pages_per_blk)) * pages_per_blk
            chunks_per_seq = -(-n_pages // pages_per_step)
            assert chunks_per_seq * pages_per_step // pages_per_blk <= LANES
            kmean = dict(pages_per_step=pages_per_step, chunks_per_seq=chunks_per_seq, n_seq=n_seq, heads=MOBA_HEADS,
                         every_step=n_seq * chunks_per_seq == steps)
            seq_of = lambda i, j: jnp.minimum((i * nj + j) // chunks_per_seq, n_seq - 1)

            def page_map(u):
                def fn(i, j, pt):
                    pg = jnp.minimum(((i * nj + j) % chunks_per_seq) * pages_per_step + u, n_pages - 1)
                    return (layer_page0 + pt[seq_of(i, j) * n_pages + pg], 0, 0)
                return fn

            in_specs += [pl.BlockSpec((None,) + cache_pages.shape[1:], page_map(u)) for u in range(pages_per_step)]
            args += [cache_pages] * pages_per_step
            prefetch = [page_table.reshape(-1)]
            w_heads = MOBA_HEADS * HEAD_DIM
            out_specs += [pl.BlockSpec((None, LANES, w_heads), lambda i, j, pt: (seq_of(i, j), 0, 0))]
            out_shape += [jax.ShapeDtypeStruct((n_seq, LANES, w_heads), F32)]
    kern = functools.partial(_ffn_up_kernel, tiles_per_seq=tiles_per_seq, per_row_state=per_row_state, kmean=kmean)
    return pl.pallas_call(
        kern,
        grid_spec=pltpu.PrefetchScalarGridSpec(
            num_scalar_prefetch=len(prefetch), grid=(m // tm, nj), in_specs=in_specs, out_specs=out_specs,
            scratch_shapes=scratch),
        out_shape=out_shape, compiler_params=_params(("arbitrary", "arbitrary")), name="ffn_up",
    )(*prefetch, *args)


def _ffn_down_kernel(*refs, final):
    if final:
        g_ref, w_ref, x_ref, gf_ref, o_ref, y_ref = refs
    else:
        g_ref, w_ref, x_ref, o_ref = refs
    xo = x_ref[...] + _dot(g_ref[...], w_ref[...])
    o_ref[...] = xo
    if final:
        y_ref[...] = _rms(xo, gf_ref[...])


def _ffn_down(gact, w, x, g_final, *, layer, tm):
    m, d = x.shape
    f = w.shape[1]
    final = g_final is not None
    rows = lambda wd: pl.BlockSpec((tm, wd), lambda i: (i, 0))
    in_specs = [rows(f), _resident(w, layer), rows(d)]
    args = [gact, w, x]
    out_specs = [rows(d)]
    out_shape = [jax.ShapeDtypeStruct((m, d), F32)]
    if final:
        in_specs += [pl.BlockSpec((1, d), lambda i: (0, 0))]
        args += [g_final]
        out_specs += [rows(d)]
        out_shape += [jax.ShapeDtypeStruct((m, d), F32)]
    res = pl.pallas_call(
        functools.partial(_ffn_down_kernel, final=final),
        grid=(m // tm,), in_specs=in_specs, out_specs=out_specs, out_shape=out_shape,
        compiler_params=_params(("arbitrary",)), name="ffn_down",
    )(*args)
    return (res[0], res[1]) if final else (res[0], None)


def _head_mask(rows, width):
    r = lax.broadcasted_iota(jnp.int32, (rows, width), 0)
    c = lax.broadcasted_iota(jnp.int32, (rows, width), 1)
    return (c // HEAD_DIM) == r


def _page_heads(page_ref, kv, heads):
    return jnp.concatenate(
        [page_ref[pl.ds(2 * h + kv, LANES, stride=2 * heads), :] for h in range(heads)], axis=1)


def _fox_dec_kernel(pt_ref, q_ref, kvn_ref, lfn_ref, lf_ref, *refs, pages_per_step, heads, n_pages):
    page_refs = refs[:pages_per_step]
    o_ref, m_scr, l_scr, acc_scr, run_scr = refs[pages_per_step:]
    b = pl.program_id(0)
    c = pl.program_id(1)
    w = heads * HEAD_DIM
    hm = _head_mask(SUBLANES, w)
    qbd = jnp.where(hm, jnp.broadcast_to(q_ref[...], (SUBLANES, w)), 0.0)

    @pl.when(c == 0)
    def _():
        kvn = kvn_ref[...]
        m_scr[...] = jnp.sum(qbd * kvn[:, :w], axis=1, keepdims=True)
        l_scr[...] = jnp.ones_like(l_scr)
        acc_scr[...] = jnp.broadcast_to(kvn[:, w:], (SUBLANES, w))
        run_scr[...] = lfn_ref[...]

    r = lax.broadcasted_iota(jnp.int32, (LANES, LANES), 0)
    cc = lax.broadcasted_iota(jnp.int32, (LANES, LANES), 1)
    upper = (r > cc).astype(BF16)
    qb = qbd.astype(BF16)
    run = run_scr[...]
    pad = jnp.zeros((SUBLANES - heads, LANES), F32)
    ss = []
    for u in range(pages_per_step):
        pid = pt_ref[b * n_pages + (n_pages - 1 - (c * pages_per_step + u))]
        lf = jnp.concatenate([lf_ref[h, pl.ds(pid, 1), :] for h in range(heads)] + [pad], axis=0)
        s = _dot_nt(qb, _page_heads(page_refs[u], 0, heads).astype(BF16))
        suffix = sum(_dot(part, upper) for part in _split3(lf))
        ss.append(s + run + suffix)
        run = run + jnp.sum(lf, axis=1, keepdims=True)
    s = jnp.concatenate(ss, axis=1)
    m_old = m_scr[...]
    m_new = jnp.maximum(m_old, jnp.max(s, axis=1, keepdims=True))
    alpha = jnp.exp(m_old - m_new)
    p = jnp.exp(s - m_new).astype(BF16)
    l_scr[...] = alpha * l_scr[...] + jnp.sum(p.astype(F32), axis=1, keepdims=True)
    pv = sum(_dot(p[:, u * LANES:(u + 1) * LANES], _page_heads(page_refs[u], 1, heads).astype(BF16))
             for u in range(pages_per_step))
    acc_scr[...] = alpha * acc_scr[...] + pv
    m_scr[...] = m_new
    run_scr[...] = run

    @pl.when(c == pl.num_programs(1) - 1)
    def _():
        o = jnp.where(hm, acc_scr[...] * (1.0 / l_scr[...]), 0.0)
        o_ref[...] = jnp.sum(o, axis=0, keepdims=True).astype(o_ref.dtype)


def _fox_decode(q, kv_new, lf_new, cache_pages, cache_lf, page_table, layer, *, pages_per_step):
    db, n_pages = page_table.shape
    heads = FOX_HEADS
    w = heads * HEAD_DIM
    n_phys = cache_lf.shape[2]
    steps = n_pages // pages_per_step

    def page_map(u):
        def f(b, c, pt):
            return (layer * n_phys + pt[b * n_pages + (n_pages - 1 - (c * pages_per_step + u))], 0, 0)
        return f

    in_specs = [pl.BlockSpec((None, 1, w), lambda b, c, pt: (b, 0, 0)),
                pl.BlockSpec((None, 1, 2 * w), lambda b, c, pt: (b, 0, 0)),
                pl.BlockSpec((None, SUBLANES, 1), lambda b, c, pt: (b, 0, 0)),
                pl.BlockSpec((None, heads, n_phys, LANES), lambda b, c, pt: (layer, 0, 0, 0))]
    in_specs += [pl.BlockSpec((None, 2 * w, LANES), page_map(u)) for u in range(pages_per_step)]
    kern = functools.partial(_fox_dec_kernel, pages_per_step=pages_per_step, heads=heads, n_pages=n_pages)
    return pl.pallas_call(
        kern,
        grid_spec=pltpu.PrefetchScalarGridSpec(
            num_scalar_prefetch=1, grid=(db, steps), in_specs=in_specs,
            out_specs=pl.BlockSpec((None, 1, w), lambda b, c, pt: (b, 0, 0)),
            scratch_shapes=[pltpu.VMEM((SUBLANES, 1), F32), pltpu.VMEM((SUBLANES, 1), F32),
                            pltpu.VMEM((SUBLANES, w), F32), pltpu.VMEM((SUBLANES, 1), F32)]),
        out_shape=jax.ShapeDtypeStruct((db, 1, w), BF16),
        compiler_params=_params(("arbitrary", "arbitrary")),
        name="fox_decode",
    )(page_table.reshape(-1), q, kv_new, lf_new, cache_lf, *([cache_pages] * pages_per_step))


def _moba_topk_kernel(q_ref, km_ref, sel_ref, *, heads, n_blk):
    w = heads * HEAD_DIM
    hm = _head_mask(SUBLANES, w)
    qbd = jnp.where(hm, jnp.broadcast_to(q_ref[...], (SUBLANES, w)), 0.0)
    km = km_ref[...]
    bs = jnp.zeros((SUBLANES, LANES), F32)
    for qp in _split3(qbd):
        for kp in _split3(km):
            bs = bs + _dot_nt(qp, kp)
    lane = lax.broadcasted_iota(jnp.int32, (SUBLANES, LANES), 1)
    bs = jnp.where(lane < n_blk, bs, -jnp.inf)
    out = jnp.zeros((SUBLANES, LANES), jnp.int32)
    for t in range(MOBA_TOPK):
        mx = jnp.max(bs, axis=1, keepdims=True)
        idx = jnp.min(jnp.where(bs == mx, lane, LANES), axis=1, keepdims=True)
        out = jnp.where(lane == t, idx, out)
        bs = jnp.where(lane == idx, -jnp.inf, bs)
    sel_ref[...] = out


def _moba_topk(q, km, n_blk):
    db = q.shape[0]
    heads = MOBA_HEADS
    w = heads * HEAD_DIM
    assert MOBA_TOPK <= n_blk <= LANES
    return pl.pallas_call(
        functools.partial(_moba_topk_kernel, heads=heads, n_blk=n_blk),
        grid=(db,),
        in_specs=[pl.BlockSpec((None, 1, w), lambda b: (b, 0, 0)),
                  pl.BlockSpec((None, LANES, w), lambda b: (b, 0, 0))],
        out_specs=pl.BlockSpec((None, SUBLANES, LANES), lambda b: (b, 0, 0)),
        out_shape=jax.ShapeDtypeStruct((db, SUBLANES, LANES), jnp.int32),
        compiler_params=_params(("arbitrary",)),
        name="moba_topk",
    )(q, km)


def _moba_dec_kernel(sel_ref, pt_ref, q_ref, kn_ref, vn_ref, *refs, n_pages_sel):
    del sel_ref, pt_ref
    page_refs = refs[:n_pages_sel]
    o_ref = refs[n_pages_sel]
    q = q_ref[...]
    qb = jnp.broadcast_to(q, (SUBLANES, HEAD_DIM)).astype(BF16)
    s_self = jnp.sum(q * kn_ref[...], axis=1, keepdims=True)
    s = jnp.concatenate([_dot_nt(qb, page_refs[u][:, 0, :].astype(BF16)) for u in range(n_pages_sel)], axis=1)
    m = jnp.maximum(s_self, jnp.max(s, axis=1, keepdims=True))
    p_self = jnp.exp(s_self - m)
    p = jnp.exp(s - m).astype(BF16)
    l = p_self + jnp.sum(p.astype(F32), axis=1, keepdims=True)
    acc = p_self * vn_ref[...]
    for u in range(n_pages_sel):
        acc = acc + _dot(p[:, u * LANES:(u + 1) * LANES], page_refs[u][:, 1, :].astype(BF16))
    o_ref[...] = (acc * (1.0 / l))[0:1, :].astype(o_ref.dtype)


def _moba_decode(q, kv_new, sel, cache_rows, page_table, layer_page0):
    db, n_pages = page_table.shape
    heads = MOBA_HEADS
    pages_per_blk = MOBA_BLOCK // LANES
    n_pages_sel = MOBA_TOPK * pages_per_blk

    def page_map(u):
        jsel, pg = divmod(u, pages_per_blk)

        def f(b, h, sel_s, pt):
            blk = sel_s[(b * heads + h) * MOBA_TOPK + jsel]
            return (layer_page0 + pt[b * n_pages + blk * pages_per_blk + pg], h, 0, 0)
        return f

    one = lambda col0: pl.BlockSpec((None, 1, HEAD_DIM), lambda b, h, sel_s, pt: (b, 0, col0 + h))
    in_specs = [one(0), one(0), one(heads)]
    in_specs += [pl.BlockSpec((LANES, None, 2, HEAD_DIM), page_map(u)) for u in range(n_pages_sel)]
    kern = functools.partial(_moba_dec_kernel, n_pages_sel=n_pages_sel)
    return pl.pallas_call(
        kern,
        grid_spec=pltpu.PrefetchScalarGridSpec(
            num_scalar_prefetch=2, grid=(db, heads), in_specs=in_specs,
            out_specs=pl.BlockSpec((None, 1, HEAD_DIM), lambda b, h, sel_s, pt: (b, 0, h))),
        out_shape=jax.ShapeDtypeStruct((db, 1, heads * HEAD_DIM), BF16),
        compiler_params=_params(("arbitrary", "arbitrary")),
        name="moba_decode",
    )(sel.reshape(-1), page_table.reshape(-1), q, kv_new, kv_new, *([cache_rows] * n_pages_sel))


def _mem_dec_kernel(q_ref, kv_ref, o_ref, *, heads, n_mem):
    outs = []
    for h in range(heads):
        qb = jnp.broadcast_to(q_ref[:, _head_cols(h)], (SUBLANES, HEAD_DIM)).astype(BF16)
        k = kv_ref[pl.ds(h, n_mem, stride=2 * heads), :].astype(BF16)
        v = kv_ref[pl.ds(heads + h, n_mem, stride=2 * heads), :].astype(BF16)
        s = _dot_nt(qb, k)
        p = jnp.exp(s - jnp.max(s, axis=1, keepdims=True))
        l = jnp.sum(p, axis=1, keepdims=True)
        outs.append((_dot(p.astype(BF16), v) * (1.0 / l))[0:1, :])
    o_ref[...] = jnp.concatenate(outs, axis=1).astype(o_ref.dtype)


def _mem_decode(q, mem_rows, layer, *, q_col_block):
    db = q.shape[0]
    heads = MEM_HEADS
    n_mem = mem_rows.shape[1] // (2 * heads)
    return pl.pallas_call(
        functools.partial(_mem_dec_kernel, heads=heads, n_mem=n_mem),
        grid=(db,),
        in_specs=[pl.BlockSpec((None, 1, HEAD_TILE), lambda b: (b, 0, q_col_block)),
                  pl.BlockSpec((None, mem_rows.shape[1], HEAD_DIM), lambda b: (layer * db + b, 0, 0))],
        out_specs=pl.BlockSpec((None, 1, heads * HEAD_DIM), lambda b: (b, 0, 0)),
        out_shape=jax.ShapeDtypeStruct((db, 1, heads * HEAD_DIM), BF16),
        compiler_params=_params(("arbitrary",)),
        name="mem_decode",
    )(q, mem_rows)


def _rope_tables(pos):
    half = HEAD_DIM // 2
    inv_freq = ROPE_THETA ** (-jnp.arange(half, dtype=F32) / half)
    ang = pos.astype(F32)[:, None] * inv_freq[None, :]
    cos, sin = jnp.cos(ang), jnp.sin(ang)
    return jnp.concatenate([cos, cos], axis=-1), jnp.concatenate([-sin, sin], axis=-1)


def _pad_cols(a, n):
    return jnp.pad(a, ((0, 0), (0, n - a.shape[1])))


def _prep_w_in(w_in, d):
    mw, fw, cw = MOBA_HEADS * HEAD_DIM, FOX_HEADS * HEAD_DIM, MEM_HEADS * HEAD_DIM
    o_fl = 3 * mw + 3 * fw
    o_qc = o_fl + FOX_HEADS
    o_gl = o_qc + cw
    layers, k, n = w_in.shape
    assert n == o_gl + 3 * d and cw % W_IN_COLS == 0 and (3 * d) % W_IN_COLS == 0 and o_fl % W_IN_COLS == 0
    w_t = w_in.reshape(layers, k // LANES, LANES, n).transpose(3, 1, 0, 2)
    starts = list(range(0, o_fl, W_IN_COLS)) + list(range(o_qc, o_gl, W_IN_COLS))
    starts += [o_gl] * ((HEAD_TILE - cw) // W_IN_COLS)
    starts += list(range(o_gl, n, W_IN_COLS))
    w_main = _w_in_layout(w_t, starts, W_IN_COLS)
    w_fl = _w_in_layout(w_t, [o_fl], LANES)
    return w_main, w_fl


def _prep_layer(l, g_attn, b_f, g_mem, g_ffn, w_conv, b_conv, fp):
    f = w_conv.shape[2]
    return dict(
        g_attn=g_attn[l][None], b_f=_pad_cols(b_f[l][None], LANES), g_mem=g_mem[l][None], g_ffn=g_ffn[l][None],
        w_conv=jnp.pad(w_conv[l], ((0, SUBLANES - CONV_W), (0, fp - f))), b_conv=_pad_cols(b_conv[l][None], fp))


def kernel(x_prompt, x_sample, cache_moba_kv, cache_fox_kv, cache_fox_logf, cache_mem_kv, state_ffn_conv,
           page_table, mem_prompt, g_attn, w_in, b_f, w_br_moba, w_br_fox, w_br_mem, w_out, g_mem,
           w_mem_kv, g_ffn, w_up, w_conv, b_conv, w_down, g_final):
    bsz, seq, d = x_prompt.shape
    db = x_sample.shape[0]
    depth, n_phys, page = cache_moba_kv.shape[:3]
    n_pages = page_table.shape[1]
    past_len = n_pages * page
    n_mem = mem_prompt.shape[1]
    mw, fw, cw = MOBA_HEADS * HEAD_DIM, FOX_HEADS * HEAD_DIM, MEM_HEADS * HEAD_DIM
    assert page == LANES and db == SUBLANES and x_sample.shape[1] == 1
    ff_tile = 512
    tm = max(t for t in (1024, 512, 256) if seq % t == 0)
    tq = MOBA_BLOCK

    cos_p, sin_p = _rope_tables(jnp.arange(seq))
    cos_s, sin_s = _rope_tables(jnp.full((db,), past_len))
    gf = g_final[None]

    moba_t = jnp.transpose(cache_moba_kv, (0, 1, 2, 4, 3, 5))
    moba_pages = moba_t.reshape(depth * n_phys, page * 2 * MOBA_HEADS, HEAD_DIM)
    moba_rows = moba_t.reshape(depth * n_phys * page, MOBA_HEADS, 2, HEAD_DIM)
    fox_pages = jnp.transpose(cache_fox_kv, (0, 1, 2, 4, 3, 5)).reshape(depth * n_phys, page * 2 * FOX_HEADS, HEAD_DIM)
    lf_cache = jnp.transpose(cache_fox_logf, (0, 3, 1, 2))
    mem_rows = cache_mem_kv.reshape(depth * db, n_mem * 2 * MEM_HEADS, HEAD_DIM)

    f = w_down.shape[1]
    fp = -(-f // ff_tile) * ff_tile
    w_main, w_fl = _prep_w_in(w_in, d)
    wm, wf, wc = _cast_bf16(w_br_moba, mw), _cast_bf16(w_br_fox, fw), _cast_bf16(w_br_mem, cw)
    wo, w_mem = _cast_bf16(w_out, 512), _cast_bf16(w_mem_kv, 512)
    wa, wb = _cast_w_up(w_up, fp, 256)
    wdn = _cast_bf16(w_down, f // SUBLANES)
    moba_out = jnp.zeros((depth, bsz, seq, MOBA_HEADS, 2, HEAD_DIM), F32)
    fox_out = jnp.zeros((depth, bsz, seq, FOX_HEADS, 2, HEAD_DIM), F32)
    xp = x_prompt.reshape(bsz * seq, d)
    xs = x_sample.reshape(db, d)
    outs = {k: [] for k in ("p_logf", "p_mem", "p_conv", "s_moba", "s_fox", "s_logf", "s_conv")}
    yp = ys = None
    for l in range(depth):
        w = _prep_layer(l, g_attn, b_f, g_mem, g_ffn, w_conv, b_conv, fp)
        g_last = gf if l == depth - 1 else None

        mkv = _norm_mm(mem_prompt.reshape(bsz * n_mem, d), w["g_mem"], w_mem, layer=l, tm=min(512, bsz * n_mem))
        q_all, moba_kv, fox_kv, gates, logf, c = _in_proj(
            xp, w["g_attn"], w_main, w_fl, w["b_f"], cos_p, sin_p, layer=l, tm=tm, seq_len=seq, q_dtype=BF16,
            q_scale=SCALE * LOG2E)
        q3 = q_all.reshape(bsz, seq, 3 * HEAD_TILE)
        c3 = c.reshape(bsz, seq, LANES)
        o_m, moba_out = _causal_attention(q3, moba_kv.reshape(bsz, seq, 2 * mw), mode="moba", heads=MOBA_HEADS,
                                          q_col0=0, layer=l, kv_out=moba_out)
        o_f, fox_out = _causal_attention(q3, fox_kv.reshape(bsz, seq, 2 * fw), mode="fox", heads=FOX_HEADS,
                                         q_col0=MOBA_HEADS, layer=l, kv_out=fox_out, c=c3)
        o_c = _mem_attention(q3, mkv.reshape(bsz, n_mem, 2 * cw), heads=MEM_HEADS, q_col_block=2, tq=tq)
        xp = _merge_out(o_m.reshape(bsz * seq, mw), o_f.reshape(bsz * seq, fw), o_c.reshape(bsz * seq, cw),
                        gates, xp, wm, wf, wc, wo, layer=l, tm=256)
        gact, tails, km = _ffn_up(xp, w["g_ffn"], wa, wb, w["w_conv"], w["b_conv"], layer=l, tm=tm, tn=ff_tile,
                                  seq_len=seq, kmean_src=(moba_pages, page_table, l * n_phys))
        xp, yp = _ffn_down(gact, wdn, xp, g_last, layer=l, tm=256)
        outs["p_logf"].append(logf[:, :FOX_HEADS].reshape(bsz, seq, FOX_HEADS))
        outs["p_mem"].append(mkv.reshape(bsz, n_mem, 2, MEM_HEADS, HEAD_DIM))
        tiles_per_seq = seq // tm
        outs["p_conv"].append(tails[tiles_per_seq - 1::tiles_per_seq, SUBLANES - (CONV_W - 1):, :f])

        q_s, moba_s, fox_s, gates_s, logf_s, _ = _in_proj(
            xs, w["g_attn"], w_main, w_fl, w["b_f"], cos_s, sin_s, layer=l, tm=db, seq_len=1, q_dtype=F32,
            q_scale=SCALE)
        q_s3 = q_s.reshape(db, 1, 3 * HEAD_TILE)
        sel = _moba_topk(q_s3, km, past_len // MOBA_BLOCK)
        o_ms = _moba_decode(q_s3, moba_s.reshape(db, 1, 2 * mw), sel[:, :MOBA_HEADS, :MOBA_TOPK],
                            moba_rows, page_table, l * n_phys)
        o_fs = _fox_decode(q_s3[:, :, mw:mw + fw], fox_s.reshape(db, 1, 2 * fw),
                           logf_s[:, :SUBLANES].reshape(db, SUBLANES, 1), fox_pages, lf_cache, page_table,
                           l, pages_per_step=16)
        o_cs = _mem_decode(q_s3, mem_rows, l, q_col_block=2)
        xs = _merge_out(o_ms.reshape(db, mw), o_fs.reshape(db, fw), o_cs.reshape(db, cw), gates_s, xs,
                        wm, wf, wc, wo, layer=l, tm=db)
        prev = jnp.pad(state_ffn_conv[l], ((0, 0), (0, 0), (0, fp - f)))
        gact_s, a_s = _ffn_up(xs, w["g_ffn"], wa, wb, w["w_conv"], w["b_conv"],
                              layer=l, tm=db, tn=ff_tile, seq_len=1, prev=(prev[:, 0], prev[:, 1]))
        xs, ys = _ffn_down(gact_s, wdn, xs, g_last, layer=l, tm=db)
        outs["s_moba"].append(moba_s.reshape(db, 1, 2, MOBA_HEADS, HEAD_DIM))
        outs["s_fox"].append(fox_s.reshape(db, 1, 2, FOX_HEADS, HEAD_DIM))
        outs["s_logf"].append(logf_s[:, :FOX_HEADS].reshape(db, 1, FOX_HEADS))
        outs["s_conv"].append(jnp.stack([state_ffn_conv[l][:, 1], a_s[:, :f]], axis=1))

    st = lambda k: jnp.stack(outs[k])
    kv_order = (0, 1, 2, 4, 3, 5)
    return (yp.reshape(bsz, seq, d), ys.reshape(db, 1, d),
            jnp.transpose(moba_out, kv_order), jnp.transpose(fox_out, kv_order),
            st("p_logf"), st("p_mem"), st("p_conv"),
            st("s_moba"), st("s_fox"), st("s_logf"), st("s_conv"))
```

```python
import functools

import jax
import jax.numpy as jnp
from jax import lax
from jax.experimental import pallas as pl
from jax.experimental.pallas import tpu as pltpu

F32 = jnp.float32
BF16 = jnp.bfloat16

HEAD_DIM = 128
MOBA_HEADS = 6
FOX_HEADS = 6
MEM_HEADS = 4
MOBA_BLOCK = 256
MOBA_TOPK = 3
CONV_W = 3
ROPE_THETA = 10000.0
NORM_EPS = 1e-6
NEG = -1e30

LANES = 128
SUBLANES = 8
HEAD_TILE = MOBA_HEADS * HEAD_DIM
VMEM_LIMIT = 56 * 1024 * 1024
ROW_CHUNK = 256
SCALE = HEAD_DIM ** -0.5
LOG2E = 1.4426950408889634

_NT = (((1,), (1,)), ((), ()))


def _params(sem):
    return pltpu.CompilerParams(dimension_semantics=sem, vmem_limit_bytes=VMEM_LIMIT)


def _rms(x, g):
    return x * lax.rsqrt(jnp.mean(x * x, axis=-1, keepdims=True) + NORM_EPS) * g


def _log_sigmoid(x):
    return jnp.minimum(x, 0.0) - jnp.log1p(jnp.exp(-jnp.abs(x)))


def _split3(v):
    hi = v.astype(BF16)
    r1 = v - hi.astype(F32)
    mid = r1.astype(BF16)
    lo = (r1 - mid.astype(F32)).astype(BF16)
    return hi, mid, lo


def _dot(a, b):
    return jnp.dot(a, b, preferred_element_type=F32)


def _dot_nt(a, b):
    return lax.dot_general(a, b, _NT, preferred_element_type=F32)


def _cumsum_rows(v, carry, blk=256):
    tm = v.shape[0]
    blk = min(blk, tm)
    r = lax.broadcasted_iota(jnp.int32, (blk, blk), 0)
    c = lax.broadcasted_iota(jnp.int32, (blk, blk), 1)
    tri = (c <= r).astype(BF16)
    outs = []
    for s in range(0, tm, blk):
        hi, mid, lo = _split3(v[s:s + blk])
        cs = _dot(tri, hi) + _dot(tri, mid) + _dot(tri, lo) + carry
        outs.append(cs)
        carry = cs[blk - 1:blk, :]
    return jnp.concatenate(outs, axis=0), carry


def _rope(acc, cos, sin_signed):
    outs = []
    for hh in range(acc.shape[1] // HEAD_DIM):
        xh = acc[:, hh * HEAD_DIM:(hh + 1) * HEAD_DIM]
        outs.append(xh * cos + pltpu.roll(xh, HEAD_DIM // 2, 1) * sin_signed)
    return jnp.concatenate(outs, axis=1)


def _head_cols(h):
    return slice(h * HEAD_DIM, (h + 1) * HEAD_DIM)


W_IN_COLS = 256


def _w_in_kernel(starts_ref, x_ref, o_ref):
    del starts_ref
    cols, kt, layers, _ = x_ref.shape
    for l in range(layers):
        for t in range(kt):
            o_ref[l, t * LANES:(t + 1) * LANES, :] = x_ref[:, t, l, :].T.astype(o_ref.dtype)


def _w_in_layout(w_t, starts, cols):
    _, kt, layers, _ = w_t.shape
    starts = jnp.asarray(starts, jnp.int32)
    return pl.pallas_call(
        _w_in_kernel,
        grid_spec=pltpu.PrefetchScalarGridSpec(
            num_scalar_prefetch=1, grid=(starts.shape[0],),
            in_specs=[pl.BlockSpec((pl.Element(cols), pl.Element(kt), pl.Element(layers), pl.Element(LANES)),
                                   lambda j, st: (st[j], 0, 0, 0))],
            out_specs=pl.BlockSpec((layers, kt * LANES, cols), lambda j, st: (0, 0, j))),
        out_shape=jax.ShapeDtypeStruct((layers, kt * LANES, starts.shape[0] * cols), BF16),
        compiler_params=_params(("arbitrary",)),
        name="w_in_layout",
    )(starts, w_t)


def _cast_kernel(x_ref, o_ref):
    o_ref[...] = x_ref[...].astype(o_ref.dtype)


def _cast_bf16(w, rows):
    layers, r, c = w.shape
    rows = min(rows, r)
    assert r % rows == 0
    spec = pl.BlockSpec((None, rows, c), lambda l, i: (l, i, 0))
    return pl.pallas_call(
        _cast_kernel, grid=(layers, r // rows), in_specs=[spec], out_specs=spec,
        out_shape=jax.ShapeDtypeStruct(w.shape, BF16), compiler_params=_params(("arbitrary", "arbitrary")),
        name="cast_bf16")(w)


def _cast_up_kernel(x_ref, a_ref, b_ref, *, f):
    for o_ref, c0 in ((a_ref, 0), (b_ref, f)):
        o_ref[:, :f] = x_ref[:, c0:c0 + f].astype(o_ref.dtype)
        o_ref[:, f:] = jnp.zeros((o_ref.shape[0], o_ref.shape[1] - f), o_ref.dtype)


def _cast_w_up(w_up, fp, rows):
    layers, d, f2 = w_up.shape
    out = pl.BlockSpec((None, rows, fp), lambda l, i: (l, i, 0))
    return pl.pallas_call(
        functools.partial(_cast_up_kernel, f=f2 // 2), grid=(layers, d // rows),
        in_specs=[pl.BlockSpec((None, rows, f2), lambda l, i: (l, i, 0))], out_specs=[out, out],
        out_shape=[jax.ShapeDtypeStruct((layers, d, fp), BF16)] * 2,
        compiler_params=_params(("arbitrary", "arbitrary")), name="cast_w_up")(w_up)


def _in_proj_kernel(x_ref, g_ref, w_ref, wfl_ref, bf_ref, cos_ref, sin_ref,
                    q_ref, mkv_ref, fkv_ref, gate_ref, logf_ref, c_ref,
                    h_scr, carry_scr, *, tiles_per_seq, cumsum, q_scale):
    i = pl.program_id(0)
    j = pl.program_id(1)

    @pl.when(j == 0)
    def _():
        h = _rms(x_ref[...], g_ref[...]).astype(BF16)
        h_scr[...] = h
        logf = _log_sigmoid(_dot(h, wfl_ref[...]) + bf_ref[...])
        logf_ref[...] = logf
        if cumsum:
            @pl.when(i % tiles_per_seq == 0)
            def _():
                carry_scr[...] = jnp.zeros_like(carry_scr)
            c, carry = _cumsum_rows(logf, carry_scr[...])
            c_ref[...] = c
            carry_scr[...] = carry
        else:
            c_ref[...] = logf

    tm = x_ref.shape[0]
    rc = min(ROW_CHUNK, tm)

    def tile(out_ref, epilogue):
        for r0 in range(0, tm, rc):
            rows = slice(r0, r0 + rc)
            acc = _dot(h_scr[rows, :], w_ref[...])
            out_ref[rows, :] = epilogue(acc, rows).astype(out_ref.dtype)

    rope = lambda acc, rows: _rope(acc, cos_ref[rows, :], sin_ref[rows, :])

    @pl.when(j == 0)
    def _():
        tile(q_ref, lambda acc, rows: rope(acc, rows) * q_scale)

    @pl.when(j == 1)
    def _():
        tile(mkv_ref, rope)

    @pl.when(j == 2)
    def _():
        tile(mkv_ref, lambda acc, rows: acc)

    @pl.when(jnp.logical_or(j == 3, j == 6))
    def _():
        tile(q_ref, lambda acc, rows: acc * q_scale)

    @pl.when(jnp.logical_or(j == 4, j == 5))
    def _():
        tile(fkv_ref, lambda acc, rows: acc)

    @pl.when(j >= 7)
    def _():
        tile(gate_ref, lambda acc, rows: jax.nn.sigmoid(acc))


def _in_proj(x, g, w_main, w_fl, b_f, cos, sin, *, layer, tm, seq_len, q_dtype, q_scale):
    m, d = x.shape
    n_tiles = w_main.shape[2] // HEAD_TILE
    n_gate = n_tiles - 7
    table_tiles = cos.shape[0] // tm
    cumsum = seq_len > 1
    tiles_per_seq = max(seq_len // tm, 1)
    kern = functools.partial(_in_proj_kernel, tiles_per_seq=tiles_per_seq, cumsum=cumsum, q_scale=q_scale)
    row = lambda i, j: (i, 0)
    return pl.pallas_call(
        kern,
        grid=(m // tm, n_tiles),
        in_specs=[
            pl.BlockSpec((tm, d), row),
            pl.BlockSpec((1, d), lambda i, j: (0, 0)),
            pl.BlockSpec((None, d, HEAD_TILE), lambda i, j: (layer, 0, j)),
            pl.BlockSpec((None, d, LANES), lambda i, j: (layer, 0, 0)),
            pl.BlockSpec((1, LANES), lambda i, j: (0, 0)),
            pl.BlockSpec((tm, LANES), lambda i, j: (i % table_tiles, 0)),
            pl.BlockSpec((tm, LANES), lambda i, j: (i % table_tiles, 0)),
        ],
        out_specs=[
            pl.BlockSpec((tm, HEAD_TILE), lambda i, j: (i, jnp.minimum(j // 3, 2))),
            pl.BlockSpec((tm, HEAD_TILE), lambda i, j: (i, jnp.clip(j - 1, 0, 1))),
            pl.BlockSpec((tm, HEAD_TILE), lambda i, j: (i, jnp.clip(j - 4, 0, 1))),
            pl.BlockSpec((tm, HEAD_TILE), lambda i, j: (i, jnp.clip(j - 7, 0, n_gate - 1))),
            pl.BlockSpec((tm, LANES), row),
            pl.BlockSpec((tm, LANES), row),
        ],
        out_shape=[
            jax.ShapeDtypeStruct((m, 3 * HEAD_TILE), q_dtype),
            jax.ShapeDtypeStruct((m, 2 * HEAD_TILE), F32),
            jax.ShapeDtypeStruct((m, 2 * HEAD_TILE), F32),
            jax.ShapeDtypeStruct((m, n_gate * HEAD_TILE), BF16),
            jax.ShapeDtypeStruct((m, LANES), F32),
            jax.ShapeDtypeStruct((m, LANES), F32),
        ],
        scratch_shapes=[pltpu.VMEM((tm, d), BF16), pltpu.VMEM((1, LANES), F32)],
        compiler_params=_params(("arbitrary", "arbitrary")),
        name="in_proj",
    )(x, g, w_main, w_fl, b_f, cos, sin)


def _norm_mm_kernel(x_ref, g_ref, w_ref, o_ref):
    h = _rms(x_ref[...], g_ref[...]).astype(BF16)
    o_ref[...] = _dot(h, w_ref[...])


def _norm_mm(x, g, w, *, layer, tm):
    m, d = x.shape
    n = w.shape[2]
    return pl.pallas_call(
        _norm_mm_kernel,
        grid=(m // tm,),
        in_specs=[pl.BlockSpec((tm, d), lambda i: (i, 0)),
                  pl.BlockSpec((1, d), lambda i: (0, 0)),
                  pl.BlockSpec((None, d, n), lambda i: (layer, 0, 0))],
        out_specs=pl.BlockSpec((tm, n), lambda i: (i, 0)),
        out_shape=jax.ShapeDtypeStruct((m, n), F32),
        compiler_params=_params(("arbitrary",)),
        name="mem_kv_proj",
    )(x, g, w)


MASK_BIAS = 1e30


def _causal_attn_kernel(*refs, mode, blk, n_blk, layer):
    q_ref, k_ref, v_ref = refs[:3]
    c_ref = refs[3] if mode == "fox" else None
    o_ref, kv_out_ref, sem = refs[-3:]
    b = pl.program_id(0)
    h = pl.program_id(1)
    s_len = q_ref.shape[0]
    copies = [pltpu.make_async_copy(src.at[0], kv_out_ref.at[layer, b, :, h, kv, :], sem.at[kv])
              for kv, src in enumerate((k_ref, v_ref))]
    for cp in copies:
        cp.start()
    q = q_ref[...]
    kf = k_ref[0]
    vf = v_ref[0]
    vb = jnp.concatenate([vf.astype(BF16), jnp.ones((s_len, HEAD_DIM), BF16)], axis=1)
    lane = lax.broadcasted_iota(jnp.int32, (s_len, LANES), 1)

    if mode == "fox":
        c_col = jnp.sum(jnp.where(lane == h, c_ref[...], 0.0), axis=1, keepdims=True) * LOG2E
        hi, mid, lo = [part.astype(F32) for part in _split3(c_col)]
        aug_q = jnp.where(lane == 0, hi, jnp.where(lane == 1, mid, jnp.where(lane == 2, lo,
                          jnp.where(lane < 6, 1.0, 0.0))))
        aug_k = jnp.where(lane < 3, 1.0, jnp.where(lane == 3, -hi, jnp.where(lane == 4, -mid,
                          jnp.where(lane == 5, -lo, 0.0))))
    else:
        nb8 = -(-n_blk // SUBLANES) * SUBLANES
        km = [jnp.sum(kf[n * blk:(n + 1) * blk], axis=0, keepdims=True) * (1.0 / blk) for n in range(n_blk)]
        km = jnp.concatenate(km + [jnp.zeros((1, HEAD_DIM), F32)] * (nb8 - n_blk), axis=0)
        bs = sum(_dot_nt(part, q) for part in _split3(km))
        blk_id = lax.broadcasted_iota(jnp.int32, (nb8, s_len), 0)
        own = lax.broadcasted_iota(jnp.int32, (nb8, s_len), 1) // blk
        cnt = jnp.zeros((nb8, s_len), F32)
        for mth in range(n_blk - 1):
            sm = bs[mth:mth + 1, :]
            beats = jnp.logical_or(sm > bs, jnp.logical_and(sm == bs, mth < blk_id))
            cnt = cnt + jnp.where(jnp.logical_and(beats, mth < own), 1.0, 0.0)
        sel = jnp.where(jnp.logical_and(blk_id < own, cnt < MOBA_TOPK), 1.0, 0.0)
        sel = jnp.concatenate([sel, jnp.zeros((LANES - nb8, s_len), F32)], axis=0).astype(BF16)
        eye = (lax.broadcasted_iota(jnp.int32, (blk, blk), 0)
               == lax.broadcasted_iota(jnp.int32, (blk, blk), 1)).astype(BF16)
        selc = jnp.concatenate([_dot_nt(eye, sel[:, t * blk:(t + 1) * blk]) for t in range(n_blk)], axis=0)
        row_blk = lax.broadcasted_iota(jnp.int32, (s_len, LANES), 0) // blk
        aug_q = jnp.where(lane == row_blk, 0.0, (selc - 1.0) * MASK_BIAS)
        aug_k = jnp.where(lane == row_blk, 1.0, 0.0)

    q_aug = jnp.concatenate([q, aug_q.astype(BF16)], axis=1)
    k_aug = jnp.concatenate([kf.astype(BF16), aug_k.astype(BF16)], axis=1)
    tq = 2 * blk if s_len % (2 * blk) == 0 else blk
    row = lax.broadcasted_iota(jnp.int32, (tq, tq), 0)
    col = lax.broadcasted_iota(jnp.int32, (tq, tq), 1)
    for t in range(s_len // tq):
        n = (t + 1) * tq
        s = _dot_nt(q_aug[t * tq:(t + 1) * tq], k_aug[:n])
        diag = jnp.where(col <= row, s[:, n - tq:], NEG)
        s = diag if t == 0 else jnp.concatenate([s[:, :n - tq], diag], axis=1)
        p = jnp.exp2((s - jnp.max(s, axis=1, keepdims=True)).astype(BF16))
        ol = _dot(p, vb[:n])
        o_ref[t * tq:(t + 1) * tq, :] = (ol[:, :HEAD_DIM] * (1.0 / ol[:, HEAD_DIM:])).astype(o_ref.dtype)
    for cp in copies:
        cp.wait()


def _causal_attention(q, kv, *, mode, heads, q_col0, layer, kv_out, c=None):
    b, s, _ = q.shape
    blk = MOBA_BLOCK
    n_blk = s // blk
    assert s % blk == 0 and n_blk <= LANES and kv.shape[1] == s
    seq_head = lambda col0: pl.BlockSpec((None, s, HEAD_DIM), lambda bi, h: (bi, 0, col0 + h))
    kv_head = lambda col0: pl.BlockSpec((1, s, HEAD_DIM), lambda bi, h: (bi, 0, col0 + h))
    in_specs = [seq_head(q_col0), kv_head(0), kv_head(heads)]
    args = [q, kv, kv]
    if mode == "fox":
        in_specs += [pl.BlockSpec((None, s, LANES), lambda bi, h: (bi, 0, 0))]
        args += [c]
    aliases = {len(args): 1}
    in_specs += [pl.BlockSpec(memory_space=pl.ANY)]
    args += [kv_out]
    return pl.pallas_call(
        functools.partial(_causal_attn_kernel, mode=mode, blk=blk, n_blk=n_blk, layer=layer),
        grid=(b, heads),
        in_specs=in_specs,
        out_specs=[seq_head(0), pl.BlockSpec(memory_space=pl.ANY)],
        scratch_shapes=[pltpu.SemaphoreType.DMA((2,))],
        out_shape=[jax.ShapeDtypeStruct((b, s, heads * HEAD_DIM), BF16),
                   jax.ShapeDtypeStruct(kv_out.shape, kv_out.dtype)],
        input_output_aliases=aliases,
        compiler_params=_params(("arbitrary", "arbitrary")),
        name="attn_" + mode,
    )(*args)


def _mem_attn_kernel(q_ref, kv_ref, o_ref, kb, vb, *, heads):
    w = heads * HEAD_DIM

    @pl.when(pl.program_id(1) == 0)
    def _():
        kb[...] = kv_ref[:, :w].astype(BF16)
        vb[...] = kv_ref[:, w:].astype(BF16)

    for h in range(heads):
        s = _dot_nt(q_ref[:, _head_cols(h)], kb[:, _head_cols(h)])
        p = jnp.exp2(s - jnp.max(s, axis=1, keepdims=True))
        l = jnp.sum(p, axis=1, keepdims=True)
        o_ref[:, _head_cols(h)] = (_dot(p.astype(BF16), vb[:, _head_cols(h)]) * (1.0 / l)).astype(o_ref.dtype)


def _mem_attention(q, kv, *, heads, q_col_block, tq):
    b, s, _ = q.shape
    n_mem = kv.shape[1]
    w = heads * HEAD_DIM
    return pl.pallas_call(
        functools.partial(_mem_attn_kernel, heads=heads),
        grid=(b, s // tq),
        in_specs=[pl.BlockSpec((None, tq, HEAD_TILE), lambda bi, qi: (bi, qi, q_col_block)),
                  pl.BlockSpec((None, n_mem, 2 * w), lambda bi, qi: (bi, 0, 0))],
        out_specs=pl.BlockSpec((None, tq, w), lambda bi, qi: (bi, qi, 0)),
        out_shape=jax.ShapeDtypeStruct((b, s, w), BF16),
        scratch_shapes=[pltpu.VMEM((n_mem, w), BF16), pltpu.VMEM((n_mem, w), BF16)],
        compiler_params=_params(("arbitrary", "arbitrary")),
        name="attn_mem",
    )(q, kv)


def _merge_kernel(om_ref, of_ref, oc_ref, gate_ref, x_ref, wm_ref, wf_ref, wc_ref, wo_ref, o_ref):
    d = x_ref.shape[1]
    merged = gate_ref[:, 0:d].astype(F32) * _dot(om_ref[...], wm_ref[...])
    merged = merged + gate_ref[:, d:2 * d].astype(F32) * _dot(of_ref[...], wf_ref[...])
    merged = merged + gate_ref[:, 2 * d:3 * d].astype(F32) * _dot(oc_ref[...], wc_ref[...])
    o_ref[...] = x_ref[...] + _dot(merged.astype(BF16), wo_ref[...])


def _resident(w, layer):
    return pl.BlockSpec((None,) + w.shape[1:], lambda i: (layer, 0, 0), pipeline_mode=pl.Buffered(1))


def _merge_out(om, of, oc, gates, x, wm, wf, wc, wo, *, layer, tm):
    m, d = x.shape
    rows = lambda w: pl.BlockSpec((tm, w), lambda i: (i, 0))
    return pl.pallas_call(
        _merge_kernel,
        grid=(m // tm,),
        in_specs=[rows(om.shape[1]), rows(of.shape[1]), rows(oc.shape[1]), rows(gates.shape[1]), rows(d),
                  _resident(wm, layer), _resident(wf, layer), _resident(wc, layer), _resident(wo, layer)],
        out_specs=rows(d),
        out_shape=jax.ShapeDtypeStruct((m, d), F32),
        compiler_params=_params(("arbitrary",)),
        name="merge_out",
    )(om, of, oc, gates, x, wm, wf, wc, wo)


def _block_means(page_refs, km_out, t, *, chunks_per_seq, n_seq, heads, every_step):
    pages_per_blk = MOBA_BLOCK // LANES
    blks = len(page_refs) // pages_per_blk
    c = t % chunks_per_seq

    def body():
        used = chunks_per_seq * blks
        if used < km_out.shape[0]:
            km_out[used:, :] = jnp.zeros((km_out.shape[0] - used, km_out.shape[1]), F32)
        for bb in range(blks):
            per_head = []
            for h in range(heads):
                tot = jnp.zeros((1, HEAD_DIM), F32)
                for pp in range(pages_per_blk):
                    keys = page_refs[bb * pages_per_blk + pp][pl.ds(2 * h, LANES, stride=2 * heads), :]
                    tot = tot + jnp.sum(keys, axis=0, keepdims=True)
                per_head.append(tot)
            km_out[pl.ds(c * blks + bb, 1), :] = jnp.concatenate(per_head, axis=1) * (1.0 / MOBA_BLOCK)

    if every_step:
        body()
    else:
        pl.when(t < n_seq * chunks_per_seq)(body)


def _ffn_up_kernel(*refs, tiles_per_seq, per_row_state, kmean=None):
    if per_row_state:
        x_ref, g_ref, wa_ref, wb_ref, wc_ref, bc_ref, p0_ref, p1_ref, g_out, a_out, h_scr = refs
    elif kmean is None:
        x_ref, g_ref, wa_ref, wb_ref, wc_ref, bc_ref, g_out, tail_out, h_scr, carry_scr = refs
    else:
        n_pg = kmean["pages_per_step"]
        x_ref, g_ref, wa_ref, wb_ref, wc_ref, bc_ref = refs[1:7]
        page_refs = refs[7:7 + n_pg]
        g_out, tail_out, km_out, h_scr, carry_scr = refs[7 + n_pg:]
    i = pl.program_id(0)
    j = pl.program_id(1)

    @pl.when(j == 0)
    def _():
        h_scr[...] = _rms(x_ref[...], g_ref[...]).astype(BF16)

    w = wc_ref[...]
    tm = x_ref.shape[0]
    rc = min(ROW_CHUNK, tm)
    if not per_row_state:
        @pl.when(i % tiles_per_seq == 0)
        def _():
            carry_scr[j] = jnp.zeros(carry_scr.shape[1:], F32)
        prev = carry_scr[j]
    if kmean is not None:
        _block_means(page_refs, km_out, i * pl.num_programs(1) + j, chunks_per_seq=kmean["chunks_per_seq"],
                     n_seq=kmean["n_seq"], heads=kmean["heads"], every_step=kmean["every_step"])
    for r0 in range(0, tm, rc):
        rows = slice(r0, r0 + rc)
        h = h_scr[rows, :]
        a = _dot(h, wa_ref[...])
        b = _dot(h, wb_ref[...])
        if per_row_state:
            a1 = p1_ref[rows, :]
            a2 = p0_ref[rows, :]
            a_out[rows, :] = a
        else:
            row = lax.broadcasted_iota(jnp.int32, a.shape, 0)
            a1 = jnp.where(row == 0, prev[7:8, :], pltpu.roll(a, 1, 0))
            a2 = jnp.where(row == 0, prev[6:7, :], jnp.where(row == 1, prev[7:8, :], pltpu.roll(a, 2, 0)))
            prev = a[rc - SUBLANES:, :]
        a_conv = bc_ref[...] + a2 * w[0:1, :] + a1 * w[1:2, :] + a * w[2:3, :]
        g_out[rows, :] = (a_conv * jax.nn.sigmoid(a_conv) * b).astype(g_out.dtype)
    if not per_row_state:
        carry_scr[j] = prev
        tail_out[...] = prev


def _ffn_up(x, g, wa, wb, wconv, bconv, *, layer, tm, tn, seq_len, prev=None, kmean_src=None):
    m, d = x.shape
    f = wa.shape[2]
    nj = f // tn
    per_row_state = prev is not None
    tiles_per_seq = max(seq_len // tm, 1)
    col = lambda r: pl.BlockSpec((r, tn), lambda i, j, *_: (0, j))
    wcol = pl.BlockSpec((None, d, tn), lambda i, j, *_: (layer, 0, j))
    in_specs = [pl.BlockSpec((tm, d), lambda i, j, *_: (i, 0)),
                pl.BlockSpec((1, d), lambda i, j, *_: (0, 0)), wcol, wcol, col(SUBLANES), col(1)]
    args = [x, g, wa, wb, wconv, bconv]
    tile = pl.BlockSpec((tm, tn), lambda i, j, *_: (i, j))
    scratch = [pltpu.VMEM((tm, d), BF16)]
    kmean = None
    prefetch = []
    if per_row_state:
        in_specs += [tile, tile]
        args += [prev[0], prev[1]]
        out_specs = [tile, tile]
        out_shape = [jax.ShapeDtypeStruct((m, f), BF16), jax.ShapeDtypeStruct((m, f), F32)]
    else:
        out_specs = [tile, pl.BlockSpec((None, SUBLANES, tn), lambda i, j, *_: (i, 0, j))]
        out_shape = [jax.ShapeDtypeStruct((m, f), BF16), jax.ShapeDtypeStruct((m // tm, SUBLANES, f), F32)]
        scratch += [pltpu.VMEM((nj, SUBLANES, tn), F32)]
        if kmean_src is not None:
            cache_pages, page_table, layer_page0 = kmean_src
            n_seq, n_pages = page_table.shape
            pages_per_blk = MOBA_BLOCK // LANES
            steps = (m // tm) * nj
            chunks_per_seq = steps // n_seq
            assert chunks_per_seq >= 1
            pages_per_step = -(-n_pages // (chunks_per_seq * pages_per_blk)) * pages_per_blk
            chunks_per_seq = -(-n_pages // pages_per_step)
            assert chunks_per_seq * pages_per_step // pages_per_blk <= LANES
            kmean = dict(pages_per_step=pages_per_step, chunks_per_seq=chunks_per_seq, n_seq=n_seq, heads=MOBA_HEADS,
                         every_step=n_seq * chunks_per_seq == steps)
            seq_of = lambda i, j: jnp.minimum((i * nj + j) // chunks_per_seq, n_seq - 1)

            def page_map(u):
                def fn(i, j, pt):
                    pg = jnp.minimum(((i * nj + j) % chunks_per_seq) * pages_per_step + u, n_pages - 1)
                    return (layer_page0 + pt[seq_of(i, j) * n_pages + pg], 0, 0)
                return fn

            in_specs += [pl.BlockSpec((None,) + cache_pages.shape[1:], page_map(u)) for u in range(pages_per_step)]
            args += [cache_pages] * pages_per_step
            prefetch = [page_table.reshape(-1)]
            w_heads = MOBA_HEADS * HEAD_DIM
            out_specs += [pl.BlockSpec((None, LANES, w_heads), lambda i, j, pt: (seq_of(i, j), 0, 0))]
            out_shape += [jax.ShapeDtypeStruct((n_seq, LANES, w_heads), F32)]
    kern = functools.partial(_ffn_up_kernel, tiles_per_seq=tiles_per_seq, per_row_state=per_row_state, kmean=kmean)
    return pl.pallas_call(
        kern,
        grid_spec=pltpu.PrefetchScalarGridSpec(
            num_scalar_prefetch=len(prefetch), grid=(m // tm, nj), in_specs=in_specs, out_specs=out_specs,
            scratch_shapes=scratch),
        out_shape=out_shape, compiler_params=_params(("arbitrary", "arbitrary")), name="ffn_up",
    )(*prefetch, *args)


def _ffn_down_kernel(*refs, final):
    if final:
        g_ref, w_ref, x_ref, gf_ref, o_ref, y_ref = refs
    else:
        g_ref, w_ref, x_ref, o_ref = refs
    xo = x_ref[...] + _dot(g_ref[...], w_ref[...])
    o_ref[...] = xo
    if final:
        y_ref[...] = _rms(xo, gf_ref[...])


def _ffn_down(gact, w, x, g_final, *, layer, tm):
    m, d = x.shape
    f = w.shape[1]
    final = g_final is not None
    rows = lambda wd: pl.BlockSpec((tm, wd), lambda i: (i, 0))
    in_specs = [rows(f), _resident(w, layer), rows(d)]
    args = [gact, w, x]
    out_specs = [rows(d)]
    out_shape = [jax.ShapeDtypeStruct((m, d), F32)]
    if final:
        in_specs += [pl.BlockSpec((1, d), lambda i: (0, 0))]
        args += [g_final]
        out_specs += [rows(d)]
        out_shape += [jax.ShapeDtypeStruct((m, d), F32)]
    res = pl.pallas_call(
        functools.partial(_ffn_down_kernel, final=final),
        grid=(m // tm,), in_specs=in_specs, out_specs=out_specs, out_shape=out_shape,
        compiler_params=_params(("arbitrary",)), name="ffn_down",
    )(*args)
    return (res[0], res[1]) if final else (res[0], None)


def _head_mask(rows, width):
    r = lax.broadcasted_iota(jnp.int32, (rows, width), 0)
    c = lax.broadcasted_iota(jnp.int32, (rows, width), 1)
    return (c // HEAD_DIM) == r


def _page_heads(page_ref, kv, heads):
    return jnp.concatenate(
        [page_ref[pl.ds(2 * h + kv, LANES, stride=2 * heads), :] for h in range(heads)], axis=1)


def _fox_dec_kernel(pt_ref, q_ref, kvn_ref, lfn_ref, lf_ref, *refs, pages_per_step, heads, n_pages):
    page_refs = refs[:pages_per_step]
    o_ref, m_scr, l_scr, acc_scr, run_scr = refs[pages_per_step:]
    b = pl.program_id(0)
    c = pl.program_id(1)
    w = heads * HEAD_DIM
    hm = _head_mask(SUBLANES, w)
    qbd = jnp.where(hm, jnp.broadcast_to(q_ref[...], (SUBLANES, w)), 0.0)

    @pl.when(c == 0)
    def _():
        kvn = kvn_ref[...]
        m_scr[...] = jnp.sum(qbd * kvn[:, :w], axis=1, keepdims=True)
        l_scr[...] = jnp.ones_like(l_scr)
        acc_scr[...] = jnp.broadcast_to(kvn[:, w:], (SUBLANES, w))
        run_scr[...] = lfn_ref[...]

    r = lax.broadcasted_iota(jnp.int32, (LANES, LANES), 0)
    cc = lax.broadcasted_iota(jnp.int32, (LANES, LANES), 1)
    upper = (r > cc).astype(BF16)
    qb = qbd.astype(BF16)
    run = run_scr[...]
    pad = jnp.zeros((SUBLANES - heads, LANES), F32)
    ss = []
    for u in range(pages_per_step):
        pid = pt_ref[b * n_pages + (n_pages - 1 - (c * pages_per_step + u))]
        lf = jnp.concatenate([lf_ref[h, pl.ds(pid, 1), :] for h in range(heads)] + [pad], axis=0)
        s = _dot_nt(qb, _page_heads(page_refs[u], 0, heads).astype(BF16))
        suffix = sum(_dot(part, upper) for part in _split3(lf))
        ss.append(s + run + suffix)
        run = run + jnp.sum(lf, axis=1, keepdims=True)
    s = jnp.concatenate(ss, axis=1)
    m_old = m_scr[...]
    m_new = jnp.maximum(m_old, jnp.max(s, axis=1, keepdims=True))
    alpha = jnp.exp(m_old - m_new)
    p = jnp.exp(s - m_new).astype(BF16)
    l_scr[...] = alpha * l_scr[...] + jnp.sum(p.astype(F32), axis=1, keepdims=True)
    pv = sum(_dot(p[:, u * LANES:(u + 1) * LANES], _page_heads(page_refs[u], 1, heads).astype(BF16))
             for u in range(pages_per_step))
    acc_scr[...] = alpha * acc_scr[...] + pv
    m_scr[...] = m_new
    run_scr[...] = run

    @pl.when(c == pl.num_programs(1) - 1)
    def _():
        o = jnp.where(hm, acc_scr[...] * (1.0 / l_scr[...]), 0.0)
        o_ref[...] = jnp.sum(o, axis=0, keepdims=True).astype(o_ref.dtype)


def _fox_decode(q, kv_new, lf_new, cache_pages, cache_lf, page_table, layer, *, pages_per_step):
    db, n_pages = page_table.shape
    heads = FOX_HEADS
    w = heads * HEAD_DIM
    n_phys = cache_lf.shape[2]
    steps = n_pages // pages_per_step

    def page_map(u):
        def f(b, c, pt):
            return (layer * n_phys + pt[b * n_pages + (n_pages - 1 - (c * pages_per_step + u))], 0, 0)
        return f

    in_specs = [pl.BlockSpec((None, 1, w), lambda b, c, pt: (b, 0, 0)),
                pl.BlockSpec((None, 1, 2 * w), lambda b, c, pt: (b, 0, 0)),
                pl.BlockSpec((None, SUBLANES, 1), lambda b, c, pt: (b, 0, 0)),
                pl.BlockSpec((None, heads, n_phys, LANES), lambda b, c, pt: (layer, 0, 0, 0))]
    in_specs += [pl.BlockSpec((None, 2 * w, LANES), page_map(u)) for u in range(pages_per_step)]
    kern = functools.partial(_fox_dec_kernel, pages_per_step=pages_per_step, heads=heads, n_pages=n_pages)
    return pl.pallas_call(
        kern,
        grid_spec=pltpu.PrefetchScalarGridSpec(
            num_scalar_prefetch=1, grid=(db, steps), in_specs=in_specs,
            out_specs=pl.BlockSpec((None, 1, w), lambda b, c, pt: (b, 0, 0)),
            scratch_shapes=[pltpu.VMEM((SUBLANES, 1), F32), pltpu.VMEM((SUBLANES, 1), F32),
                            pltpu.VMEM((SUBLANES, w), F32), pltpu.VMEM((SUBLANES, 1), F32)]),
        out_shape=jax.ShapeDtypeStruct((db, 1, w), BF16),
        compiler_params=_params(("arbitrary", "arbitrary")),
        name="fox_decode",
    )(page_table.reshape(-1), q, kv_new, lf_new, cache_lf, *([cache_pages] * pages_per_step))


def _moba_topk_kernel(q_ref, km_ref, sel_ref, *, heads, n_blk):
    w = heads * HEAD_DIM
    hm = _head_mask(SUBLANES, w)
    qbd = jnp.where(hm, jnp.broadcast_to(q_ref[...], (SUBLANES, w)), 0.0)
    km = km_ref[...]
    bs = jnp.zeros((SUBLANES, LANES), F32)
    for qp in _split3(qbd):
        for kp in _split3(km):
            bs = bs + _dot_nt(qp, kp)
    lane = lax.broadcasted_iota(jnp.int32, (SUBLANES, LANES), 1)
    bs = jnp.where(lane < n_blk, bs, -jnp.inf)
    out = jnp.zeros((SUBLANES, LANES), jnp.int32)
    for t in range(MOBA_TOPK):
        mx = jnp.max(bs, axis=1, keepdims=True)
        idx = jnp.min(jnp.where(bs == mx, lane, LANES), axis=1, keepdims=True)
        out = jnp.where(lane == t, idx, out)
        bs = jnp.where(lane == idx, -jnp.inf, bs)
    sel_ref[...] = out


def _moba_topk(q, km, n_blk):
    db = q.shape[0]
    heads = MOBA_HEADS
    w = heads * HEAD_DIM
    assert MOBA_TOPK <= n_blk <= LANES
    return pl.pallas_call(
        functools.partial(_moba_topk_kernel, heads=heads, n_blk=n_blk),
        grid=(db,),
        in_specs=[pl.BlockSpec((None, 1, w), lambda b: (b, 0, 0)),
                  pl.BlockSpec((None, LANES, w), lambda b: (b, 0, 0))],
        out_specs=pl.BlockSpec((None, SUBLANES, LANES), lambda b: (b, 0, 0)),
        out_shape=jax.ShapeDtypeStruct((db, SUBLANES, LANES), jnp.int32),
        compiler_params=_params(("arbitrary",)),
        name="moba_topk",
    )(q, km)


def _moba_dec_kernel(sel_ref, pt_ref, q_ref, kn_ref, vn_ref, *refs, n_pages_sel):
    del sel_ref, pt_ref
    page_refs = refs[:n_pages_sel]
    o_ref = refs[n_pages_sel]
    q = q_ref[...]
    qb = jnp.broadcast_to(q, (SUBLANES, HEAD_DIM)).astype(BF16)
    s_self = jnp.sum(q * kn_ref[...], axis=1, keepdims=True)
    s = jnp.concatenate([_dot_nt(qb, page_refs[u][:, 0, :].astype(BF16)) for u in range(n_pages_sel)], axis=1)
    m = jnp.maximum(s_self, jnp.max(s, axis=1, keepdims=True))
    p_self = jnp.exp(s_self - m)
    p = jnp.exp(s - m).astype(BF16)
    l = p_self + jnp.sum(p.astype(F32), axis=1, keepdims=True)
    acc = p_self * vn_ref[...]
    for u in range(n_pages_sel):
        acc = acc + _dot(p[:, u * LANES:(u + 1) * LANES], page_refs[u][:, 1, :].astype(BF16))
    o_ref[...] = (acc * (1.0 / l))[0:1, :].astype(o_ref.dtype)


def _moba_decode(q, kv_new, sel, cache_rows, page_table, layer_page0):
    db, n_pages = page_table.shape
    heads = MOBA_HEADS
    pages_per_blk = MOBA_BLOCK // LANES
    n_pages_sel = MOBA_TOPK * pages_per_blk

    def page_map(u):
        jsel, pg = divmod(u, pages_per_blk)

        def f(b, h, sel_s, pt):
            blk = sel_s[(b * heads + h) * MOBA_TOPK + jsel]
            return (layer_page0 + pt[b * n_pages + blk * pages_per_blk + pg], h, 0, 0)
        return f

    one = lambda col0: pl.BlockSpec((None, 1, HEAD_DIM), lambda b, h, sel_s, pt: (b, 0, col0 + h))
    in_specs = [one(0), one(0), one(heads)]
    in_specs += [pl.BlockSpec((LANES, None, 2, HEAD_DIM), page_map(u)) for u in range(n_pages_sel)]
    kern = functools.partial(_moba_dec_kernel, n_pages_sel=n_pages_sel)
    return pl.pallas_call(
        kern,
        grid_spec=pltpu.PrefetchScalarGridSpec(
            num_scalar_prefetch=2, grid=(db, heads), in_specs=in_specs,
            out_specs=pl.BlockSpec((None, 1, HEAD_DIM), lambda b, h, sel_s, pt: (b, 0, h))),
        out_shape=jax.ShapeDtypeStruct((db, 1, heads * HEAD_DIM), BF16),
        compiler_params=_params(("arbitrary", "arbitrary")),
        name="moba_decode",
    )(sel.reshape(-1), page_table.reshape(-1), q, kv_new, kv_new, *([cache_rows] * n_pages_sel))


def _mem_dec_kernel(q_ref, kv_ref, o_ref, *, heads, n_mem):
    outs = []
    for h in range(heads):
        qb = jnp.broadcast_to(q_ref[:, _head_cols(h)], (SUBLANES, HEAD_DIM)).astype(BF16)
        k = kv_ref[pl.ds(h, n_mem, stride=2 * heads), :].astype(BF16)
        v = kv_ref[pl.ds(heads + h, n_mem, stride=2 * heads), :].astype(BF16)
        s = _dot_nt(qb, k)
        p = jnp.exp(s - jnp.max(s, axis=1, keepdims=True))
        l = jnp.sum(p, axis=1, keepdims=True)
        outs.append((_dot(p.astype(BF16), v) * (1.0 / l))[0:1, :])
    o_ref[...] = jnp.concatenate(outs, axis=1).astype(o_ref.dtype)


def _mem_decode(q, mem_rows, layer, *, q_col_block):
    db = q.shape[0]
    heads = MEM_HEADS
    n_mem = mem_rows.shape[1] // (2 * heads)
    return pl.pallas_call(
        functools.partial(_mem_dec_kernel, heads=heads, n_mem=n_mem),
        grid=(db,),
        in_specs=[pl.BlockSpec((None, 1, HEAD_TILE), lambda b: (b, 0, q_col_block)),
                  pl.BlockSpec((None, mem_rows.shape[1], HEAD_DIM), lambda b: (layer * db + b, 0, 0))],
        out_specs=pl.BlockSpec((None, 1, heads * HEAD_DIM), lambda b: (b, 0, 0)),
        out_shape=jax.ShapeDtypeStruct((db, 1, heads * HEAD_DIM), BF16),
        compiler_params=_params(("arbitrary",)),
        name="mem_decode",
    )(q, mem_rows)


def _rope_tables(pos):
    half = HEAD_DIM // 2
    inv_freq = ROPE_THETA ** (-jnp.arange(half, dtype=F32) / half)
    ang = pos.astype(F32)[:, None] * inv_freq[None, :]
    cos, sin = jnp.cos(ang), jnp.sin(ang)
    return jnp.concatenate([cos, cos], axis=-1), jnp.concatenate([-sin, sin], axis=-1)


def _pad_cols(a, n):
    return jnp.pad(a, ((0, 0), (0, n - a.shape[1])))


def _prep_w_in(w_in, d):
    mw, fw, cw = MOBA_HEADS * HEAD_DIM, FOX_HEADS * HEAD_DIM, MEM_HEADS * HEAD_DIM
    o_fl = 3 * mw + 3 * fw
    o_qc = o_fl + FOX_HEADS
    o_gl = o_qc + cw
    layers, k, n = w_in.shape
    assert n == o_gl + 3 * d and cw % W_IN_COLS == 0 and (3 * d) % W_IN_COLS == 0 and o_fl % W_IN_COLS == 0
    w_t = w_in.reshape(layers, k // LANES, LANES, n).transpose(3, 1, 0, 2)
    starts = list(range(0, o_fl, W_IN_COLS)) + list(range(o_qc, o_gl, W_IN_COLS))
    starts += [o_gl] * ((HEAD_TILE - cw) // W_IN_COLS)
    starts += list(range(o_gl, n, W_IN_COLS))
    w_main = _w_in_layout(w_t, starts, W_IN_COLS)
    w_fl = _w_in_layout(w_t, [o_fl], LANES)
    return w_main, w_fl


def _prep_layer(l, g_attn, b_f, g_mem, g_ffn, w_conv, b_conv, fp):
    f = w_conv.shape[2]
    return dict(
        g_attn=g_attn[l][None], b_f=_pad_cols(b_f[l][None], LANES), g_mem=g_mem[l][None], g_ffn=g_ffn[l][None],
        w_conv=jnp.pad(w_conv[l], ((0, SUBLANES - CONV_W), (0, fp - f))), b_conv=_pad_cols(b_conv[l][None], fp))


def kernel(x_prompt, x_sample, cache_moba_kv, cache_fox_kv, cache_fox_logf, cache_mem_kv, state_ffn_conv,
           page_table, mem_prompt, g_attn, w_in, b_f, w_br_moba, w_br_fox, w_br_mem, w_out, g_mem,
           w_mem_kv, g_ffn, w_up, w_conv, b_conv, w_down, g_final):
    bsz, seq, d = x_prompt.shape
    db = x_sample.shape[0]
    depth, n_phys, page = cache_moba_kv.shape[:3]
    n_pages = page_table.shape[1]
    past_len = n_pages * page
    n_mem = mem_prompt.shape[1]
    mw, fw, cw = MOBA_HEADS * HEAD_DIM, FOX_HEADS * HEAD_DIM, MEM_HEADS * HEAD_DIM
    assert page == LANES and db == SUBLANES and x_sample.shape[1] == 1
    ff_tile = 512
    tm = max(t for t in (1024, 512, 256) if seq % t == 0)
    tq = MOBA_BLOCK

    cos_p, sin_p = _rope_tables(jnp.arange(seq))
    cos_s, sin_s = _rope_tables(jnp.full((db,), past_len))
    gf = g_final[None]

    moba_t = jnp.transpose(cache_moba_kv, (0, 1, 2, 4, 3, 5))
    moba_pages = moba_t.reshape(depth * n_phys, page * 2 * MOBA_HEADS, HEAD_DIM)
    moba_rows = moba_t.reshape(depth * n_phys * page, MOBA_HEADS, 2, HEAD_DIM)
    fox_pages = jnp.transpose(cache_fox_kv, (0, 1, 2, 4, 3, 5)).reshape(depth * n_phys, page * 2 * FOX_HEADS, HEAD_DIM)
    lf_cache = jnp.transpose(cache_fox_logf, (0, 3, 1, 2))
    mem_rows = cache_mem_kv.reshape(depth * db, n_mem * 2 * MEM_HEADS, HEAD_DIM)

    f = w_down.shape[1]
    fp = -(-f // ff_tile) * ff_tile
    w_main, w_fl = _prep_w_in(w_in, d)
    wm, wf, wc = _cast_bf16(w_br_moba, mw), _cast_bf16(w_br_fox, fw), _cast_bf16(w_br_mem, cw)
    wo, w_mem = _cast_bf16(w_out, 512), _cast_bf16(w_mem_kv, 512)
    wa, wb = _cast_w_up(w_up, fp, 256)
    wdn = _cast_bf16(w_down, f // SUBLANES)
    moba_out = jnp.zeros((depth, bsz, seq, MOBA_HEADS, 2, HEAD_DIM), F32)
    fox_out = jnp.zeros((depth, bsz, seq, FOX_HEADS, 2, HEAD_DIM), F32)
    xp = x_prompt.reshape(bsz * seq, d)
    xs = x_sample.reshape(db, d)
    outs = {k: [] for k in ("p_logf", "p_mem", "p_conv", "s_moba", "s_fox", "s_logf", "s_conv")}
    yp = ys = None
    for l in range(depth):
        w = _prep_layer(l, g_attn, b_f, g_mem, g_ffn, w_conv, b_conv, fp)
        g_last = gf if l == depth - 1 else None

        mkv = _norm_mm(mem_prompt.reshape(bsz * n_mem, d), w["g_mem"], w_mem, layer=l, tm=min(512, bsz * n_mem))
        q_all, moba_kv, fox_kv, gates, logf, c = _in_proj(
            xp, w["g_attn"], w_main, w_fl, w["b_f"], cos_p, sin_p, layer=l, tm=tm, seq_len=seq, q_dtype=BF16,
            q_scale=SCALE * LOG2E)
        q3 = q_all.reshape(bsz, seq, 3 * HEAD_TILE)
        c3 = c.reshape(bsz, seq, LANES)
        o_m, moba_out = _causal_attention(q3, moba_kv.reshape(bsz, seq, 2 * mw), mode="moba", heads=MOBA_HEADS,
                                          q_col0=0, layer=l, kv_out=moba_out)
        o_f, fox_out = _causal_attention(q3, fox_kv.reshape(bsz, seq, 2 * fw), mode="fox", heads=FOX_HEADS,
                                         q_col0=MOBA_HEADS, layer=l, kv_out=fox_out, c=c3)
        o_c = _mem_attention(q3, mkv.reshape(bsz, n_mem, 2 * cw), heads=MEM_HEADS, q_col_block=2, tq=tq)
        xp = _merge_out(o_m.reshape(bsz * seq, mw), o_f.reshape(bsz * seq, fw), o_c.reshape(bsz * seq, cw),
                        gates, xp, wm, wf, wc, wo, layer=l, tm=256)
        gact, tails, km = _ffn_up(xp, w["g_ffn"], wa, wb, w["w_conv"], w["b_conv"], layer=l, tm=tm, tn=ff_tile,
                                  seq_len=seq, kmean_src=(moba_pages, page_table, l * n_phys))
        xp, yp = _ffn_down(gact, wdn, xp, g_last, layer=l, tm=256)
        outs["p_logf"].append(logf[:, :FOX_HEADS].reshape(bsz, seq, FOX_HEADS))
        outs["p_mem"].append(mkv.reshape(bsz, n_mem, 2, MEM_HEADS, HEAD_DIM))
        tiles_per_seq = seq // tm
        outs["p_conv"].append(tails[tiles_per_seq - 1::tiles_per_seq, SUBLANES - (CONV_W - 1):, :f])

        q_s, moba_s, fox_s, gates_s, logf_s, _ = _in_proj(
            xs, w["g_attn"], w_main, w_fl, w["b_f"], cos_s, sin_s, layer=l, tm=db, seq_len=1, q_dtype=F32,
            q_scale=SCALE)
        q_s3 = q_s.reshape(db, 1, 3 * HEAD_TILE)
        sel = _moba_topk(q_s3, km, past_len // MOBA_BLOCK)
        o_ms = _moba_decode(q_s3, moba_s.reshape(db, 1, 2 * mw), sel[:, :MOBA_HEADS, :MOBA_TOPK],
                            moba_rows, page_table, l * n_phys)
        o_fs = _fox_decode(q_s3[:, :, mw:mw + fw], fox_s.reshape(db, 1, 2 * fw),
                           logf_s[:, :SUBLANES].reshape(db, SUBLANES, 1), fox_pages, lf_cache, page_table,
                           l, pages_per_step=16)
        o_cs = _mem_decode(q_s3, mem_rows, l, q_col_block=2)
        xs = _merge_out(o_ms.reshape(db, mw), o_fs.reshape(db, fw), o_cs.reshape(db, cw), gates_s, xs,
                        wm, wf, wc, wo, layer=l, tm=db)
        prev = jnp.pad(state_ffn_conv[l], ((0, 0), (0, 0), (0, fp - f)))
        gact_s, a_s = _ffn_up(xs, w["g_ffn"], wa, wb, w["w_conv"], w["b_conv"],
                              layer=l, tm=db, tn=ff_tile, seq_len=1, prev=(prev[:, 0], prev[:, 1]))
        xs, ys = _ffn_down(gact_s, wdn, xs, g_last, layer=l, tm=db)
        outs["s_moba"].append(moba_s.reshape(db, 1, 2, MOBA_HEADS, HEAD_DIM))
        outs["s_fox"].append(fox_s.reshape(db, 1, 2, FOX_HEADS, HEAD_DIM))
        outs["s_logf"].append(logf_s[:, :FOX_HEADS].reshape(db, 1, FOX_HEADS))
        outs["s_conv"].append(jnp.stack([state_ffn_conv[l][:, 1], a_s[:, :f]], axis=1))

    st = lambda k: jnp.stack(outs[k])
    kv_order = (0, 1, 2, 4, 3, 5)
    return (yp.reshape(bsz, seq, d), ys.reshape(db, 1, d),
            jnp.transpose(moba_out, kv_order), jnp.transpose(fox_out, kv_order),
            st("p_logf"), st("p_mem"), st("p_conv"),
            st("s_moba"), st("s_fox"), st("s_logf"), st("s_conv"))
```

```python
import functools

import jax
import jax.numpy as jnp
from jax import lax
from jax.experimental import pallas as pl
from jax.experimental.pallas import tpu as pltpu

F32 = jnp.float32
BF16 = jnp.bfloat16

HEAD_DIM = 128
MOBA_HEADS = 6
FOX_HEADS = 6
MEM_HEADS = 4
MOBA_BLOCK = 256
MOBA_TOPK = 3
CONV_W = 3
ROPE_THETA = 10000.0
NORM_EPS = 1e-6
NEG = -1e30

LANES = 128
SUBLANES = 8
HEAD_TILE = MOBA_HEADS * HEAD_DIM
VMEM_LIMIT = 56 * 1024 * 1024
ROW_CHUNK = 256
SCALE = HEAD_DIM ** -0.5
LOG2E = 1.4426950408889634

_NT = (((1,), (1,)), ((), ()))


def _params(sem):
    return pltpu.CompilerParams(dimension_semantics=sem, vmem_limit_bytes=VMEM_LIMIT)


def _rms(x, g):
    return x * lax.rsqrt(jnp.mean(x * x, axis=-1, keepdims=True) + NORM_EPS) * g


def _log_sigmoid(x):
    return jnp.minimum(x, 0.0) - jnp.log1p(jnp.exp(-jnp.abs(x)))


def _split3(v):
    hi = v.astype(BF16)
    r1 = v - hi.astype(F32)
    mid = r1.astype(BF16)
    lo = (r1 - mid.astype(F32)).astype(BF16)
    return hi, mid, lo


def _dot(a, b):
    return jnp.dot(a, b, preferred_element_type=F32)


def _dot_nt(a, b):
    return lax.dot_general(a, b, _NT, preferred_element_type=F32)


def _cumsum_rows(v, carry, blk=256):
    tm = v.shape[0]
    blk = min(blk, tm)
    r = lax.broadcasted_iota(jnp.int32, (blk, blk), 0)
    c = lax.broadcasted_iota(jnp.int32, (blk, blk), 1)
    tri = (c <= r).astype(BF16)
    outs = []
    for s in range(0, tm, blk):
        hi, mid, lo = _split3(v[s:s + blk])
        cs = _dot(tri, hi) + _dot(tri, mid) + _dot(tri, lo) + carry
        outs.append(cs)
        carry = cs[blk - 1:blk, :]
    return jnp.concatenate(outs, axis=0), carry


def _rope(acc, cos, sin_signed):
    outs = []
    for hh in range(acc.shape[1] // HEAD_DIM):
        xh = acc[:, hh * HEAD_DIM:(hh + 1) * HEAD_DIM]
        outs.append(xh * cos + pltpu.roll(xh, HEAD_DIM // 2, 1) * sin_signed)
    return jnp.concatenate(outs, axis=1)


def _head_cols(h):
    return slice(h * HEAD_DIM, (h + 1) * HEAD_DIM)


W_IN_COLS = 256


def _w_in_kernel(starts_ref, x_ref, o_ref):
    del starts_ref
    cols, kt, layers, _ = x_ref.shape
    for l in range(layers):
        for t in range(kt):
            o_ref[l, t * LANES:(t + 1) * LANES, :] = x_ref[:, t, l, :].T.astype(o_ref.dtype)


def _w_in_layout(w_t, starts, cols):
    _, kt, layers, _ = w_t.shape
    starts = jnp.asarray(starts, jnp.int32)
    return pl.pallas_call(
        _w_in_kernel,
        grid_spec=pltpu.PrefetchScalarGridSpec(
            num_scalar_prefetch=1, grid=(starts.shape[0],),
            in_specs=[pl.BlockSpec((pl.Element(cols), pl.Element(kt), pl.Element(layers), pl.Element(LANES)),
                                   lambda j, st: (st[j], 0, 0, 0))],
            out_specs=pl.BlockSpec((layers, kt * LANES, cols), lambda j, st: (0, 0, j))),
        out_shape=jax.ShapeDtypeStruct((layers, kt * LANES, starts.shape[0] * cols), BF16),
        compiler_params=_params(("arbitrary",)),
        name="w_in_layout",
    )(starts, w_t)


def _cast_kernel(x_ref, o_ref):
    o_ref[...] = x_ref[...].astype(o_ref.dtype)


def _cast_bf16(w, rows):
    layers, r, c = w.shape
    rows = min(rows, r)
    assert r % rows == 0
    spec = pl.BlockSpec((None, rows, c), lambda l, i: (l, i, 0))
    return pl.pallas_call(
        _cast_kernel, grid=(layers, r // rows), in_specs=[spec], out_specs=spec,
        out_shape=jax.ShapeDtypeStruct(w.shape, BF16), compiler_params=_params(("arbitrary", "arbitrary")),
        name="cast_bf16")(w)


def _cast_up_kernel(x_ref, a_ref, b_ref, *, f):
    for o_ref, c0 in ((a_ref, 0), (b_ref, f)):
        o_ref[:, :f] = x_ref[:, c0:c0 + f].astype(o_ref.dtype)
        o_ref[:, f:] = jnp.zeros((o_ref.shape[0], o_ref.shape[1] - f), o_ref.dtype)


def _cast_w_up(w_up, fp, rows):
    layers, d, f2 = w_up.shape
    out = pl.BlockSpec((None, rows, fp), lambda l, i: (l, i, 0))
    return pl.pallas_call(
        functools.partial(_cast_up_kernel, f=f2 // 2), grid=(layers, d // rows),
        in_specs=[pl.BlockSpec((None, rows, f2), lambda l, i: (l, i, 0))], out_specs=[out, out],
        out_shape=[jax.ShapeDtypeStruct((layers, d, fp), BF16)] * 2,
        compiler_params=_params(("arbitrary", "arbitrary")), name="cast_w_up")(w_up)


def _in_proj_kernel(x_ref, g_ref, w_ref, wfl_ref, bf_ref, cos_ref, sin_ref,
                    q_ref, mkv_ref, fkv_ref, gate_ref, logf_ref, c_ref,
                    h_scr, carry_scr, *, tiles_per_seq, cumsum, q_scale):
    i = pl.program_id(0)
    j = pl.program_id(1)

    @pl.when(j == 0)
    def _():
        h = _rms(x_ref[...], g_ref[...]).astype(BF16)
        h_scr[...] = h
        logf = _log_sigmoid(_dot(h, wfl_ref[...]) + bf_ref[...])
        logf_ref[...] = logf
        if cumsum:
            @pl.when(i % tiles_per_seq == 0)
            def _():
                carry_scr[...] = jnp.zeros_like(carry_scr)
            c, carry = _cumsum_rows(logf, carry_scr[...])
            c_ref[...] = c
            carry_scr[...] = carry
        else:
            c_ref[...] = logf

    tm = x_ref.shape[0]
    rc = min(ROW_CHUNK, tm)

    def tile(out_ref, epilogue):
        for r0 in range(0, tm, rc):
            rows = slice(r0, r0 + rc)
            acc = _dot(h_scr[rows, :], w_ref[...])
            out_ref[rows, :] = epilogue(acc, rows).astype(out_ref.dtype)

    rope = lambda acc, rows: _rope(acc, cos_ref[rows, :], sin_ref[rows, :])

    @pl.when(j == 0)
    def _():
        tile(q_ref, lambda acc, rows: rope(acc, rows) * q_scale)

    @pl.when(j == 1)
    def _():
        tile(mkv_ref, rope)

    @pl.when(j == 2)
    def _():
        tile(mkv_ref, lambda acc, rows: acc)

    @pl.when(jnp.logical_or(j == 3, j == 6))
    def _():
        tile(q_ref, lambda acc, rows: acc * q_scale)

    @pl.when(jnp.logical_or(j == 4, j == 5))
    def _():
        tile(fkv_ref, lambda acc, rows: acc)

    @pl.when(j >= 7)
    def _():
        tile(gate_ref, lambda acc, rows: jax.nn.sigmoid(acc))


def _in_proj(x, g, w_main, w_fl, b_f, cos, sin, *, layer, tm, seq_len, q_dtype, q_scale):
    m, d = x.shape
    n_tiles = w_main.shape[2] // HEAD_TILE
    n_gate = n_tiles - 7
    table_tiles = cos.shape[0] // tm
    cumsum = seq_len > 1
    tiles_per_seq = max(seq_len // tm, 1)
    kern = functools.partial(_in_proj_kernel, tiles_per_seq=tiles_per_seq, cumsum=cumsum, q_scale=q_scale)
    row = lambda i, j: (i, 0)
    return pl.pallas_call(
        kern,
        grid=(m // tm, n_tiles),
        in_specs=[
            pl.BlockSpec((tm, d), row),
            pl.BlockSpec((1, d), lambda i, j: (0, 0)),
            pl.BlockSpec((None, d, HEAD_TILE), lambda i, j: (layer, 0, j)),
            pl.BlockSpec((None, d, LANES), lambda i, j: (layer, 0, 0)),
            pl.BlockSpec((1, LANES), lambda i, j: (0, 0)),
            pl.BlockSpec((tm, LANES), lambda i, j: (i % table_tiles, 0)),
            pl.BlockSpec((tm, LANES), lambda i, j: (i % table_tiles, 0)),
        ],
        out_specs=[
            pl.BlockSpec((tm, HEAD_TILE), lambda i, j: (i, jnp.minimum(j // 3, 2))),
            pl.BlockSpec((tm, HEAD_TILE), lambda i, j: (i, jnp.clip(j - 1, 0, 1))),
            pl.BlockSpec((tm, HEAD_TILE), lambda i, j: (i, jnp.clip(j - 4, 0, 1))),
            pl.BlockSpec((tm, HEAD_TILE), lambda i, j: (i, jnp.clip(j - 7, 0, n_gate - 1))),
            pl.BlockSpec((tm, LANES), row),
            pl.BlockSpec((tm, LANES), row),
        ],
        out_shape=[
            jax.ShapeDtypeStruct((m, 3 * HEAD_TILE), q_dtype),
            jax.ShapeDtypeStruct((m, 2 * HEAD_TILE), F32),
            jax.ShapeDtypeStruct((m, 2 * HEAD_TILE), F32),
            jax.ShapeDtypeStruct((m, n_gate * HEAD_TILE), BF16),
            jax.ShapeDtypeStruct((m, LANES), F32),
            jax.ShapeDtypeStruct((m, LANES), F32),
        ],
        scratch_shapes=[pltpu.VMEM((tm, d), BF16), pltpu.VMEM((1, LANES), F32)],
        compiler_params=_params(("arbitrary", "arbitrary")),
        name="in_proj",
    )(x, g, w_main, w_fl, b_f, cos, sin)


def _norm_mm_kernel(x_ref, g_ref, w_ref, o_ref):
    h = _rms(x_ref[...], g_ref[...]).astype(BF16)
    o_ref[...] = _dot(h, w_ref[...])


def _norm_mm(x, g, w, *, layer, tm):
    m, d = x.shape
    n = w.shape[2]
    return pl.pallas_call(
        _norm_mm_kernel,
        grid=(m // tm,),
        in_specs=[pl.BlockSpec((tm, d), lambda i: (i, 0)),
                  pl.BlockSpec((1, d), lambda i: (0, 0)),
                  pl.BlockSpec((None, d, n), lambda i: (layer, 0, 0))],
        out_specs=pl.BlockSpec((tm, n), lambda i: (i, 0)),
        out_shape=jax.ShapeDtypeStruct((m, n), F32),
        compiler_params=_params(("arbitrary",)),
        name="mem_kv_proj",
    )(x, g, w)


MASK_BIAS = 1e30


def _causal_attn_kernel(*refs, mode, blk, n_blk, slots):
    q_ref, k_ref, v_ref = refs[:3]
    c_ref = refs[3] if mode == "fox" else None
    o_ref, kv_out_ref, sem = refs[-3:]
    b = pl.program_id(0)
    h = pl.program_id(1)
    s_len = q_ref.shape[0]
    copies = [pltpu.make_async_copy(src.at[0], kv_out_ref.at[slot, b, :, h, kv, :], sem.at[si, kv])
              for si, slot in enumerate(slots) for kv, src in enumerate((k_ref, v_ref))]
    for cp in copies:
        cp.start()
    q = q_ref[...]
    kf = k_ref[0]
    vf = v_ref[0]
    vb = jnp.concatenate([vf.astype(BF16), jnp.ones((s_len, HEAD_DIM), BF16)], axis=1)
    lane = lax.broadcasted_iota(jnp.int32, (s_len, LANES), 1)

    if mode == "fox":
        c_col = jnp.sum(jnp.where(lane == h, c_ref[...], 0.0), axis=1, keepdims=True) * LOG2E
        hi, mid, lo = [part.astype(F32) for part in _split3(c_col)]
        aug_q = jnp.where(lane == 0, hi, jnp.where(lane == 1, mid, jnp.where(lane == 2, lo,
                          jnp.where(lane < 6, 1.0, 0.0))))
        aug_k = jnp.where(lane < 3, 1.0, jnp.where(lane == 3, -hi, jnp.where(lane == 4, -mid,
                          jnp.where(lane == 5, -lo, 0.0))))
    else:
        nb8 = -(-n_blk // SUBLANES) * SUBLANES
        km = [jnp.sum(kf[n * blk:(n + 1) * blk], axis=0, keepdims=True) * (1.0 / blk) for n in range(n_blk)]
        km = jnp.concatenate(km + [jnp.zeros((1, HEAD_DIM), F32)] * (nb8 - n_blk), axis=0)
        bs = sum(_dot_nt(part, q) for part in _split3(km))
        blk_id = lax.broadcasted_iota(jnp.int32, (nb8, s_len), 0)
        own = lax.broadcasted_iota(jnp.int32, (nb8, s_len), 1) // blk
        cnt = jnp.zeros((nb8, s_len), F32)
        for mth in range(n_blk - 1):
            sm = bs[mth:mth + 1, :]
            beats = jnp.logical_or(sm > bs, jnp.logical_and(sm == bs, mth < blk_id))
            cnt = cnt + jnp.where(jnp.logical_and(beats, mth < own), 1.0, 0.0)
        sel = jnp.where(jnp.logical_and(blk_id < own, cnt < MOBA_TOPK), 1.0, 0.0)
        sel = jnp.concatenate([sel, jnp.zeros((LANES - nb8, s_len), F32)], axis=0).astype(BF16)
        eye = (lax.broadcasted_iota(jnp.int32, (blk, blk), 0)
               == lax.broadcasted_iota(jnp.int32, (blk, blk), 1)).astype(BF16)
        selc = jnp.concatenate([_dot_nt(eye, sel[:, t * blk:(t + 1) * blk]) for t in range(n_blk)], axis=0)
        row_blk = lax.broadcasted_iota(jnp.int32, (s_len, LANES), 0) // blk
        aug_q = jnp.where(lane == row_blk, 0.0, (selc - 1.0) * MASK_BIAS)
        aug_k = jnp.where(lane == row_blk, 1.0, 0.0)

    q_aug = jnp.concatenate([q, aug_q.astype(BF16)], axis=1)
    k_aug = jnp.concatenate([kf.astype(BF16), aug_k.astype(BF16)], axis=1)
    tq = 2 * blk if s_len % (2 * blk) == 0 else blk
    row = lax.broadcasted_iota(jnp.int32, (tq, tq), 0)
    col = lax.broadcasted_iota(jnp.int32, (tq, tq), 1)
    for t in range(s_len // tq):
        n = (t + 1) * tq
        s = _dot_nt(q_aug[t * tq:(t + 1) * tq], k_aug[:n])
        diag = jnp.where(col <= row, s[:, n - tq:], NEG)
        s = diag if t == 0 else jnp.concatenate([s[:, :n - tq], diag], axis=1)
        p = jnp.exp2((s - jnp.max(s, axis=1, keepdims=True)).astype(BF16))
        ol = _dot(p, vb[:n])
        o_ref[t * tq:(t + 1) * tq, :] = (ol[:, :HEAD_DIM] * (1.0 / ol[:, HEAD_DIM:])).astype(o_ref.dtype)
    for cp in copies:
        cp.wait()


def _causal_attention(q, kv, *, mode, heads, q_col0, layer, depth, kv_out=None, c=None):
    b, s, _ = q.shape
    blk = MOBA_BLOCK
    n_blk = s // blk
    assert s % blk == 0 and n_blk <= LANES and kv.shape[1] == s
    seq_head = lambda col0: pl.BlockSpec((None, s, HEAD_DIM), lambda bi, h: (bi, 0, col0 + h))
    kv_head = lambda col0: pl.BlockSpec((1, s, HEAD_DIM), lambda bi, h: (bi, 0, col0 + h))
    in_specs = [seq_head(q_col0), kv_head(0), kv_head(heads)]
    args = [q, kv, kv]
    if mode == "fox":
        in_specs += [pl.BlockSpec((None, s, LANES), lambda bi, h: (bi, 0, 0))]
        args += [c]
    aliases = {}
    slots = tuple(range(layer, depth))
    if kv_out is not None:
        slots = (layer,)
        aliases = {len(args): 1}
        in_specs += [pl.BlockSpec(memory_space=pl.ANY)]
        args += [kv_out]
    return pl.pallas_call(
        functools.partial(_causal_attn_kernel, mode=mode, blk=blk, n_blk=n_blk, slots=slots),
        grid=(b, heads),
        in_specs=in_specs,
        out_specs=[seq_head(0), pl.BlockSpec(memory_space=pl.ANY)],
        scratch_shapes=[pltpu.SemaphoreType.DMA((len(slots), 2))],
        out_shape=[jax.ShapeDtypeStruct((b, s, heads * HEAD_DIM), BF16),
                   jax.ShapeDtypeStruct((depth, b, s, heads, 2, HEAD_DIM), F32)],
        input_output_aliases=aliases,
        compiler_params=_params(("arbitrary", "arbitrary")),
        name="attn_" + mode,
    )(*args)


def _mem_attn_kernel(q_ref, kv_ref, o_ref, kb, vb, *, heads):
    w = heads * HEAD_DIM

    @pl.when(pl.program_id(1) == 0)
    def _():
        kb[...] = kv_ref[:, :w].astype(BF16)
        vb[...] = kv_ref[:, w:].astype(BF16)

    for h in range(heads):
        s = _dot_nt(q_ref[:, _head_cols(h)], kb[:, _head_cols(h)])
        p = jnp.exp2(s - jnp.max(s, axis=1, keepdims=True))
        l = jnp.sum(p, axis=1, keepdims=True)
        o_ref[:, _head_cols(h)] = (_dot(p.astype(BF16), vb[:, _head_cols(h)]) * (1.0 / l)).astype(o_ref.dtype)


def _mem_attention(q, kv, *, heads, q_col_block, tq):
    b, s, _ = q.shape
    n_mem = kv.shape[1]
    w = heads * HEAD_DIM
    return pl.pallas_call(
        functools.partial(_mem_attn_kernel, heads=heads),
        grid=(b, s // tq),
        in_specs=[pl.BlockSpec((None, tq, HEAD_TILE), lambda bi, qi: (bi, qi, q_col_block)),
                  pl.BlockSpec((None, n_mem, 2 * w), lambda bi, qi: (bi, 0, 0))],
        out_specs=pl.BlockSpec((None, tq, w), lambda bi, qi: (bi, qi, 0)),
        out_shape=jax.ShapeDtypeStruct((b, s, w), BF16),
        scratch_shapes=[pltpu.VMEM((n_mem, w), BF16), pltpu.VMEM((n_mem, w), BF16)],
        compiler_params=_params(("arbitrary", "arbitrary")),
        name="attn_mem",
    )(q, kv)


def _merge_kernel(om_ref, of_ref, oc_ref, gate_ref, x_ref, wm_ref, wf_ref, wc_ref, wo_ref, o_ref):
    d = x_ref.shape[1]
    merged = gate_ref[:, 0:d].astype(F32) * _dot(om_ref[...], wm_ref[...])
    merged = merged + gate_ref[:, d:2 * d].astype(F32) * _dot(of_ref[...], wf_ref[...])
    merged = merged + gate_ref[:, 2 * d:3 * d].astype(F32) * _dot(oc_ref[...], wc_ref[...])
    o_ref[...] = x_ref[...] + _dot(merged.astype(BF16), wo_ref[...])


def _resident(w, layer):
    return pl.BlockSpec((None,) + w.shape[1:], lambda i: (layer, 0, 0), pipeline_mode=pl.Buffered(1))


def _merge_out(om, of, oc, gates, x, wm, wf, wc, wo, *, layer, tm):
    m, d = x.shape
    rows = lambda w: pl.BlockSpec((tm, w), lambda i: (i, 0))
    return pl.pallas_call(
        _merge_kernel,
        grid=(m // tm,),
        in_specs=[rows(om.shape[1]), rows(of.shape[1]), rows(oc.shape[1]), rows(gates.shape[1]), rows(d),
                  _resident(wm, layer), _resident(wf, layer), _resident(wc, layer), _resident(wo, layer)],
        out_specs=rows(d),
        out_shape=jax.ShapeDtypeStruct((m, d), F32),
        compiler_params=_params(("arbitrary",)),
        name="merge_out",
    )(om, of, oc, gates, x, wm, wf, wc, wo)


def _block_means(page_refs, km_out, t, *, chunks_per_seq, n_seq, heads, every_step):
    pages_per_blk = MOBA_BLOCK // LANES
    blks = len(page_refs) // pages_per_blk
    c = t % chunks_per_seq

    def body():
        used = chunks_per_seq * blks
        if used < km_out.shape[0]:
            km_out[used:, :] = jnp.zeros((km_out.shape[0] - used, km_out.shape[1]), F32)
        for bb in range(blks):
            per_head = []
            for h in range(heads):
                tot = jnp.zeros((1, HEAD_DIM), F32)
                for pp in range(pages_per_blk):
                    keys = page_refs[bb * pages_per_blk + pp][pl.ds(2 * h, LANES, stride=2 * heads), :]
                    tot = tot + jnp.sum(keys, axis=0, keepdims=True)
                per_head.append(tot)
            km_out[pl.ds(c * blks + bb, 1), :] = jnp.concatenate(per_head, axis=1) * (1.0 / MOBA_BLOCK)

    if every_step:
        body()
    else:
        pl.when(t < n_seq * chunks_per_seq)(body)


def _ffn_up_kernel(*refs, tiles_per_seq, per_row_state, kmean=None):
    if per_row_state:
        x_ref, g_ref, wa_ref, wb_ref, wc_ref, bc_ref, p0_ref, p1_ref, g_out, a_out, h_scr = refs
    elif kmean is None:
        x_ref, g_ref, wa_ref, wb_ref, wc_ref, bc_ref, g_out, tail_out, h_scr, carry_scr = refs
    else:
        n_pg = kmean["pages_per_step"]
        x_ref, g_ref, wa_ref, wb_ref, wc_ref, bc_ref = refs[1:7]
        page_refs = refs[7:7 + n_pg]
        g_out, tail_out, km_out, h_scr, carry_scr = refs[7 + n_pg:]
    i = pl.program_id(0)
    j = pl.program_id(1)

    @pl.when(j == 0)
    def _():
        h_scr[...] = _rms(x_ref[...], g_ref[...]).astype(BF16)

    w = wc_ref[...]
    tm = x_ref.shape[0]
    rc = min(ROW_CHUNK, tm)
    if not per_row_state:
        @pl.when(i % tiles_per_seq == 0)
        def _():
            carry_scr[j] = jnp.zeros(carry_scr.shape[1:], F32)
        prev = carry_scr[j]
    if kmean is not None:
        _block_means(page_refs, km_out, i * pl.num_programs(1) + j, chunks_per_seq=kmean["chunks_per_seq"],
                     n_seq=kmean["n_seq"], heads=kmean["heads"], every_step=kmean["every_step"])
    for r0 in range(0, tm, rc):
        rows = slice(r0, r0 + rc)
        h = h_scr[rows, :]
        a = _dot(h, wa_ref[...])
        b = _dot(h, wb_ref[...])
        if per_row_state:
            a1 = p1_ref[rows, :]
            a2 = p0_ref[rows, :]
            a_out[rows, :] = a
        else:
            row = lax.broadcasted_iota(jnp.int32, a.shape, 0)
            a1 = jnp.where(row == 0, prev[7:8, :], pltpu.roll(a, 1, 0))
            a2 = jnp.where(row == 0, prev[6:7, :], jnp.where(row == 1, prev[7:8, :], pltpu.roll(a, 2, 0)))
            prev = a[rc - SUBLANES:, :]
        a_conv = bc_ref[...] + a2 * w[0:1, :] + a1 * w[1:2, :] + a * w[2:3, :]
        g_out[rows, :] = (a_conv * jax.nn.sigmoid(a_conv) * b).astype(g_out.dtype)
    if not per_row_state:
        carry_scr[j] = prev
        tail_out[...] = prev


def _ffn_up(x, g, wa, wb, wconv, bconv, *, layer, tm, tn, seq_len, prev=None, kmean_src=None):
    m, d = x.shape
    f = wa.shape[2]
    nj = f // tn
    per_row_state = prev is not None
    tiles_per_seq = max(seq_len // tm, 1)
    col = lambda r: pl.BlockSpec((r, tn), lambda i, j, *_: (0, j))
    wcol = pl.BlockSpec((None, d, tn), lambda i, j, *_: (layer, 0, j))
    in_specs = [pl.BlockSpec((tm, d), lambda i, j, *_: (i, 0)),
                pl.BlockSpec((1, d), lambda i, j, *_: (0, 0)), wcol, wcol, col(SUBLANES), col(1)]
    args = [x, g, wa, wb, wconv, bconv]
    tile = pl.BlockSpec((tm, tn), lambda i, j, *_: (i, j))
    scratch = [pltpu.VMEM((tm, d), BF16)]
    kmean = None
    prefetch = []
    if per_row_state:
        in_specs += [tile, tile]
        args += [prev[0], prev[1]]
        out_specs = [tile, tile]
        out_shape = [jax.ShapeDtypeStruct((m, f), BF16), jax.ShapeDtypeStruct((m, f), F32)]
    else:
        out_specs = [tile, pl.BlockSpec((None, SUBLANES, tn), lambda i, j, *_: (i, 0, j))]
        out_shape = [jax.ShapeDtypeStruct((m, f), BF16), jax.ShapeDtypeStruct((m // tm, SUBLANES, f), F32)]
        scratch += [pltpu.VMEM((nj, SUBLANES, tn), F32)]
        if kmean_src is not None:
            cache_pages, page_table, layer_page0 = kmean_src
            n_seq, n_pages = page_table.shape
            pages_per_blk = MOBA_BLOCK // LANES
            steps = (m // tm) * nj
            chunks_per_seq = steps // n_seq
            assert chunks_per_seq >= 1
            pages_per_step = -(-n_pages // (chunks_per_seq * pages_per_blk)) * pages_per_blk
            chunks_per_seq = -(-n_pages // pages_per_step)
            assert chunks_per_seq * pages_per_step // pages_per_blk <= LANES
            kmean = dict(pages_per_step=pages_per_step, chunks_per_seq=chunks_per_seq, n_seq=n_seq, heads=MOBA_HEADS,
                         every_step=n_seq * chunks_per_seq == steps)
            seq_of = lambda i, j: jnp.minimum((i * nj + j) // chunks_per_seq, n_seq - 1)

            def page_map(u):
                def fn(i, j, pt):
                    pg = jnp.minimum(((i * nj + j) % chunks_per_seq) * pages_per_step + u, n_pages - 1)
                    return (layer_page0 + pt[seq_of(i, j) * n_pages + pg], 0, 0)
                return fn

            in_specs += [pl.BlockSpec((None,) + cache_pages.shape[1:], page_map(u)) for u in range(pages_per_step)]
            args += [cache_pages] * pages_per_step
            prefetch = [page_table.reshape(-1)]
            w_heads = MOBA_HEADS * HEAD_DIM
            out_specs += [pl.BlockSpec((None, LANES, w_heads), lambda i, j, pt: (seq_of(i, j), 0, 0))]
            out_shape += [jax.ShapeDtypeStruct((n_seq, LANES, w_heads), F32)]
    kern = functools.partial(_ffn_up_kernel, tiles_per_seq=tiles_per_seq, per_row_state=per_row_state, kmean=kmean)
    return pl.pallas_call(
        kern,
        grid_spec=pltpu.PrefetchScalarGridSpec(
            num_scalar_prefetch=len(prefetch), grid=(m // tm, nj), in_specs=in_specs, out_specs=out_specs,
            scratch_shapes=scratch),
        out_shape=out_shape, compiler_params=_params(("arbitrary", "arbitrary")), name="ffn_up",
    )(*prefetch, *args)


def _ffn_down_kernel(*refs, final):
    if final:
        g_ref, w_ref, x_ref, gf_ref, o_ref, y_ref = refs
    else:
        g_ref, w_ref, x_ref, o_ref = refs
    xo = x_ref[...] + _dot(g_ref[...], w_ref[...])
    o_ref[...] = xo
    if final:
        y_ref[...] = _rms(xo, gf_ref[...])


def _ffn_down(gact, w, x, g_final, *, layer, tm):
    m, d = x.shape
    f = w.shape[1]
    final = g_final is not None
    rows = lambda wd: pl.BlockSpec((tm, wd), lambda i: (i, 0))
    in_specs = [rows(f), _resident(w, layer), rows(d)]
    args = [gact, w, x]
    out_specs = [rows(d)]
    out_shape = [jax.ShapeDtypeStruct((m, d), F32)]
    if final:
        in_specs += [pl.BlockSpec((1, d), lambda i: (0, 0))]
        args += [g_final]
        out_specs += [rows(d)]
        out_shape += [jax.ShapeDtypeStruct((m, d), F32)]
    res = pl.pallas_call(
        functools.partial(_ffn_down_kernel, final=final),
        grid=(m // tm,), in_specs=in_specs, out_specs=out_specs, out_shape=out_shape,
        compiler_params=_params(("arbitrary",)), name="ffn_down",
    )(*args)
    return (res[0], res[1]) if final else (res[0], None)


def _head_mask(rows, width):
    r = lax.broadcasted_iota(jnp.int32, (rows, width), 0)
    c = lax.broadcasted_iota(jnp.int32, (rows, width), 1)
    return (c // HEAD_DIM) == r


def _page_heads(page_ref, kv, heads):
    return jnp.concatenate(
        [page_ref[pl.ds(2 * h + kv, LANES, stride=2 * heads), :] for h in range(heads)], axis=1)


def _fox_dec_kernel(pt_ref, q_ref, kvn_ref, lfn_ref, lf_ref, *refs, pages_per_step, heads, n_pages):
    page_refs = refs[:pages_per_step]
    o_ref, m_scr, l_scr, acc_scr, run_scr = refs[pages_per_step:]
    b = pl.program_id(0)
    c = pl.program_id(1)
    w = heads * HEAD_DIM
    hm = _head_mask(SUBLANES, w)
    qbd = jnp.where(hm, jnp.broadcast_to(q_ref[...], (SUBLANES, w)), 0.0)

    @pl.when(c == 0)
    def _():
        kvn = kvn_ref[...]
        m_scr[...] = jnp.sum(qbd * kvn[:, :w], axis=1, keepdims=True)
        l_scr[...] = jnp.ones_like(l_scr)
        acc_scr[...] = jnp.broadcast_to(kvn[:, w:], (SUBLANES, w))
        run_scr[...] = lfn_ref[...]

    r = lax.broadcasted_iota(jnp.int32, (LANES, LANES), 0)
    cc = lax.broadcasted_iota(jnp.int32, (LANES, LANES), 1)
    upper = (r > cc).astype(BF16)
    qb = qbd.astype(BF16)
    run = run_scr[...]
    pad = jnp.zeros((SUBLANES - heads, LANES), F32)
    ss = []
    for u in range(pages_per_step):
        pid = pt_ref[b * n_pages + (n_pages - 1 - (c * pages_per_step + u))]
        lf = jnp.concatenate([lf_ref[h, pl.ds(pid, 1), :] for h in range(heads)] + [pad], axis=0)
        s = _dot_nt(qb, _page_heads(page_refs[u], 0, heads).astype(BF16))
        suffix = sum(_dot(part, upper) for part in _split3(lf))
        ss.append(s + run + suffix)
        run = run + jnp.sum(lf, axis=1, keepdims=True)
    s = jnp.concatenate(ss, axis=1)
    m_old = m_scr[...]
    m_new = jnp.maximum(m_old, jnp.max(s, axis=1, keepdims=True))
    alpha = jnp.exp(m_old - m_new)
    p = jnp.exp(s - m_new).astype(BF16)
    l_scr[...] = alpha * l_scr[...] + jnp.sum(p.astype(F32), axis=1, keepdims=True)
    pv = sum(_dot(p[:, u * LANES:(u + 1) * LANES], _page_heads(page_refs[u], 1, heads).astype(BF16))
             for u in range(pages_per_step))
    acc_scr[...] = alpha * acc_scr[...] + pv
    m_scr[...] = m_new
    run_scr[...] = run

    @pl.when(c == pl.num_programs(1) - 1)
    def _():
        o = jnp.where(hm, acc_scr[...] * (1.0 / l_scr[...]), 0.0)
        o_ref[...] = jnp.sum(o, axis=0, keepdims=True).astype(o_ref.dtype)


def _fox_decode(q, kv_new, lf_new, cache_pages, cache_lf, page_table, layer, *, pages_per_step):
    db, n_pages = page_table.shape
    heads = FOX_HEADS
    w = heads * HEAD_DIM
    n_phys = cache_lf.shape[2]
    steps = n_pages // pages_per_step

    def page_map(u):
        def f(b, c, pt):
            return (layer * n_phys + pt[b * n_pages + (n_pages - 1 - (c * pages_per_step + u))], 0, 0)
        return f

    in_specs = [pl.BlockSpec((None, 1, w), lambda b, c, pt: (b, 0, 0)),
                pl.BlockSpec((None, 1, 2 * w), lambda b, c, pt: (b, 0, 0)),
                pl.BlockSpec((None, SUBLANES, 1), lambda b, c, pt: (b, 0, 0)),
                pl.BlockSpec((None, heads, n_phys, LANES), lambda b, c, pt: (layer, 0, 0, 0))]
    in_specs += [pl.BlockSpec((None, 2 * w, LANES), page_map(u)) for u in range(pages_per_step)]
    kern = functools.partial(_fox_dec_kernel, pages_per_step=pages_per_step, heads=heads, n_pages=n_pages)
    return pl.pallas_call(
        kern,
        grid_spec=pltpu.PrefetchScalarGridSpec(
            num_scalar_prefetch=1, grid=(db, steps), in_specs=in_specs,
            out_specs=pl.BlockSpec((None, 1, w), lambda b, c, pt: (b, 0, 0)),
            scratch_shapes=[pltpu.VMEM((SUBLANES, 1), F32), pltpu.VMEM((SUBLANES, 1), F32),
                            pltpu.VMEM((SUBLANES, w), F32), pltpu.VMEM((SUBLANES, 1), F32)]),
        out_shape=jax.ShapeDtypeStruct((db, 1, w), BF16),
        compiler_params=_params(("arbitrary", "arbitrary")),
        name="fox_decode",
    )(page_table.reshape(-1), q, kv_new, lf_new, cache_lf, *([cache_pages] * pages_per_step))


def _moba_topk_kernel(q_ref, km_ref, sel_ref, *, heads, n_blk):
    w = heads * HEAD_DIM
    hm = _head_mask(SUBLANES, w)
    qbd = jnp.where(hm, jnp.broadcast_to(q_ref[...], (SUBLANES, w)), 0.0)
    km = km_ref[...]
    bs = jnp.zeros((SUBLANES, LANES), F32)
    for qp in _split3(qbd):
        for kp in _split3(km):
            bs = bs + _dot_nt(qp, kp)
    lane = lax.broadcasted_iota(jnp.int32, (SUBLANES, LANES), 1)
    bs = jnp.where(lane < n_blk, bs, -jnp.inf)
    out = jnp.zeros((SUBLANES, LANES), jnp.int32)
    for t in range(MOBA_TOPK):
        mx = jnp.max(bs, axis=1, keepdims=True)
        idx = jnp.min(jnp.where(bs == mx, lane, LANES), axis=1, keepdims=True)
        out = jnp.where(lane == t, idx, out)
        bs = jnp.where(lane == idx, -jnp.inf, bs)
    sel_ref[...] = out


def _moba_topk(q, km, n_blk):
    db = q.shape[0]
    heads = MOBA_HEADS
    w = heads * HEAD_DIM
    assert MOBA_TOPK <= n_blk <= LANES
    return pl.pallas_call(
        functools.partial(_moba_topk_kernel, heads=heads, n_blk=n_blk),
        grid=(db,),
        in_specs=[pl.BlockSpec((None, 1, w), lambda b: (b, 0, 0)),
                  pl.BlockSpec((None, LANES, w), lambda b: (b, 0, 0))],
        out_specs=pl.BlockSpec((None, SUBLANES, LANES), lambda b: (b, 0, 0)),
        out_shape=jax.ShapeDtypeStruct((db, SUBLANES, LANES), jnp.int32),
        compiler_params=_params(("arbitrary",)),
        name="moba_topk",
    )(q, km)


def _moba_dec_kernel(sel_ref, pt_ref, q_ref, kn_ref, vn_ref, *refs, n_pages_sel):
    del sel_ref, pt_ref
    page_refs = refs[:n_pages_sel]
    o_ref = refs[n_pages_sel]
    q = q_ref[...]
    qb = jnp.broadcast_to(q, (SUBLANES, HEAD_DIM)).astype(BF16)
    s_self = jnp.sum(q * kn_ref[...], axis=1, keepdims=True)
    s = jnp.concatenate([_dot_nt(qb, page_refs[u][:, 0, :].astype(BF16)) for u in range(n_pages_sel)], axis=1)
    m = jnp.maximum(s_self, jnp.max(s, axis=1, keepdims=True))
    p_self = jnp.exp(s_self - m)
    p = jnp.exp(s - m).astype(BF16)
    l = p_self + jnp.sum(p.astype(F32), axis=1, keepdims=True)
    acc = p_self * vn_ref[...]
    for u in range(n_pages_sel):
        acc = acc + _dot(p[:, u * LANES:(u + 1) * LANES], page_refs[u][:, 1, :].astype(BF16))
    o_ref[...] = (acc * (1.0 / l))[0:1, :].astype(o_ref.dtype)


def _moba_decode(q, kv_new, sel, cache_rows, page_table, layer_page0):
    db, n_pages = page_table.shape
    heads = MOBA_HEADS
    pages_per_blk = MOBA_BLOCK // LANES
    n_pages_sel = MOBA_TOPK * pages_per_blk

    def page_map(u):
        jsel, pg = divmod(u, pages_per_blk)

        def f(b, h, sel_s, pt):
            blk = sel_s[(b * heads + h) * MOBA_TOPK + jsel]
            return (layer_page0 + pt[b * n_pages + blk * pages_per_blk + pg], h, 0, 0)
        return f

    one = lambda col0: pl.BlockSpec((None, 1, HEAD_DIM), lambda b, h, sel_s, pt: (b, 0, col0 + h))
    in_specs = [one(0), one(0), one(heads)]
    in_specs += [pl.BlockSpec((LANES, None, 2, HEAD_DIM), page_map(u)) for u in range(n_pages_sel)]
    kern = functools.partial(_moba_dec_kernel, n_pages_sel=n_pages_sel)
    return pl.pallas_call(
        kern,
        grid_spec=pltpu.PrefetchScalarGridSpec(
            num_scalar_prefetch=2, grid=(db, heads), in_specs=in_specs,
            out_specs=pl.BlockSpec((None, 1, HEAD_DIM), lambda b, h, sel_s, pt: (b, 0, h))),
        out_shape=jax.ShapeDtypeStruct((db, 1, heads * HEAD_DIM), BF16),
        compiler_params=_params(("arbitrary", "arbitrary")),
        name="moba_decode",
    )(sel.reshape(-1), page_table.reshape(-1), q, kv_new, kv_new, *([cache_rows] * n_pages_sel))


def _mem_dec_kernel(q_ref, kv_ref, o_ref, *, heads, n_mem):
    outs = []
    for h in range(heads):
        qb = jnp.broadcast_to(q_ref[:, _head_cols(h)], (SUBLANES, HEAD_DIM)).astype(BF16)
        k = kv_ref[pl.ds(h, n_mem, stride=2 * heads), :].astype(BF16)
        v = kv_ref[pl.ds(heads + h, n_mem, stride=2 * heads), :].astype(BF16)
        s = _dot_nt(qb, k)
        p = jnp.exp(s - jnp.max(s, axis=1, keepdims=True))
        l = jnp.sum(p, axis=1, keepdims=True)
        outs.append((_dot(p.astype(BF16), v) * (1.0 / l))[0:1, :])
    o_ref[...] = jnp.concatenate(outs, axis=1).astype(o_ref.dtype)


def _mem_decode(q, mem_rows, layer, *, q_col_block):
    db = q.shape[0]
    heads = MEM_HEADS
    n_mem = mem_rows.shape[1] // (2 * heads)
    return pl.pallas_call(
        functools.partial(_mem_dec_kernel, heads=heads, n_mem=n_mem),
        grid=(db,),
        in_specs=[pl.BlockSpec((None, 1, HEAD_TILE), lambda b: (b, 0, q_col_block)),
                  pl.BlockSpec((None, mem_rows.shape[1], HEAD_DIM), lambda b: (layer * db + b, 0, 0))],
        out_specs=pl.BlockSpec((None, 1, heads * HEAD_DIM), lambda b: (b, 0, 0)),
        out_shape=jax.ShapeDtypeStruct((db, 1, heads * HEAD_DIM), BF16),
        compiler_params=_params(("arbitrary",)),
        name="mem_decode",
    )(q, mem_rows)


def _rope_tables(pos):
    half = HEAD_DIM // 2
    inv_freq = ROPE_THETA ** (-jnp.arange(half, dtype=F32) / half)
    ang = pos.astype(F32)[:, None] * inv_freq[None, :]
    cos, sin = jnp.cos(ang), jnp.sin(ang)
    return jnp.concatenate([cos, cos], axis=-1), jnp.concatenate([-sin, sin], axis=-1)


def _pad_cols(a, n):
    return jnp.pad(a, ((0, 0), (0, n - a.shape[1])))


def _prep_w_in(w_in, d):
    mw, fw, cw = MOBA_HEADS * HEAD_DIM, FOX_HEADS * HEAD_DIM, MEM_HEADS * HEAD_DIM
    o_fl = 3 * mw + 3 * fw
    o_qc = o_fl + FOX_HEADS
    o_gl = o_qc + cw
    layers, k, n = w_in.shape
    assert n == o_gl + 3 * d and cw % W_IN_COLS == 0 and (3 * d) % W_IN_COLS == 0 and o_fl % W_IN_COLS == 0
    w_t = w_in.reshape(layers, k // LANES, LANES, n).transpose(3, 1, 0, 2)
    starts = list(range(0, o_fl, W_IN_COLS)) + list(range(o_qc, o_gl, W_IN_COLS))
    starts += [o_gl] * ((HEAD_TILE - cw) // W_IN_COLS)
    starts += list(range(o_gl, n, W_IN_COLS))
    w_main = _w_in_layout(w_t, starts, W_IN_COLS)
    w_fl = _w_in_layout(w_t, [o_fl], LANES)
    return w_main, w_fl


def _prep_layer(l, g_attn, b_f, g_mem, g_ffn, w_conv, b_conv, fp):
    f = w_conv.shape[2]
    return dict(
        g_attn=g_attn[l][None], b_f=_pad_cols(b_f[l][None], LANES), g_mem=g_mem[l][None], g_ffn=g_ffn[l][None],
        w_conv=jnp.pad(w_conv[l], ((0, SUBLANES - CONV_W), (0, fp - f))), b_conv=_pad_cols(b_conv[l][None], fp))


def kernel(x_prompt, x_sample, cache_moba_kv, cache_fox_kv, cache_fox_logf, cache_mem_kv, state_ffn_conv,
           page_table, mem_prompt, g_attn, w_in, b_f, w_br_moba, w_br_fox, w_br_mem, w_out, g_mem,
           w_mem_kv, g_ffn, w_up, w_conv, b_conv, w_down, g_final):
    bsz, seq, d = x_prompt.shape
    db = x_sample.shape[0]
    depth, n_phys, page = cache_moba_kv.shape[:3]
    n_pages = page_table.shape[1]
    past_len = n_pages * page
    n_mem = mem_prompt.shape[1]
    mw, fw, cw = MOBA_HEADS * HEAD_DIM, FOX_HEADS * HEAD_DIM, MEM_HEADS * HEAD_DIM
    assert page == LANES and db == SUBLANES and x_sample.shape[1] == 1
    ff_tile = 512
    tm = max(t for t in (1024, 512, 256) if seq % t == 0)
    tq = MOBA_BLOCK

    cos_p, sin_p = _rope_tables(jnp.arange(seq))
    cos_s, sin_s = _rope_tables(jnp.full((db,), past_len))
    gf = g_final[None]

    moba_t = jnp.transpose(cache_moba_kv, (0, 1, 2, 4, 3, 5))
    moba_pages = moba_t.reshape(depth * n_phys, page * 2 * MOBA_HEADS, HEAD_DIM)
    moba_rows = moba_t.reshape(depth * n_phys * page, MOBA_HEADS, 2, HEAD_DIM)
    fox_pages = jnp.transpose(cache_fox_kv, (0, 1, 2, 4, 3, 5)).reshape(depth * n_phys, page * 2 * FOX_HEADS, HEAD_DIM)
    lf_cache = jnp.transpose(cache_fox_logf, (0, 3, 1, 2))
    mem_rows = cache_mem_kv.reshape(depth * db, n_mem * 2 * MEM_HEADS, HEAD_DIM)

    f = w_down.shape[1]
    fp = -(-f // ff_tile) * ff_tile
    w_main, w_fl = _prep_w_in(w_in, d)
    wm, wf, wc = _cast_bf16(w_br_moba, mw), _cast_bf16(w_br_fox, fw), _cast_bf16(w_br_mem, cw)
    wo, w_mem = _cast_bf16(w_out, 512), _cast_bf16(w_mem_kv, 512)
    wa, wb = _cast_w_up(w_up, fp, 256)
    wdn = _cast_bf16(w_down, f // SUBLANES)
    moba_out = fox_out = None
    xp = x_prompt.reshape(bsz * seq, d)
    xs = x_sample.reshape(db, d)
    outs = {k: [] for k in ("p_logf", "p_mem", "p_conv", "s_moba", "s_fox", "s_logf", "s_conv")}
    yp = ys = None
    for l in range(depth):
        w = _prep_layer(l, g_attn, b_f, g_mem, g_ffn, w_conv, b_conv, fp)
        g_last = gf if l == depth - 1 else None

        mkv = _norm_mm(mem_prompt.reshape(bsz * n_mem, d), w["g_mem"], w_mem, layer=l, tm=min(512, bsz * n_mem))
        q_all, moba_kv, fox_kv, gates, logf, c = _in_proj(
            xp, w["g_attn"], w_main, w_fl, w["b_f"], cos_p, sin_p, layer=l, tm=tm, seq_len=seq, q_dtype=BF16,
            q_scale=SCALE * LOG2E)
        q3 = q_all.reshape(bsz, seq, 3 * HEAD_TILE)
        c3 = c.reshape(bsz, seq, LANES)
        o_m, moba_out = _causal_attention(q3, moba_kv.reshape(bsz, seq, 2 * mw), mode="moba", heads=MOBA_HEADS,
                                          q_col0=0, layer=l, depth=depth, kv_out=moba_out)
        o_f, fox_out = _causal_attention(q3, fox_kv.reshape(bsz, seq, 2 * fw), mode="fox", heads=FOX_HEADS,
                                         q_col0=MOBA_HEADS, layer=l, depth=depth, kv_out=fox_out, c=c3)
        o_c = _mem_attention(q3, mkv.reshape(bsz, n_mem, 2 * cw), heads=MEM_HEADS, q_col_block=2, tq=tq)
        xp = _merge_out(o_m.reshape(bsz * seq, mw), o_f.reshape(bsz * seq, fw), o_c.reshape(bsz * seq, cw),
                        gates, xp, wm, wf, wc, wo, layer=l, tm=256)
        gact, tails, km = _ffn_up(xp, w["g_ffn"], wa, wb, w["w_conv"], w["b_conv"], layer=l, tm=tm, tn=ff_tile,
                                  seq_len=seq, kmean_src=(moba_pages, page_table, l * n_phys))
        xp, yp = _ffn_down(gact, wdn, xp, g_last, layer=l, tm=256)
        outs["p_logf"].append(logf[:, :FOX_HEADS].reshape(bsz, seq, FOX_HEADS))
        outs["p_mem"].append(mkv.reshape(bsz, n_mem, 2, MEM_HEADS, HEAD_DIM))
        tiles_per_seq = seq // tm
        outs["p_conv"].append(tails[tiles_per_seq - 1::tiles_per_seq, SUBLANES - (CONV_W - 1):, :f])

        q_s, moba_s, fox_s, gates_s, logf_s, _ = _in_proj(
            xs, w["g_attn"], w_main, w_fl, w["b_f"], cos_s, sin_s, layer=l, tm=db, seq_len=1, q_dtype=F32,
            q_scale=SCALE)
        q_s3 = q_s.reshape(db, 1, 3 * HEAD_TILE)
        sel = _moba_topk(q_s3, km, past_len // MOBA_BLOCK)
        o_ms = _moba_decode(q_s3, moba_s.reshape(db, 1, 2 * mw), sel[:, :MOBA_HEADS, :MOBA_TOPK],
                            moba_rows, page_table, l * n_phys)
        o_fs = _fox_decode(q_s3[:, :, mw:mw + fw], fox_s.reshape(db, 1, 2 * fw),
                           logf_s[:, :SUBLANES].reshape(db, SUBLANES, 1), fox_pages, lf_cache, page_table,
                           l, pages_per_step=16)
        o_cs = _mem_decode(q_s3, mem_rows, l, q_col_block=2)
        xs = _merge_out(o_ms.reshape(db, mw), o_fs.reshape(db, fw), o_cs.reshape(db, cw), gates_s, xs,
                        wm, wf, wc, wo, layer=l, tm=db)
        prev = jnp.pad(state_ffn_conv[l], ((0, 0), (0, 0), (0, fp - f)))
        gact_s, a_s = _ffn_up(xs, w["g_ffn"], wa, wb, w["w_conv"], w["b_conv"],
                              layer=l, tm=db, tn=ff_tile, seq_len=1, prev=(prev[:, 0], prev[:, 1]))
        xs, ys = _ffn_down(gact_s, wdn, xs, g_last, layer=l, tm=db)
        outs["s_moba"].append(moba_s.reshape(db, 1, 2, MOBA_HEADS, HEAD_DIM))
        outs["s_fox"].append(fox_s.reshape(db, 1, 2, FOX_HEADS, HEAD_DIM))
        outs["s_logf"].append(logf_s[:, :FOX_HEADS].reshape(db, 1, FOX_HEADS))
        outs["s_conv"].append(jnp.stack([state_ffn_conv[l][:, 1], a_s[:, :f]], axis=1))

    st = lambda k: jnp.stack(outs[k])
    kv_order = (0, 1, 2, 4, 3, 5)
    return (yp.reshape(bsz, seq, d), ys.reshape(db, 1, d),
            jnp.transpose(moba_out, kv_order), jnp.transpose(fox_out, kv_order),
            st("p_logf"), st("p_mem"), st("p_conv"),
            st("s_moba"), st("s_fox"), st("s_logf"), st("s_conv"))
```
